```python
import math
import jax
import jax.numpy as jnp
from jax import lax
import numpy as np

D_MODEL = 4096
BATCH = 2
SEQ = 8192
DEPTH = 4

HEAD_DIM = 128
BLOCK = 128
GRID_W = 64
A_HEADS = 8
A_KV_HEADS = 2
WINDOW = 128
B_HEADS = 8
B_KV_HEADS = 2
ROPE_THETA = 10000.0
C_HEADS = 8
D_HEADS = 8
D_HALF = HEAD_DIM // 2
N_BRANCH = 4
BRANCH_W = 8 * HEAD_DIM
MIX_WIDTHS = (A_HEADS * HEAD_DIM, A_KV_HEADS * HEAD_DIM, A_KV_HEADS * HEAD_DIM,
              B_HEADS * HEAD_DIM, B_KV_HEADS * HEAD_DIM, B_KV_HEADS * HEAD_DIM,
              C_HEADS * HEAD_DIM, C_HEADS * HEAD_DIM, C_HEADS * HEAD_DIM, C_HEADS * HEAD_DIM,
              D_HEADS * HEAD_DIM, D_HEADS * HEAD_DIM, D_HEADS * HEAD_DIM)
MIX_COLS = sum(MIX_WIDTHS)
GATE_COLS = N_BRANCH * D_MODEL
IN_COLS = MIX_COLS + GATE_COLS
N_GROUPS = 4
EXPERTS_PER_GROUP = 4
N_EXPERTS = N_GROUPS * EXPERTS_PER_GROUP
TOP_K_IN_GROUP = 2
D_FF_EXPERT = 384
ALPHA = (2.0 * DEPTH) ** 0.25
BETA = (8.0 * DEPTH) ** -0.25
LN_EPS = 1e-5
RMS_EPS = 1e-6
NEG_INF = -1e30

kernel_name = "hybrid_gated_parallel_encoder"


def split_points():
    pts, acc = [], 0
    for w in MIX_WIDTHS[:-1]:
        acc += w
        pts.append(acc)
    return pts


def layer_norm(x, g, b):
    xf = x.astype(jnp.float32)
    mu = jnp.mean(xf, axis=-1, keepdims=True)
    var = jnp.mean(jnp.square(xf - mu), axis=-1, keepdims=True)
    return ((xf - mu) * lax.rsqrt(var + LN_EPS) * g + b).astype(x.dtype)


def rms_norm(x, g):
    xf = x.astype(jnp.float32)
    y = xf * lax.rsqrt(jnp.mean(jnp.square(xf), axis=-1, keepdims=True) + RMS_EPS) * g
    return y.astype(x.dtype)


def alibi_slopes(n):
    return 2.0 ** (-8.0 * jnp.arange(1, n + 1, dtype=jnp.float32) / n)


def windowed_sink_attention(q, k, v, sink, slopes):
    B, S, H, hd = q.shape
    G = k.shape[2]
    R = H // G
    nb = S // BLOCK
    qb = q.reshape(B, nb, BLOCK, G, R, hd)

    def neighbours(t):
        tb = t.reshape(B, nb, BLOCK, G, hd)
        tp = jnp.pad(tb, ((0, 0), (1, 1), (0, 0), (0, 0), (0, 0)))
        return jnp.concatenate([tp[:, :-2], tp[:, 1:-1], tp[:, 2:]], axis=2)

    kb, vb = neighbours(k), neighbours(v)
    s = jnp.einsum('bnqgrd,bnkgd->bngrqk', qb, kb).astype(jnp.float32) * (hd ** -0.5)
    blk = jnp.arange(nb)[:, None]
    qpos = blk * BLOCK + jnp.arange(BLOCK)[None, :]
    kpos = (blk - 1) * BLOCK + jnp.arange(3 * BLOCK)[None, :]
    dist = jnp.abs(qpos[:, :, None] - kpos[:, None, :])
    valid = (dist <= WINDOW) & (kpos[:, None, :] >= 0) & (kpos[:, None, :] < S)
    slope = slopes.reshape(G, R)[None, None, :, :, None, None]
    s = s - slope * dist.astype(jnp.float32)[None, :, None, None]
    s = jnp.where(valid[None, :, None, None], s, NEG_INF)
    sink_l = sink.astype(jnp.float32).reshape(G, R)[None, None, :, :, None, None]
    m = jnp.maximum(jnp.max(s, axis=-1, keepdims=True), sink_l)
    p = jnp.exp(s - m)
    p = p / (jnp.sum(p, axis=-1, keepdims=True) + jnp.exp(sink_l - m))
    o = jnp.einsum('bngrqk,bnkgd->bnqgrd', p.astype(v.dtype), vb)
    return o.reshape(B, S, H * hd)


def rope_half(x, ang):
    x1, x2 = jnp.split(x, 2, axis=-1)
    c = jnp.cos(ang)[:, None, :]
    s = jnp.sin(ang)[:, None, :]
    return jnp.concatenate([x1 * c - x2 * s, x2 * c + x1 * s], axis=-1)


def axial_rope(x):
    S = x.shape[1]
    rows = S // GRID_W
    row_id = jnp.repeat(jnp.arange(rows), GRID_W).astype(jnp.float32)
    col_id = (jnp.arange(rows * GRID_W) % GRID_W).astype(jnp.float32)
    n = HEAD_DIM // 4
    inv = ROPE_THETA ** (-jnp.arange(n, dtype=jnp.float32) / n)
    xf = x.astype(jnp.float32)
    xr, xc = jnp.split(xf, 2, axis=-1)
    out = jnp.concatenate([rope_half(xr, row_id[:, None] * inv),
                           rope_half(xc, col_id[:, None] * inv)], axis=-1)
    return out.astype(x.dtype)


def dense_gqa_blocks(q, k, v):
    B, S, H, hd = q.shape
    G = k.shape[2]
    R = H // G
    nb = S // BLOCK
    qb = jnp.moveaxis(q.reshape(B, nb, BLOCK, G, R, hd), 1, 0)

    def one(qblk):
        s = jnp.einsum('bqgrd,bkgd->bgrqk', qblk, k).astype(jnp.float32) * (hd ** -0.5)
        p = jax.nn.softmax(s, axis=-1)
        return jnp.einsum('bgrqk,bkgd->bqgrd', p.astype(v.dtype), v)

    o = lax.map(one, qb)
    return jnp.moveaxis(o, 0, 1).reshape(B, S, H * hd)


def diff_attention_blocks(q, k, v, lam, slopes):
    B, S, H, _, dh = q.shape
    nb = S // BLOCK
    qb = jnp.moveaxis(q.reshape(B, nb, BLOCK, H, 2, dh), 1, 0)
    kpos = jnp.arange(S)

    def one(args):
        qblk, i = args
        s = jnp.einsum('bqhcd,bkhcd->bchqk', qblk, k).astype(jnp.float32) * (dh ** -0.5)
        qpos = i * BLOCK + jnp.arange(BLOCK)
        dist = jnp.abs(qpos[:, None] - kpos[None, :]).astype(jnp.float32)
        s = s - slopes[None, None, :, None, None] * dist
        p = jax.nn.softmax(s, axis=-1)
        a = p[:, 0] - lam * p[:, 1]
        return jnp.einsum('bhqk,bkhd->bqhd', a.astype(v.dtype), v)

    o = lax.map(one, (qb, jnp.arange(nb)))
    return jnp.moveaxis(o, 0, 1).reshape(B, S, H, v.shape[-1])


def retention_chunkwise(q, k, v, log_gamma, include_diag):
    B, H, S, d = q.shape
    nb = S // BLOCK
    pos = jnp.arange(BLOCK, dtype=jnp.float32)
    diff = pos[:, None] - pos[None, :]
    mask = (diff >= 0) if include_diag else (diff > 0)
    lg = log_gamma[:, None, None]
    inner_decay = jnp.where(mask[None], jnp.exp(lg * jnp.where(mask, diff, 0.0)[None]), 0.0)
    q_decay = jnp.exp(log_gamma[:, None] * (pos + 1.0))
    k_decay = jnp.exp(log_gamma[:, None] * (BLOCK - 1.0 - pos))
    chunk_decay = jnp.exp(log_gamma * BLOCK)
    to_chunks = lambda t: jnp.moveaxis(t.astype(jnp.float32).reshape(B, H, nb, BLOCK, d), 2, 0)
    qc, kc, vc = to_chunks(q), to_chunks(k), to_chunks(v)

    def step(state, inp):
        qi, ki, vi = inp
        inner = jnp.einsum('bhqd,bhkd->bhqk', qi, ki) * inner_decay[None]
        o = (jnp.einsum('bhqk,bhkv->bhqv', inner, vi)
             + jnp.einsum('bhqd,bhdv->bhqv', qi, state) * q_decay[None, :, :, None])
        state = (state * chunk_decay[None, :, None, None]
                 + jnp.einsum('bhkd,bhkv->bhdv', ki * k_decay[None, :, :, None], vi))
        return state, o

    init = jnp.zeros((B, H, d, d), jnp.float32)
    _, o = lax.scan(step, init, (qc, kc, vc))
    return jnp.moveaxis(o, 0, 2).reshape(B, H, S, d)


def mixer_retention(qc, kc, vc, gc, dec_f, dec_b, norm_g):
    B, S, _ = qc.shape
    heads = lambda t: t.reshape(B, S, C_HEADS, HEAD_DIM).transpose(0, 2, 1, 3)
    q, k, v = heads(qc), heads(kc) * (HEAD_DIM ** -0.5), heads(vc)
    lg_f = jnp.log1p(-jnp.exp(dec_f.astype(jnp.float32)))
    lg_b = jnp.log1p(-jnp.exp(dec_b.astype(jnp.float32)))
    fwd = retention_chunkwise(q, k, v, lg_f, True)
    bwd = jnp.flip(retention_chunkwise(jnp.flip(q, 2), jnp.flip(k, 2), jnp.flip(v, 2), lg_b, False), 2)
    r = (fwd + bwd).transpose(0, 2, 1, 3)
    mu = jnp.mean(r, axis=-1, keepdims=True)
    var = jnp.mean(jnp.square(r - mu), axis=-1, keepdims=True)
    r = ((r - mu) * lax.rsqrt(var + LN_EPS)).reshape(B, S, C_HEADS * HEAD_DIM) * norm_g
    return (jax.nn.silu(gc.astype(jnp.float32)) * r).astype(qc.dtype)


def hierarchical_moe(x, w_rg, b_rg, w_re, b_re, w_gate, w_up, w_down):
    B, S, D = x.shape
    t = x.reshape(B * S, D)
    N = t.shape[0]
    g_logits = (t @ w_rg).astype(jnp.float32) + b_rg
    g_prob = jax.nn.softmax(g_logits, axis=-1)
    g_top = jnp.argmax(g_logits, axis=-1)
    g_w = jnp.take_along_axis(g_prob, g_top[:, None], axis=-1)
    e_logits = ((t @ w_re).astype(jnp.float32) + b_re).reshape(N, N_GROUPS, EXPERTS_PER_GROUP)
    e_in = jnp.take_along_axis(e_logits, g_top[:, None, None], axis=1)[:, 0]
    top_v, top_i = lax.top_k(e_in, TOP_K_IN_GROUP)
    top_w = jax.nn.softmax(top_v, axis=-1) * g_w
    expert_id = g_top[:, None] * EXPERTS_PER_GROUP + top_i
    combine = jnp.sum(jax.nn.one_hot(expert_id, N_EXPERTS, dtype=jnp.float32) * top_w[..., None], axis=1)
    h = jax.nn.silu(jnp.einsum('nd,edf->nef', t, w_gate)) * jnp.einsum('nd,edf->nef', t, w_up)
    y = jnp.einsum('nef,efd->nd', h * combine[:, :, None].astype(h.dtype), w_down)
    return y.reshape(B, S, D)


def setup_inputs(seed: int = 0) -> dict:
    key = jax.random.key(seed)
    ks = jax.random.split(key, 26)
    L, D = DEPTH, D_MODEL
    nrm = lambda k, shape, scale: jax.random.normal(k, shape, jnp.float32) * scale
    gain = lambda k, shape: 1.0 + 0.02 * jax.random.normal(k, shape, jnp.float32)
    dec0 = (-5.0 - jnp.arange(C_HEADS, dtype=jnp.float32)) * math.log(2.0)
    return {
        'x': nrm(ks[0], (BATCH, SEQ, D), 1.0),
        'w_in': nrm(ks[1], (L, D, IN_COLS), D ** -0.5),
        'attn_sink': nrm(ks[2], (L, A_HEADS), 0.5),
        'qk_norm_q': gain(ks[3], (L, HEAD_DIM)),
        'qk_norm_k': gain(ks[4], (L, HEAD_DIM)),
        'ret_decay_fwd': dec0 + nrm(ks[5], (L, C_HEADS), 0.05),
        'ret_decay_bwd': dec0 + nrm(ks[6], (L, C_HEADS), 0.05),
        'ret_norm_g': gain(ks[7], (L, C_HEADS * HEAD_DIM)),
        'diff_lambda_q1': nrm(ks[8], (L, D_HALF), 0.1),
        'diff_lambda_k1': nrm(ks[9], (L, D_HALF), 0.1),
        'diff_lambda_q2': nrm(ks[10], (L, D_HALF), 0.1),
        'diff_lambda_k2': nrm(ks[11], (L, D_HALF), 0.1),
        'diff_subln_g': gain(ks[12], (L, HEAD_DIM)),
        'w_branch': nrm(ks[13], (L, N_BRANCH, BRANCH_W, D), BETA * BRANCH_W ** -0.5),
        'w_out': nrm(ks[14], (L, D, D), BETA * D ** -0.5),
        'ln_mix_g': gain(ks[15], (L, D)),
        'ln_mix_b': nrm(ks[16], (L, D), 0.02),
        'router_group': nrm(ks[17], (L, D, N_GROUPS), D ** -0.5),
        'router_group_b': nrm(ks[18], (L, N_GROUPS), 0.01),
        'router_expert': nrm(ks[19], (L, D, N_EXPERTS), D ** -0.5),
        'router_expert_b': nrm(ks[20], (L, N_EXPERTS), 0.01),
        'expert_w_gate': nrm(ks[21], (L, N_EXPERTS, D, D_FF_EXPERT), D ** -0.5),
        'expert_w_up': nrm(ks[22], (L, N_EXPERTS, D, D_FF_EXPERT), D ** -0.5),
        'expert_w_down': nrm(ks[23], (L, N_EXPERTS, D_FF_EXPERT, D), BETA * D_FF_EXPERT ** -0.5),
        'ln_ffn_g': gain(ks[24], (L, D)),
        'ln_ffn_b': nrm(ks[25], (L, D), 0.02),
    }


def reference(x, w_in, attn_sink, qk_norm_q, qk_norm_k, ret_decay_fwd, ret_decay_bwd, ret_norm_g,
              diff_lambda_q1, diff_lambda_k1, diff_lambda_q2, diff_lambda_k2, diff_subln_g,
              w_branch, w_out, ln_mix_g, ln_mix_b, router_group, router_group_b, router_expert,
              router_expert_b, expert_w_gate, expert_w_up, expert_w_down, ln_ffn_g, ln_ffn_b):
    B, S, _ = x.shape
    pts = split_points()
    slopes = alibi_slopes(8)
    for l in range(DEPTH):
        h = x @ w_in[l, :, :MIX_COLS]
        qa, ka, va, qb, kb, vb, qc, kc, vc, gc, qd, kd, vd = jnp.split(h, pts, axis=-1)
        o_a = windowed_sink_attention(qa.reshape(B, S, A_HEADS, HEAD_DIM),
                                      ka.reshape(B, S, A_KV_HEADS, HEAD_DIM),
                                      va.reshape(B, S, A_KV_HEADS, HEAD_DIM),
                                      attn_sink[l], slopes)
        q_b = axial_rope(rms_norm(qb.reshape(B, S, B_HEADS, HEAD_DIM), qk_norm_q[l]))
        k_b = axial_rope(rms_norm(kb.reshape(B, S, B_KV_HEADS, HEAD_DIM), qk_norm_k[l]))
        o_b = dense_gqa_blocks(q_b, k_b, vb.reshape(B, S, B_KV_HEADS, HEAD_DIM))
        o_c = mixer_retention(qc, kc, vc, gc, ret_decay_fwd[l], ret_decay_bwd[l], ret_norm_g[l])
        lam_init = 0.8 - 0.6 * math.exp(-0.3 * l)
        lam = (jnp.exp(jnp.sum(diff_lambda_q1[l].astype(jnp.float32) * diff_lambda_k1[l]))
               - jnp.exp(jnp.sum(diff_lambda_q2[l].astype(jnp.float32) * diff_lambda_k2[l])) + lam_init)
        o_d = diff_attention_blocks(qd.reshape(B, S, D_HEADS, 2, D_HALF),
                                    kd.reshape(B, S, D_HEADS, 2, D_HALF),
                                    vd.reshape(B, S, D_HEADS, HEAD_DIM), lam, slopes)
        o_d = (rms_norm(o_d, diff_subln_g[l]) * (1.0 - lam_init)).reshape(B, S, D_HEADS * HEAD_DIM)
        merged = jnp.zeros_like(x)
        for i, o in enumerate((o_a, o_b, o_c, o_d)):
            lo = MIX_COLS + i * D_MODEL
            gate = jax.nn.sigmoid(x @ w_in[l, :, lo:lo + D_MODEL])
            merged = merged + gate * (o @ w_branch[l, i])
        x = layer_norm(ALPHA * x + merged @ w_out[l], ln_mix_g[l], ln_mix_b[l])
        y = hierarchical_moe(x, router_group[l], router_group_b[l], router_expert[l], router_expert_b[l],
                             expert_w_gate[l], expert_w_up[l], expert_w_down[l])
        x = layer_norm(ALPHA * x + y, ln_ffn_g[l], ln_ffn_b[l])
    return x
```

```python
import functools
import math

import jax
import jax.numpy as jnp
from jax import lax
from jax.experimental import pallas as pl
from jax.experimental.pallas import tpu as pltpu

F32 = jnp.float32
BF16 = jnp.bfloat16

HEAD_DIM = 128
BLOCK = 128
GRID_W = 64
N_HEADS = 8
KV_HEADS = 2
D_HALF = HEAD_DIM // 2
N_BRANCH = 4
BRANCH_W = N_HEADS * HEAD_DIM
ROPE_THETA = 10000.0
N_GROUPS = 4
EXPERTS_PER_GROUP = 4
N_EXPERTS = N_GROUPS * EXPERTS_PER_GROUP
LN_EPS = 1e-5
RMS_EPS = 1e-6
NEG_BIG = -1e30
LOG2E = 1.4426950408889634
LANES = 128
VMEM_LIMIT = 56 * 1024 * 1024

COL_AQ, COL_AK, COL_AV = 0, 1024, 1280
COL_BQ, COL_BK, COL_BV = 1536, 2560, 2816
COL_CQ, COL_CK, COL_CV, COL_CG = 3072, 4096, 5120, 6144
COL_DQ, COL_DK, COL_DV = 7168, 8192, 9216
MIX_COLS = 10240


def _params(sem):
    return pltpu.CompilerParams(dimension_semantics=sem, vmem_limit_bytes=VMEM_LIMIT)


def _nt_dot(a, b):
    return lax.dot_general(a, b, (((1,), (1,)), ((), ())), preferred_element_type=F32)


def _tn_dot(a, b):
    return lax.dot_general(a, b, (((0,), (0,)), ((), ())), preferred_element_type=F32)


def _dot(a, b):
    return jnp.dot(a, b, preferred_element_type=F32)


def _mm_kernel(x_ref, w_ref, o_ref):
    o_ref[...] = _dot(x_ref[...], w_ref[...]).astype(o_ref.dtype)


def in_proj(x_bf, w_in_bf, layer, n_cols, tm, tn):
    n, k = x_bf.shape
    return pl.pallas_call(
        _mm_kernel,
        grid=(n // tm, n_cols // tn),
        in_specs=[pl.BlockSpec((tm, k), lambda i, j: (i, 0)),
                  pl.BlockSpec((None, k, tn), lambda i, j: (layer, 0, j))],
        out_specs=pl.BlockSpec((tm, tn), lambda i, j: (i, j)),
        out_shape=jax.ShapeDtypeStruct((n, n_cols), BF16),
        compiler_params=_params(("parallel", "parallel")),
        name="in_proj",
    )(x_bf, w_in_bf)


def _attn_a_kernel(sink_ref, q_ref, kp_ref, kc_ref, kn_ref, vp_ref, vc_ref, vn_ref, o_ref, *, slopes, nb):
    i = pl.program_id(1)
    rep = N_HEADS // KV_HEADS
    rows = rep * BLOCK
    row = lax.broadcasted_iota(jnp.int32, (rows, 3 * BLOCK), 0)
    col = lax.broadcasted_iota(jnp.int32, (rows, 3 * BLOCK), 1)
    rel = (row % BLOCK) - (col - BLOCK)
    dist = jnp.abs(rel)
    valid = (dist <= BLOCK) & ((col >= BLOCK) | (i > 0)) & ((col < 2 * BLOCK) | (i < nb - 1))
    distf = dist.astype(F32)
    rowc = lax.broadcasted_iota(jnp.int32, (rows, 1), 0)
    for g in range(KV_HEADS):
        q = jnp.concatenate([q_ref[:, (g * rep + r) * HEAD_DIM:(g * rep + r + 1) * HEAD_DIM]
                             for r in range(rep)], axis=0)
        ks = slice(g * HEAD_DIM, (g + 1) * HEAD_DIM)
        k = jnp.concatenate([kp_ref[:, ks], kc_ref[:, ks], kn_ref[:, ks]], axis=0)
        v = jnp.concatenate([vp_ref[:, ks], vc_ref[:, ks], vn_ref[:, ks]], axis=0)
        slope = jnp.full((rows, 1), slopes[g * rep], F32)
        sink = jnp.full((rows, 1), sink_ref[g * rep], F32)
        for r in range(1, rep):
            slope = jnp.where(rowc >= r * BLOCK, slopes[g * rep + r], slope)
            sink = jnp.where(rowc >= r * BLOCK, sink_ref[g * rep + r], sink)
        s = _nt_dot(q, k) - slope * distf
        s = jnp.where(valid, s, NEG_BIG)
        m = jnp.maximum(jnp.max(s, axis=-1, keepdims=True), sink)
        p = jnp.exp2(s - m)
        denom = jnp.sum(p, axis=-1, keepdims=True) + jnp.exp2(sink - m)
        o = _dot(p.astype(BF16), v) / denom
        for r in range(rep):
            hh = g * rep + r
            o_ref[:, hh * HEAD_DIM:(hh + 1) * HEAD_DIM] = o[r * BLOCK:(r + 1) * BLOCK].astype(o_ref.dtype)


def attn_a(h, sink_log2, batch, seq):
    n = h.shape[0]
    nb = seq // BLOCK
    slopes = tuple(LOG2E * 2.0 ** (-(i + 1.0)) for i in range(N_HEADS))
    kvw = KV_HEADS * HEAD_DIM
    kcol, vcol = COL_AK // kvw, COL_AV // kvw

    def prev(b, i):
        return b * nb + jnp.maximum(i - 1, 0)

    def nxt(b, i):
        return b * nb + jnp.minimum(i + 1, nb - 1)

    return pl.pallas_call(
        functools.partial(_attn_a_kernel, slopes=slopes, nb=nb),
        grid=(batch, nb),
        in_specs=[pl.BlockSpec(memory_space=pltpu.SMEM),
                  pl.BlockSpec((BLOCK, BRANCH_W), lambda b, i: (b * nb + i, COL_AQ // BRANCH_W)),
                  pl.BlockSpec((BLOCK, kvw), lambda b, i: (prev(b, i), kcol)),
                  pl.BlockSpec((BLOCK, kvw), lambda b, i: (b * nb + i, kcol)),
                  pl.BlockSpec((BLOCK, kvw), lambda b, i: (nxt(b, i), kcol)),
                  pl.BlockSpec((BLOCK, kvw), lambda b, i: (prev(b, i), vcol)),
                  pl.BlockSpec((BLOCK, kvw), lambda b, i: (b * nb + i, vcol)),
                  pl.BlockSpec((BLOCK, kvw), lambda b, i: (nxt(b, i), vcol))],
        out_specs=pl.BlockSpec((BLOCK, BRANCH_W), lambda b, i: (b * nb + i, 0)),
        out_shape=jax.ShapeDtypeStruct((n, BRANCH_W), BF16),
        compiler_params=_params(("parallel", "parallel")),
        name="attn_a",
    )(sink_log2, h, h, h, h, h, h, h)


def _qk_prep_kernel(x_ref, g_ref, cos_ref, sin_ref, o_ref):
    lane = lax.broadcasted_iota(jnp.int32, (x_ref.shape[0], HEAD_DIM), 1)
    first_quarter = (lane % (HEAD_DIM // 2)) < (HEAD_DIM // 4)
    cos = cos_ref[...]
    sin = sin_ref[...]
    for hh in range(x_ref.shape[1] // HEAD_DIM):
        sl = slice(hh * HEAD_DIM, (hh + 1) * HEAD_DIM)
        x = x_ref[:, sl].astype(F32)
        y = x * lax.rsqrt(jnp.mean(x * x, axis=-1, keepdims=True) + RMS_EPS) * g_ref[:, sl]
        partner = jnp.where(first_quarter, pltpu.roll(y, HEAD_DIM - HEAD_DIM // 4, 1), pltpu.roll(y, HEAD_DIM // 4, 1))
        o_ref[:, sl] = (y * cos + partner * sin).astype(o_ref.dtype)


def qk_prep(h, gains, cos_t, sin_t, batch, seq, tm):
    n = h.shape[0]
    width = (N_HEADS + KV_HEADS) * HEAD_DIM
    cw = 2 * HEAD_DIM
    nblk = seq // tm
    return pl.pallas_call(
        _qk_prep_kernel,
        grid=(batch, nblk, width // cw),
        in_specs=[pl.BlockSpec((tm, cw), lambda b, i, j: (b * nblk + i, COL_BQ // cw + j)),
                  pl.BlockSpec((1, cw), lambda b, i, j: (0, j)),
                  pl.BlockSpec((tm, HEAD_DIM), lambda b, i, j: (i, 0)),
                  pl.BlockSpec((tm, HEAD_DIM), lambda b, i, j: (i, 0))],
        out_specs=pl.BlockSpec((tm, cw), lambda b, i, j: (b * nblk + i, j)),
        out_shape=jax.ShapeDtypeStruct((n, width), BF16),
        compiler_params=_params(("parallel", "parallel", "parallel")),
        name="qk_prep",
    )(h, gains, cos_t, sin_t)


def _attn_b_kernel(q_ref, k_ref, v_ref, o_ref, m_ref, l_ref, acc_ref, *, tk):
    tq = q_ref.shape[0]
    rep = q_ref.shape[1] // HEAD_DIM
    seq = k_ref.shape[0]
    q = jnp.concatenate([q_ref[:, r * HEAD_DIM:(r + 1) * HEAD_DIM] for r in range(rep)], axis=0)
    m_ref[...] = jnp.full(m_ref.shape, NEG_BIG, F32)
    l_ref[...] = jnp.zeros(l_ref.shape, F32)
    acc_ref[...] = jnp.zeros(acc_ref.shape, F32)

    def body(c, carry):
        off = pl.multiple_of(c * tk, tk)
        k = k_ref[pl.ds(off, tk), :]
        v = v_ref[pl.ds(off, tk), :]
        s = _nt_dot(q, k)
        m_old = m_ref[...]
        m_new = jnp.maximum(m_old, jnp.max(s, axis=-1, keepdims=True))
        a = jnp.exp2(m_old - m_new)
        p = jnp.exp2(s - m_new)
        l_ref[...] = a * l_ref[...] + jnp.sum(p, axis=-1, keepdims=True)
        acc_ref[...] = a * acc_ref[...] + _dot(p.astype(BF16), v)
        m_ref[...] = m_new
        return carry

    lax.fori_loop(0, seq // tk, body, 0)
    o = acc_ref[...] / l_ref[...]
    for r in range(rep):
        o_ref[:, r * HEAD_DIM:(r + 1) * HEAD_DIM] = o[r * tq:(r + 1) * tq].astype(o_ref.dtype)


def attn_b(qk, h, batch, seq, tq, tk):
    n = h.shape[0]
    rep = N_HEADS // KV_HEADS
    nq = seq // tq
    rows = rep * tq
    return pl.pallas_call(
        functools.partial(_attn_b_kernel, tk=tk),
        grid=(batch, KV_HEADS, nq),
        in_specs=[pl.BlockSpec((tq, rep * HEAD_DIM), lambda b, g, i: (b * nq + i, g)),
                  pl.BlockSpec((seq, HEAD_DIM), lambda b, g, i: (b, N_HEADS + g)),
                  pl.BlockSpec((seq, HEAD_DIM), lambda b, g, i: (b, COL_BV // HEAD_DIM + g))],
        out_specs=pl.BlockSpec((tq, rep * HEAD_DIM), lambda b, g, i: (b * nq + i, g)),
        out_shape=jax.ShapeDtypeStruct((n, BRANCH_W), BF16),
        scratch_shapes=[pltpu.VMEM((rows, 1), F32), pltpu.VMEM((rows, 1), F32), pltpu.VMEM((rows, HEAD_DIM), F32)],
        compiler_params=_params(("parallel", "parallel", "parallel")),
        name="attn_b",
    )(qk, qk, h)


def _retention_kernel(*refs, final):
    if final:
        (q_ref, k_ref, v_ref, inner_ref, qdec_ref, kdec_ref, cdec_ref, prev_ref, gate_ref, ng_ref,
         o_ref, state_ref) = refs
    else:
        q_ref, k_ref, v_ref, inner_ref, qdec_ref, kdec_ref, cdec_ref, o_ref, state_ref = refs

    @pl.when(pl.program_id(1) == 0)
    def _():
        state_ref[...] = jnp.zeros(state_ref.shape, F32)

    for hh in range(N_HEADS):
        sl = slice(hh * HEAD_DIM, (hh + 1) * HEAD_DIM)
        q = q_ref[:, sl]
        k = k_ref[:, sl]
        v = v_ref[:, sl]
        state = state_ref[hh]
        inner = _nt_dot(q, k) * inner_ref[hh]
        o = _dot(inner.astype(BF16), v) + _dot(q, state.astype(BF16)) * qdec_ref[hh]
        kd = (k.astype(F32) * kdec_ref[hh]).astype(BF16)
        state_ref[hh] = state * cdec_ref[hh] + _tn_dot(kd, v)
        if final:
            r = o + prev_ref[:, sl]
            mu = jnp.mean(r, axis=-1, keepdims=True)
            d = r - mu
            var = jnp.mean(d * d, axis=-1, keepdims=True)
            rn = d * lax.rsqrt(var + LN_EPS) * ng_ref[:, sl]
            gate = gate_ref[:, sl].astype(F32)
            silu = gate / (1.0 + jnp.exp(-gate))
            o_ref[:, sl] = (silu * rn).astype(o_ref.dtype)
        else:
            o_ref[:, sl] = o


def retention(h, tabs_f, tabs_b, norm_g, batch, seq):
    n = h.shape[0]
    nb = seq // BLOCK
    w = BRANCH_W
    tab_specs = [pl.BlockSpec((N_HEADS, BLOCK, BLOCK), lambda b, c: (0, 0, 0)),
                 pl.BlockSpec((N_HEADS, BLOCK, HEAD_DIM), lambda b, c: (0, 0, 0)),
                 pl.BlockSpec((N_HEADS, BLOCK, HEAD_DIM), lambda b, c: (0, 0, 0)),
                 pl.BlockSpec((N_HEADS, 1, HEAD_DIM), lambda b, c: (0, 0, 0))]

    def specs(rowmap):
        return [pl.BlockSpec((BLOCK, w), lambda b, c: (rowmap(b, c), COL_CQ // w)),
                pl.BlockSpec((BLOCK, w), lambda b, c: (rowmap(b, c), COL_CK // w)),
                pl.BlockSpec((BLOCK, w), lambda b, c: (rowmap(b, c), COL_CV // w))]

    fmap = lambda b, c: b * nb + c
    bmap = lambda b, c: b * nb + (nb - 1 - c)
    scratch = [pltpu.VMEM((N_HEADS, HEAD_DIM, HEAD_DIM), F32)]
    o_f = pl.pallas_call(
        functools.partial(_retention_kernel, final=False),
        grid=(batch, nb),
        in_specs=specs(fmap) + tab_specs,
        out_specs=pl.BlockSpec((BLOCK, w), lambda b, c: (fmap(b, c), 0)),
        out_shape=jax.ShapeDtypeStruct((n, w), F32),
        scratch_shapes=scratch,
        compiler_params=_params(("parallel", "arbitrary")),
        name="retention_fwd",
    )(h, h, h, *tabs_f)
    return pl.pallas_call(
        functools.partial(_retention_kernel, final=True),
        grid=(batch, nb),
        in_specs=specs(bmap) + tab_specs + [
            pl.BlockSpec((BLOCK, w), lambda b, c: (bmap(b, c), 0)),
            pl.BlockSpec((BLOCK, w), lambda b, c: (bmap(b, c), COL_CG // w)),
            pl.BlockSpec((1, w), lambda b, c: (0, 0))],
        out_specs=pl.BlockSpec((BLOCK, w), lambda b, c: (bmap(b, c), 0)),
        out_shape=jax.ShapeDtypeStruct((n, w), BF16),
        scratch_shapes=scratch,
        compiler_params=_params(("parallel", "arbitrary")),
        name="retention_bwd",
    )(h, h, h, *tabs_b, o_f, h, norm_g)


def retention_tables(dec, scale, backward):
    lg = jnp.log1p(-jnp.exp(dec.astype(F32)))
    pos = jnp.arange(BLOCK, dtype=F32)
    diff = pos[:, None] - pos[None, :]
    if backward:
        diff = -diff
        mask = diff > 0
        qexp = BLOCK - pos
        kexp = pos
    else:
        mask = diff >= 0
        qexp = pos + 1.0
        kexp = BLOCK - 1.0 - pos
    inner = jnp.where(mask[None], jnp.exp(lg[:, None, None] * jnp.where(mask, diff, 0.0)[None]), 0.0) * scale
    qdec = jnp.broadcast_to(jnp.exp(lg[:, None] * qexp)[:, :, None], (N_HEADS, BLOCK, HEAD_DIM))
    kdec = jnp.broadcast_to((jnp.exp(lg[:, None] * kexp) * scale)[:, :, None], (N_HEADS, BLOCK, HEAD_DIM))
    cdec = jnp.broadcast_to(jnp.exp(lg * BLOCK)[:, None, None], (N_HEADS, 1, HEAD_DIM))
    return inner, qdec, kdec, cdec


def _attn_d_kernel(slope_ref, lam_ref, g_ref, q_ref, k_ref, v_ref, o_ref, m_ref, l_ref, acc_ref, *, tk, lam_init):
    tq = q_ref.shape[0]
    seq = k_ref.shape[0]
    slope = slope_ref[pl.program_id(1)]
    q0 = pl.program_id(2) * tq
    lane = lax.broadcasted_iota(jnp.int32, (tq, HEAD_DIM), 1)
    q = q_ref[...]
    zero = jnp.zeros_like(q)
    qs = (jnp.where(lane < D_HALF, q, zero), jnp.where(lane >= D_HALF, q, zero))
    rel = (lax.broadcasted_iota(jnp.int32, (tq, tk), 0) - lax.broadcasted_iota(jnp.int32, (tq, tk), 1)).astype(F32)
    m_ref[...] = jnp.full(m_ref.shape, NEG_BIG, F32)
    l_ref[...] = jnp.zeros(l_ref.shape, F32)
    acc_ref[...] = jnp.zeros(acc_ref.shape, F32)

    def body(c, carry):
        off = pl.multiple_of(c * tk, tk)
        k = k_ref[pl.ds(off, tk), :]
        v = v_ref[pl.ds(off, tk), :]
        bias = slope * jnp.abs(rel + (q0 - off).astype(F32))
        for half in range(2):
            s = _nt_dot(qs[half], k) - bias
            m_old = m_ref[half]
            m_new = jnp.maximum(m_old, jnp.max(s, axis=-1, keepdims=True))
            a = jnp.exp2(m_old - m_new)
            p = jnp.exp2(s - m_new)
            l_ref[half] = a * l_ref[half] + jnp.sum(p, axis=-1, keepdims=True)
            acc_ref[half] = a * acc_ref[half] + _dot(p.astype(BF16), v)
            m_ref[half] = m_new
        return carry

    lax.fori_loop(0, seq // tk, body, 0)
    lp = lam_ref[...]
    lam = (jnp.exp(jnp.sum(lp[0:1] * lp[1:2], axis=-1, keepdims=True))
           - jnp.exp(jnp.sum(lp[2:3] * lp[3:4], axis=-1, keepdims=True)) + lam_init)
    o = acc_ref[0] / l_ref[0] - lam * (acc_ref[1] / l_ref[1])
    y = o * lax.rsqrt(jnp.mean(o * o, axis=-1, keepdims=True) + RMS_EPS) * g_ref[...]
    o_ref[...] = (y * (1.0 - lam_init)).astype(o_ref.dtype)


def attn_d(h, slopes_log2, lam_params, subln_g, lam_init, batch, seq, tq, tk):
    n = h.shape[0]
    nq = seq // tq
    return pl.pallas_call(
        functools.partial(_attn_d_kernel, tk=tk, lam_init=lam_init),
        grid=(batch, N_HEADS, nq),
        in_specs=[pl.BlockSpec(memory_space=pltpu.SMEM),
                  pl.BlockSpec((4, D_HALF), lambda b, hh, i: (0, 0)),
                  pl.BlockSpec((1, HEAD_DIM), lambda b, hh, i: (0, 0)),
                  pl.BlockSpec((tq, HEAD_DIM), lambda b, hh, i: (b * nq + i, COL_DQ // HEAD_DIM + hh)),
                  pl.BlockSpec((seq, HEAD_DIM), lambda b, hh, i: (b, COL_DK // HEAD_DIM + hh)),
                  pl.BlockSpec((seq, HEAD_DIM), lambda b, hh, i: (b, COL_DV // HEAD_DIM + hh))],
        out_specs=pl.BlockSpec((tq, HEAD_DIM), lambda b, hh, i: (b * nq + i, hh)),
        out_shape=jax.ShapeDtypeStruct((n, BRANCH_W), BF16),
        scratch_shapes=[pltpu.VMEM((2, tq, 1), F32), pltpu.VMEM((2, tq, 1), F32), pltpu.VMEM((2, tq, HEAD_DIM), F32)],
        compiler_params=_params(("parallel", "parallel", "parallel")),
        name="attn_d",
    )(slopes_log2, lam_params, subln_g, h, h, h)


def _gate_merge_kernel(x_ref, wg_ref, o_ref, wb_ref, out_ref, acc_ref):
    i = pl.program_id(2)
    gate = _dot(x_ref[...], wg_ref[...])
    contrib = _dot(o_ref[...], wb_ref[...]) / (1.0 + jnp.exp(-gate))

    @pl.when(i == 0)
    def _():
        acc_ref[...] = contrib

    @pl.when(i > 0)
    def _():
        acc_ref[...] += contrib

    @pl.when(i == N_BRANCH - 1)
    def _():
        out_ref[...] = acc_ref[...].astype(out_ref.dtype)


def gate_merge(x_bf, w_in_bf, o_stack, w_branch_bf, layer, tm, tn):
    n, d = x_bf.shape
    gcol0 = MIX_COLS // tn
    per = d // tn
    return pl.pallas_call(
        _gate_merge_kernel,
        grid=(n // tm, d // tn, N_BRANCH),
        in_specs=[pl.BlockSpec((tm, d), lambda m, j, i: (m, 0)),
                  pl.BlockSpec((None, d, tn), lambda m, j, i: (layer, 0, gcol0 + i * per + j)),
                  pl.BlockSpec((None, tm, BRANCH_W), lambda m, j, i: (i, m, 0)),
                  pl.BlockSpec((None, None, BRANCH_W, tn), lambda m, j, i: (layer, i, 0, j))],
        out_specs=pl.BlockSpec((tm, tn), lambda m, j, i: (m, j)),
        out_shape=jax.ShapeDtypeStruct((n, d), BF16),
        scratch_shapes=[pltpu.VMEM((tm, tn), F32)],
        compiler_params=_params(("parallel", "parallel", "arbitrary")),
        name="gate_merge",
    )(x_bf, w_in_bf, o_stack, w_branch_bf)


def _layer_norm(y, g, b):
    mu = jnp.mean(y, axis=-1, keepdims=True)
    d = y - mu
    var = jnp.mean(d * d, axis=-1, keepdims=True)
    return d * lax.rsqrt(var + LN_EPS) * g + b


def _out_ln_kernel(m_ref, w_ref, x_ref, g_ref, b_ref, of_ref, ob_ref, *, alpha):
    k = pl.program_id(1)
    part = _dot(m_ref[...], w_ref[...])

    @pl.when(k == 0)
    def _():
        of_ref[...] = part

    @pl.when(k > 0)
    def _():
        of_ref[...] += part

    @pl.when(k == pl.num_programs(1) - 1)
    def _():
        out = _layer_norm(alpha * x_ref[...] + of_ref[...], g_ref[...], b_ref[...])
        of_ref[...] = out
        ob_ref[...] = out.astype(ob_ref.dtype)


def out_proj_ln(merged, w_out_bf, x, g, b, layer, alpha, tm, tk):
    n, d = x.shape
    return pl.pallas_call(
        functools.partial(_out_ln_kernel, alpha=alpha),
        grid=(n // tm, d // tk),
        in_specs=[pl.BlockSpec((tm, tk), lambda m, k: (m, k)),
                  pl.BlockSpec((None, tk, d), lambda m, k: (layer, k, 0)),
                  pl.BlockSpec((tm, d), lambda m, k: (m, 0)),
                  pl.BlockSpec((None, 1, d), lambda m, k: (layer, 0, 0)),
                  pl.BlockSpec((None, 1, d), lambda m, k: (layer, 0, 0))],
        out_specs=[pl.BlockSpec((tm, d), lambda m, k: (m, 0)), pl.BlockSpec((tm, d), lambda m, k: (m, 0))],
        out_shape=[jax.ShapeDtypeStruct((n, d), F32), jax.ShapeDtypeStruct((n, d), BF16)],
        compiler_params=_params(("parallel", "arbitrary")),
        name="out_proj_ln",
    )(merged, w_out_bf, x, g, b)


def _router_kernel(x_ref, whi_ref, wlo_ref, b_ref, o_ref):
    x = x_ref[...]
    x_hi = x.astype(BF16)
    x_lo = (x - x_hi.astype(F32)).astype(BF16)
    whi = whi_ref[...]
    logits = _dot(x_hi, whi) + _dot(x_lo, whi) + _dot(x_hi, wlo_ref[...]) + b_ref[...]
    lane = lax.broadcasted_iota(jnp.int32, logits.shape, 1)
    big = jnp.int32(LANES)
    gmask = lane < N_GROUPS
    gl = jnp.where(gmask, logits, NEG_BIG)
    gmax = jnp.max(gl, axis=-1, keepdims=True)
    g_top = jnp.min(jnp.where(gmask & (gl == gmax), lane, big), axis=-1, keepdims=True)
    g_w = 1.0 / jnp.sum(jnp.where(gmask, jnp.exp(gl - gmax), 0.0), axis=-1, keepdims=True)
    lo = N_GROUPS + g_top * EXPERTS_PER_GROUP
    emask = (lane >= lo) & (lane < lo + EXPERTS_PER_GROUP)
    e1 = jnp.where(emask, logits, NEG_BIG)
    v1 = jnp.max(e1, axis=-1, keepdims=True)
    i1 = jnp.min(jnp.where(emask & (e1 == v1), lane, big), axis=-1, keepdims=True)
    emask2 = emask & (lane != i1)
    e2 = jnp.where(emask2, logits, NEG_BIG)
    v2 = jnp.max(e2, axis=-1, keepdims=True)
    i2 = jnp.min(jnp.where(emask2 & (e2 == v2), lane, big), axis=-1, keepdims=True)
    t = jnp.exp(v2 - v1)
    w1 = g_w / (1.0 + t)
    w2 = g_w * t / (1.0 + t)
    comb = jnp.where(lane == i1, w1, 0.0) + jnp.where(lane == i2, w2, 0.0)
    ids = jnp.where(lane == 0, (i1 - N_GROUPS).astype(F32), 0.0) + jnp.where(lane == 1, (i2 - N_GROUPS).astype(F32), 0.0)
    o_ref[...] = comb + ids


def router(x, w_hi, w_lo, bias, tm):
    n, d = x.shape
    return pl.pallas_call(
        _router_kernel,
        grid=(n // tm,),
        in_specs=[pl.BlockSpec((tm, d), lambda m: (m, 0)),
                  pl.BlockSpec((d, LANES), lambda m: (0, 0)),
                  pl.BlockSpec((d, LANES), lambda m: (0, 0)),
                  pl.BlockSpec((1, LANES), lambda m: (0, 0))],
        out_specs=pl.BlockSpec((tm, LANES), lambda m: (m, 0)),
        out_shape=jax.ShapeDtypeStruct((n, LANES), F32),
        compiler_params=_params(("parallel",)),
        name="router",
    )(x, w_hi, w_lo, bias)


def _moe_dense_kernel(xb_ref, comb_ref, wg_ref, wu_ref, wd_ref, x_ref, g_ref, b_ref, of_ref, ob_ref, *, alpha):
    e = pl.program_id(1)
    xb = xb_ref[...]
    gate = _dot(xb, wg_ref[...])
    up = _dot(xb, wu_ref[...])
    lane = lax.broadcasted_iota(jnp.int32, comb_ref.shape, 1)
    c = jnp.sum(jnp.where(lane == e + N_GROUPS, comb_ref[...], 0.0), axis=-1, keepdims=True)
    hmid = (gate / (1.0 + jnp.exp(-gate))) * up * c
    y = _dot(hmid.astype(BF16), wd_ref[...])

    @pl.when(e == 0)
    def _():
        of_ref[...] = y

    @pl.when(e > 0)
    def _():
        of_ref[...] += y

    @pl.when(e == N_EXPERTS - 1)
    def _():
        out = _layer_norm(alpha * x_ref[...] + of_ref[...], g_ref[...], b_ref[...])
        of_ref[...] = out
        ob_ref[...] = out.astype(ob_ref.dtype)


def moe_dense_ln(x_bf, comb, wg_bf, wu_bf, wd_bf, x, g, b, layer, alpha, tm):
    n, d = x.shape
    f = wg_bf.shape[-1]
    return pl.pallas_call(
        functools.partial(_moe_dense_kernel, alpha=alpha),
        grid=(n // tm, N_EXPERTS),
        in_specs=[pl.BlockSpec((tm, d), lambda m, e: (m, 0)),
                  pl.BlockSpec((tm, LANES), lambda m, e: (m, 0)),
                  pl.BlockSpec((None, None, d, f), lambda m, e: (layer, e, 0, 0)),
                  pl.BlockSpec((None, None, d, f), lambda m, e: (layer, e, 0, 0)),
                  pl.BlockSpec((None, None, f, d), lambda m, e: (layer, e, 0, 0)),
                  pl.BlockSpec((tm, d), lambda m, e: (m, 0)),
                  pl.BlockSpec((None, 1, d), lambda m, e: (layer, 0, 0)),
                  pl.BlockSpec((None, 1, d), lambda m, e: (layer, 0, 0))],
        out_specs=[pl.BlockSpec((tm, d), lambda m, e: (m, 0)), pl.BlockSpec((tm, d), lambda m, e: (m, 0))],
        out_shape=[jax.ShapeDtypeStruct((n, d), F32), jax.ShapeDtypeStruct((n, d), BF16)],
        compiler_params=_params(("parallel", "arbitrary")),
        name="moe_dense_ln",
    )(x_bf, comb, wg_bf, wu_bf, wd_bf, x, g, b)


def _rope_tables(seq):
    t = jnp.arange(seq)
    row_id = (t // GRID_W).astype(F32)
    col_id = (t % GRID_W).astype(F32)
    nfreq = HEAD_DIM // 4
    inv = ROPE_THETA ** (-jnp.arange(nfreq, dtype=F32) / nfreq)
    ar = row_id[:, None] * inv
    ac = col_id[:, None] * inv
    cos = jnp.concatenate([jnp.cos(ar), jnp.cos(ar), jnp.cos(ac), jnp.cos(ac)], axis=-1)
    sin = jnp.concatenate([-jnp.sin(ar), jnp.sin(ar), -jnp.sin(ac), jnp.sin(ac)], axis=-1)
    return cos, sin


def _tile(n, pref):
    t = min(n, pref)
    while n % t:
        t //= 2
    return t


def kernel(x, w_in, attn_sink, qk_norm_q, qk_norm_k, ret_decay_fwd, ret_decay_bwd, ret_norm_g,
           diff_lambda_q1, diff_lambda_k1, diff_lambda_q2, diff_lambda_k2, diff_subln_g,
           w_branch, w_out, ln_mix_g, ln_mix_b, router_group, router_group_b, router_expert,
           router_expert_b, expert_w_gate, expert_w_up, expert_w_down, ln_ffn_g, ln_ffn_b):
    batch, seq, d = x.shape
    depth = w_in.shape[0]
    n = batch * seq
    alpha = (2.0 * depth) ** 0.25
    assert w_in.shape[2] == MIX_COLS + N_BRANCH * d and seq % BLOCK == 0 and seq % GRID_W == 0

    colscale = jnp.ones((w_in.shape[2],), F32)
    colscale = colscale.at[COL_AQ:COL_AQ + BRANCH_W].set(HEAD_DIM ** -0.5 * LOG2E)
    colscale = colscale.at[COL_DQ:COL_DQ + BRANCH_W].set(D_HALF ** -0.5 * LOG2E)
    w_in_bf = (w_in * colscale).astype(BF16)
    w_branch_bf = w_branch.astype(BF16)
    w_out_bf = w_out.astype(BF16)
    wg_bf = expert_w_gate.astype(BF16)
    wu_bf = expert_w_up.astype(BF16)
    wd_bf = expert_w_down.astype(BF16)
    ln_mix_g3, ln_mix_b3 = ln_mix_g[:, None, :], ln_mix_b[:, None, :]
    ln_ffn_g3, ln_ffn_b3 = ln_ffn_g[:, None, :], ln_ffn_b[:, None, :]

    cos_t, sin_t = _rope_tables(seq)
    slopes_log2 = jnp.asarray([LOG2E * 2.0 ** (-(i + 1.0)) for i in range(N_HEADS)], F32)

    tm_big = _tile(n, 1024)
    xf = x.reshape(n, d)
    xb = xf.astype(BF16)
    for l in range(depth):
        h = in_proj(xb, w_in_bf, l, MIX_COLS, tm_big, 1024)
        o_a = attn_a(h, attn_sink[l].astype(F32) * LOG2E, batch, seq)
        gains = jnp.concatenate([jnp.tile(qk_norm_q[l], N_HEADS) * (HEAD_DIM ** -0.5 * LOG2E),
                                 jnp.tile(qk_norm_k[l], KV_HEADS)])[None, :]
        qk = qk_prep(h, gains, cos_t, sin_t, batch, seq, _tile(seq, 512))
        o_b = attn_b(qk, h, batch, seq, _tile(seq, 256), _tile(seq, 512))
        tabs_f = retention_tables(ret_decay_fwd[l], HEAD_DIM ** -0.5, False)
        tabs_b = retention_tables(ret_decay_bwd[l], HEAD_DIM ** -0.5, True)
        o_c = retention(h, tabs_f, tabs_b, ret_norm_g[l][None, :], batch, seq)
        lam_init = 0.8 - 0.6 * math.exp(-0.3 * l)
        lam_params = jnp.stack([diff_lambda_q1[l], diff_lambda_k1[l], diff_lambda_q2[l], diff_lambda_k2[l]]).astype(F32)
        o_d = attn_d(h, slopes_log2, lam_params, diff_subln_g[l][None, :], lam_init, batch, seq,
                     _tile(seq, 512), _tile(seq, 512))
        merged = gate_merge(xb, w_in_bf, jnp.stack([o_a, o_b, o_c, o_d]), w_branch_bf, l, tm_big, 512)
        xf, xb = out_proj_ln(merged, w_out_bf, xf, ln_mix_g3, ln_mix_b3, l, alpha, _tile(n, 256), 1024)
        w_r = jnp.concatenate([router_group[l], router_expert[l]], axis=1)
        w_r = jnp.pad(w_r, ((0, 0), (0, LANES - w_r.shape[1])))
        w_r_hi = w_r.astype(BF16)
        w_r_lo = (w_r - w_r_hi.astype(F32)).astype(BF16)
        b_r = jnp.pad(jnp.concatenate([router_group_b[l], router_expert_b[l]]), (0, LANES - N_GROUPS - N_EXPERTS))[None, :]
        comb = router(xf, w_r_hi, w_r_lo, b_r, _tile(n, 512))
        xf, xb = moe_dense_ln(xb, comb, wg_bf, wu_bf, wd_bf, xf, ln_ffn_g3, ln_ffn_b3, l, alpha, _tile(n, 256))
    return xf.reshape(batch, seq, d)
```

```python
import functools
import math

import jax
import jax.numpy as jnp
from jax import lax
from jax.experimental import pallas as pl
from jax.experimental.pallas import tpu as pltpu

F32 = jnp.float32
BF16 = jnp.bfloat16

HEAD_DIM = 128
BLOCK = 128
GRID_W = 64
N_HEADS = 8
KV_HEADS = 2
D_HALF = HEAD_DIM // 2
N_BRANCH = 4
BRANCH_W = N_HEADS * HEAD_DIM
ROPE_THETA = 10000.0
N_GROUPS = 4
EXPERTS_PER_GROUP = 4
N_EXPERTS = N_GROUPS * EXPERTS_PER_GROUP
LN_EPS = 1e-5
RMS_EPS = 1e-6
NEG_BIG = -1e30
LOG2E = 1.4426950408889634
LANES = 128
VMEM_LIMIT = 56 * 1024 * 1024

COL_AQ, COL_AK, COL_AV = 0, 1024, 1280
COL_BQ, COL_BK, COL_BV = 1536, 2560, 2816
COL_CQ, COL_CK, COL_CV, COL_CG = 3072, 4096, 5120, 6144
COL_DQ, COL_DK, COL_DV = 7168, 8192, 9216
MIX_COLS = 10240


def _params(sem):
    return pltpu.CompilerParams(dimension_semantics=sem, vmem_limit_bytes=VMEM_LIMIT)


def _nt_dot(a, b):
    return lax.dot_general(a, b, (((1,), (1,)), ((), ())), preferred_element_type=F32)


def _tn_dot(a, b):
    return lax.dot_general(a, b, (((0,), (0,)), ((), ())), preferred_element_type=F32)


def _dot(a, b):
    return jnp.dot(a, b, preferred_element_type=F32)


def _mm_kernel(x_ref, w_ref, o_ref):
    o_ref[...] = _dot(x_ref[...], w_ref[...]).astype(o_ref.dtype)


def in_proj(x_bf, w_in_bf, layer, n_cols, tm, tn):
    n, k = x_bf.shape
    return pl.pallas_call(
        _mm_kernel,
        grid=(n // tm, n_cols // tn),
        in_specs=[pl.BlockSpec((tm, k), lambda i, j: (i, 0)),
                  pl.BlockSpec((None, k, tn), lambda i, j: (layer, 0, j))],
        out_specs=pl.BlockSpec((tm, tn), lambda i, j: (i, j)),
        out_shape=jax.ShapeDtypeStruct((n, n_cols), BF16),
        compiler_params=_params(("parallel", "parallel")),
        name="in_proj",
    )(x_bf, w_in_bf)


def _attn_a_kernel(sink_ref, q_ref, kp_ref, kc_ref, kn_ref, vp_ref, vc_ref, vn_ref, o_ref, *, slopes, nb):
    i = pl.program_id(1)
    rep = N_HEADS // KV_HEADS
    rows = rep * BLOCK
    row = lax.broadcasted_iota(jnp.int32, (rows, 3 * BLOCK), 0)
    col = lax.broadcasted_iota(jnp.int32, (rows, 3 * BLOCK), 1)
    rel = (row % BLOCK) - (col - BLOCK)
    dist = jnp.abs(rel)
    valid = (dist <= BLOCK) & ((col >= BLOCK) | (i > 0)) & ((col < 2 * BLOCK) | (i < nb - 1))
    distf = dist.astype(F32)
    rowc = lax.broadcasted_iota(jnp.int32, (rows, 1), 0)
    for g in range(KV_HEADS):
        q = jnp.concatenate([q_ref[:, (g * rep + r) * HEAD_DIM:(g * rep + r + 1) * HEAD_DIM]
                             for r in range(rep)], axis=0)
        ks = slice(g * HEAD_DIM, (g + 1) * HEAD_DIM)
        k = jnp.concatenate([kp_ref[:, ks], kc_ref[:, ks], kn_ref[:, ks]], axis=0)
        v = jnp.concatenate([vp_ref[:, ks], vc_ref[:, ks], vn_ref[:, ks]], axis=0)
        slope = jnp.full((rows, 1), slopes[g * rep], F32)
        sink = jnp.full((rows, 1), sink_ref[g * rep], F32)
        for r in range(1, rep):
            slope = jnp.where(rowc >= r * BLOCK, slopes[g * rep + r], slope)
            sink = jnp.where(rowc >= r * BLOCK, sink_ref[g * rep + r], sink)
        s = _nt_dot(q, k) - slope * distf
        s = jnp.where(valid, s, NEG_BIG)
        m = jnp.maximum(jnp.max(s, axis=-1, keepdims=True), sink)
        p = jnp.exp2(s - m)
        denom = jnp.sum(p, axis=-1, keepdims=True) + jnp.exp2(sink - m)
        o = _dot(p.astype(BF16), v) / denom
        for r in range(rep):
            hh = g * rep + r
            o_ref[:, hh * HEAD_DIM:(hh + 1) * HEAD_DIM] = o[r * BLOCK:(r + 1) * BLOCK].astype(o_ref.dtype)


def attn_a(h, sink_log2, batch, seq):
    n = h.shape[0]
    nb = seq // BLOCK
    slopes = tuple(LOG2E * 2.0 ** (-(i + 1.0)) for i in range(N_HEADS))
    kvw = KV_HEADS * HEAD_DIM
    kcol, vcol = COL_AK // kvw, COL_AV // kvw

    def prev(b, i):
        return b * nb + jnp.maximum(i - 1, 0)

    def nxt(b, i):
        return b * nb + jnp.minimum(i + 1, nb - 1)

    return pl.pallas_call(
        functools.partial(_attn_a_kernel, slopes=slopes, nb=nb),
        grid=(batch, nb),
        in_specs=[pl.BlockSpec(memory_space=pltpu.SMEM),
                  pl.BlockSpec((BLOCK, BRANCH_W), lambda b, i: (b * nb + i, COL_AQ // BRANCH_W)),
                  pl.BlockSpec((BLOCK, kvw), lambda b, i: (prev(b, i), kcol)),
                  pl.BlockSpec((BLOCK, kvw), lambda b, i: (b * nb + i, kcol)),
                  pl.BlockSpec((BLOCK, kvw), lambda b, i: (nxt(b, i), kcol)),
                  pl.BlockSpec((BLOCK, kvw), lambda b, i: (prev(b, i), vcol)),
                  pl.BlockSpec((BLOCK, kvw), lambda b, i: (b * nb + i, vcol)),
                  pl.BlockSpec((BLOCK, kvw), lambda b, i: (nxt(b, i), vcol))],
        out_specs=pl.BlockSpec((BLOCK, BRANCH_W), lambda b, i: (b * nb + i, 0)),
        out_shape=jax.ShapeDtypeStruct((n, BRANCH_W), BF16),
        compiler_params=_params(("parallel", "parallel")),
        name="attn_a",
    )(sink_log2, h, h, h, h, h, h, h)


def _norm_rope(x, g, cos, sin, first_quarter):
    y = x * lax.rsqrt(jnp.mean(x * x, axis=-1, keepdims=True) + RMS_EPS) * g
    partner = jnp.where(first_quarter, pltpu.roll(y, HEAD_DIM - HEAD_DIM // 4, 1), pltpu.roll(y, HEAD_DIM // 4, 1))
    return y * cos + partner * sin


def _qk_prep_kernel(q0_ref, q1_ref, k_ref, g_ref, cos_ref, sin_ref, qt_ref, ko_ref):
    lane = lax.broadcasted_iota(jnp.int32, (k_ref.shape[0], HEAD_DIM), 1)
    first_quarter = (lane % (HEAD_DIM // 2)) < (HEAD_DIM // 4)
    cos = cos_ref[...]
    sin = sin_ref[...]
    half = N_HEADS // 2
    for hh in range(N_HEADS):
        src = q0_ref if hh < half else q1_ref
        sl = slice((hh % half) * HEAD_DIM, (hh % half + 1) * HEAD_DIM)
        gs = slice(hh * HEAD_DIM, (hh + 1) * HEAD_DIM)
        y = _norm_rope(src[:, sl].astype(F32), g_ref[:, gs], cos, sin, first_quarter)
        qt_ref[gs, :] = y.T.astype(qt_ref.dtype)
    for hh in range(KV_HEADS):
        sl = slice(hh * HEAD_DIM, (hh + 1) * HEAD_DIM)
        gs = slice((N_HEADS + hh) * HEAD_DIM, (N_HEADS + hh + 1) * HEAD_DIM)
        ko_ref[:, sl] = _norm_rope(k_ref[:, sl].astype(F32), g_ref[:, gs], cos, sin, first_quarter).astype(ko_ref.dtype)


def qk_prep(h, gains, cos_t, sin_t, batch, seq, tm):
    n = h.shape[0]
    qw = BRANCH_W // 2
    kw = KV_HEADS * HEAD_DIM
    nblk = seq // tm
    return pl.pallas_call(
        _qk_prep_kernel,
        grid=(batch, nblk),
        in_specs=[pl.BlockSpec((tm, qw), lambda b, i: (b * nblk + i, COL_BQ // qw)),
                  pl.BlockSpec((tm, qw), lambda b, i: (b * nblk + i, COL_BQ // qw + 1)),
                  pl.BlockSpec((tm, kw), lambda b, i: (b * nblk + i, COL_BK // kw)),
                  pl.BlockSpec((1, BRANCH_W + kw), lambda b, i: (0, 0)),
                  pl.BlockSpec((tm, HEAD_DIM), lambda b, i: (i, 0)),
                  pl.BlockSpec((tm, HEAD_DIM), lambda b, i: (i, 0))],
        out_specs=[pl.BlockSpec((BRANCH_W, tm), lambda b, i: (0, b * nblk + i)),
                   pl.BlockSpec((tm, kw), lambda b, i: (b * nblk + i, 0))],
        out_shape=[jax.ShapeDtypeStruct((BRANCH_W, n), BF16), jax.ShapeDtypeStruct((n, kw), BF16)],
        compiler_params=_params(("parallel", "parallel")),
        name="qk_prep",
    )(h, h, h, gains, cos_t, sin_t)


ONES_ROWS = 16


def _flash_pipeline(n_chunks, score_fn, vt_fn, bufs, m_ref, acc_ref):
    assert n_chunks % 2 == 0
    (s0, x0), (s1, x1) = bufs

    def scores(c, s_ref, x_ref):
        s = score_fn(c)
        s_ref[...] = s
        x_ref[...] = jnp.max(s, axis=0, keepdims=True)

    def accumulate(c, s_ref, x_ref):
        m_old = m_ref[...]
        m_new = jnp.maximum(m_old, x_ref[...])
        a = jnp.exp2(m_old - m_new)
        p = jnp.exp2(s_ref[...] - m_new).astype(BF16)
        vt = vt_fn(c)
        vt1 = jnp.concatenate([vt, jnp.ones((ONES_ROWS, vt.shape[1]), vt.dtype)], axis=0)
        acc_ref[...] = a * acc_ref[...] + _dot(vt1, p)
        m_ref[...] = m_new

    m_ref[...] = jnp.full(m_ref.shape, NEG_BIG, F32)
    acc_ref[...] = jnp.zeros(acc_ref.shape, F32)
    scores(0, s0, x0)

    def body(i, carry):
        c = 2 * i
        scores(c + 1, s1, x1)
        accumulate(c, s0, x0)
        scores(jnp.minimum(c + 2, n_chunks - 1), s0, x0)
        accumulate(c + 1, s1, x1)
        return carry

    lax.fori_loop(0, n_chunks // 2, body, 0)


def _attn_b_kernel(qt_ref, k_ref, vt_ref, o_ref, m_ref, acc_ref, s0_ref, x0_ref, s1_ref, x1_ref, *, tk):
    rep = qt_ref.shape[0] // HEAD_DIM
    tq = qt_ref.shape[1]
    seq = k_ref.shape[0]
    qt = jnp.concatenate([qt_ref[r * HEAD_DIM:(r + 1) * HEAD_DIM, :] for r in range(rep)], axis=1)

    def score_fn(c):
        return _dot(k_ref[pl.ds(pl.multiple_of(c * tk, tk), tk), :], qt)

    def vt_fn(c):
        return vt_ref[:, pl.ds(pl.multiple_of(c * tk, tk), tk)]

    _flash_pipeline(seq // tk, score_fn, vt_fn, ((s0_ref, x0_ref), (s1_ref, x1_ref)), m_ref, acc_ref)
    o = acc_ref[:HEAD_DIM, :] / acc_ref[HEAD_DIM:HEAD_DIM + 1, :]
    for r in range(rep):
        o_ref[:, r * HEAD_DIM:(r + 1) * HEAD_DIM] = o[:, r * tq:(r + 1) * tq].T.astype(o_ref.dtype)


def _flash_scratch(tk, width):
    return [pltpu.VMEM((1, width), F32), pltpu.VMEM((HEAD_DIM + ONES_ROWS, width), F32),
            pltpu.VMEM((tk, width), F32), pltpu.VMEM((1, width), F32),
            pltpu.VMEM((tk, width), F32), pltpu.VMEM((1, width), F32)]


def attn_b(qt, kb, vt, batch, seq, tq, tk):
    n = kb.shape[0]
    rep = N_HEADS // KV_HEADS
    nq = seq // tq
    width = rep * tq
    return pl.pallas_call(
        functools.partial(_attn_b_kernel, tk=tk),
        grid=(batch, KV_HEADS, nq),
        in_specs=[pl.BlockSpec((rep * HEAD_DIM, tq), lambda b, g, i: (g, b * nq + i)),
                  pl.BlockSpec((seq, HEAD_DIM), lambda b, g, i: (b, g)),
                  pl.BlockSpec((HEAD_DIM, seq), lambda b, g, i: (g, b))],
        out_specs=pl.BlockSpec((tq, rep * HEAD_DIM), lambda b, g, i: (b * nq + i, g)),
        out_shape=jax.ShapeDtypeStruct((n, BRANCH_W), BF16),
        scratch_shapes=_flash_scratch(tk, width),
        compiler_params=_params(("parallel", "parallel", "parallel")),
        name="attn_b",
    )(qt, kb, vt)


def _retention_kernel(*refs, final):
    if final:
        (q_ref, k_ref, v_ref, inner_ref, qdec_ref, kdec_ref, cdec_ref, prev_ref, gate_ref, ng_ref,
         o_ref, state_ref) = refs
    else:
        q_ref, k_ref, v_ref, inner_ref, qdec_ref, kdec_ref, cdec_ref, o_ref, state_ref = refs

    @pl.when(pl.program_id(1) == 0)
    def _():
        state_ref[...] = jnp.zeros(state_ref.shape, F32)

    for hh in range(N_HEADS):
        sl = slice(hh * HEAD_DIM, (hh + 1) * HEAD_DIM)
        q = q_ref[:, sl]
        k = k_ref[:, sl]
        v = v_ref[:, sl]
        state = state_ref[hh]
        inner = _nt_dot(q, k) * inner_ref[hh]
        o = _dot(inner.astype(BF16), v) + _dot(q, state.astype(BF16)) * qdec_ref[hh]
        kd = (k.astype(F32) * kdec_ref[hh]).astype(BF16)
        state_ref[hh] = state * cdec_ref[hh] + _tn_dot(kd, v)
        if final:
            r = o + prev_ref[:, sl]
            mu = jnp.mean(r, axis=-1, keepdims=True)
            d = r - mu
            var = jnp.mean(d * d, axis=-1, keepdims=True)
            rn = d * lax.rsqrt(var + LN_EPS) * ng_ref[:, sl]
            gate = gate_ref[:, sl].astype(F32)
            silu = gate / (1.0 + jnp.exp(-gate))
            o_ref[:, sl] = (silu * rn).astype(o_ref.dtype)
        else:
            o_ref[:, sl] = o


def retention(h, tabs_f, tabs_b, norm_g, batch, seq):
    n = h.shape[0]
    nb = seq // BLOCK
    w = BRANCH_W
    tab_specs = [pl.BlockSpec((N_HEADS, BLOCK, BLOCK), lambda b, c: (0, 0, 0)),
                 pl.BlockSpec((N_HEADS, BLOCK, HEAD_DIM), lambda b, c: (0, 0, 0)),
                 pl.BlockSpec((N_HEADS, BLOCK, HEAD_DIM), lambda b, c: (0, 0, 0)),
                 pl.BlockSpec((N_HEADS, 1, HEAD_DIM), lambda b, c: (0, 0, 0))]

    def specs(rowmap):
        return [pl.BlockSpec((BLOCK, w), lambda b, c: (rowmap(b, c), COL_CQ // w)),
                pl.BlockSpec((BLOCK, w), lambda b, c: (rowmap(b, c), COL_CK // w)),
                pl.BlockSpec((BLOCK, w), lambda b, c: (rowmap(b, c), COL_CV // w))]

    fmap = lambda b, c: b * nb + c
    bmap = lambda b, c: b * nb + (nb - 1 - c)
    scratch = [pltpu.VMEM((N_HEADS, HEAD_DIM, HEAD_DIM), F32)]
    o_f = pl.pallas_call(
        functools.partial(_retention_kernel, final=False),
        grid=(batch, nb),
        in_specs=specs(fmap) + tab_specs,
        out_specs=pl.BlockSpec((BLOCK, w), lambda b, c: (fmap(b, c), 0)),
        out_shape=jax.ShapeDtypeStruct((n, w), F32),
        scratch_shapes=scratch,
        compiler_params=_params(("parallel", "arbitrary")),
        name="retention_fwd",
    )(h, h, h, *tabs_f)
    return pl.pallas_call(
        functools.partial(_retention_kernel, final=True),
        grid=(batch, nb),
        in_specs=specs(bmap) + tab_specs + [
            pl.BlockSpec((BLOCK, w), lambda b, c: (bmap(b, c), 0)),
            pl.BlockSpec((BLOCK, w), lambda b, c: (bmap(b, c), COL_CG // w)),
            pl.BlockSpec((1, w), lambda b, c: (0, 0))],
        out_specs=pl.BlockSpec((BLOCK, w), lambda b, c: (bmap(b, c), 0)),
        out_shape=jax.ShapeDtypeStruct((n, w), BF16),
        scratch_shapes=scratch,
        compiler_params=_params(("parallel", "arbitrary")),
        name="retention_bwd",
    )(h, h, h, *tabs_b, o_f, h, norm_g)


def retention_tables(dec, scale, backward):
    lg = jnp.log1p(-jnp.exp(dec.astype(F32)))
    pos = jnp.arange(BLOCK, dtype=F32)
    diff = pos[:, None] - pos[None, :]
    if backward:
        diff = -diff
        mask = diff > 0
        qexp = BLOCK - pos
        kexp = pos
    else:
        mask = diff >= 0
        qexp = pos + 1.0
        kexp = BLOCK - 1.0 - pos
    inner = jnp.where(mask[None], jnp.exp(lg[:, None, None] * jnp.where(mask, diff, 0.0)[None]), 0.0) * scale
    qdec = jnp.broadcast_to(jnp.exp(lg[:, None] * qexp)[:, :, None], (N_HEADS, BLOCK, HEAD_DIM))
    kdec = jnp.broadcast_to((jnp.exp(lg[:, None] * kexp) * scale)[:, :, None], (N_HEADS, BLOCK, HEAD_DIM))
    cdec = jnp.broadcast_to(jnp.exp(lg * BLOCK)[:, None, None], (N_HEADS, 1, HEAD_DIM))
    return inner, qdec, kdec, cdec


def _attn_d_kernel(slope_ref, lam_ref, g_ref, q_ref, k_ref, vt_ref, o_ref, m_ref, acc_ref,
                   s0_ref, x0_ref, s1_ref, x1_ref, *, tk, lam_init):
    tq = q_ref.shape[1]
    seq = k_ref.shape[0]
    slope = slope_ref[pl.program_id(1)]
    q0 = pl.program_id(2) * tq
    feat = lax.broadcasted_iota(jnp.int32, (HEAD_DIM, tq), 0)
    q = q_ref[...]
    zero = jnp.zeros_like(q)
    qt = jnp.concatenate([jnp.where(feat < D_HALF, q, zero), jnp.where(feat >= D_HALF, q, zero)], axis=1)
    rel = (lax.broadcasted_iota(jnp.int32, (tk, tq), 0) - lax.broadcasted_iota(jnp.int32, (tk, tq), 1)).astype(F32)

    def score_fn(c):
        off = pl.multiple_of(c * tk, tk)
        bias = slope * jnp.abs(rel + (off - q0).astype(F32))
        return _dot(k_ref[pl.ds(off, tk), :], qt) - jnp.concatenate([bias, bias], axis=1)

    def vt_fn(c):
        return vt_ref[:, pl.ds(pl.multiple_of(c * tk, tk), tk)]

    _flash_pipeline(seq // tk, score_fn, vt_fn, ((s0_ref, x0_ref), (s1_ref, x1_ref)), m_ref, acc_ref)
    lp = lam_ref[...]
    lam = (jnp.exp(jnp.sum(lp[0:1] * lp[1:2], axis=-1, keepdims=True))
           - jnp.exp(jnp.sum(lp[2:3] * lp[3:4], axis=-1, keepdims=True)) + lam_init)
    on = acc_ref[:HEAD_DIM, :] / acc_ref[HEAD_DIM:HEAD_DIM + 1, :]
    o = on[:, :tq] - lam * on[:, tq:]
    y = o * lax.rsqrt(jnp.mean(o * o, axis=0, keepdims=True) + RMS_EPS) * g_ref[...]
    o_ref[...] = (y * (1.0 - lam_init)).T.astype(o_ref.dtype)


def attn_d(qt, h, vt, slopes_log2, lam_params, subln_g, lam_init, batch, seq, tq, tk):
    n = h.shape[0]
    nq = seq // tq
    return pl.pallas_call(
        functools.partial(_attn_d_kernel, tk=tk, lam_init=lam_init),
        grid=(batch, N_HEADS, nq),
        in_specs=[pl.BlockSpec(memory_space=pltpu.SMEM),
                  pl.BlockSpec((4, D_HALF), lambda b, hh, i: (0, 0)),
                  pl.BlockSpec((HEAD_DIM, 1), lambda b, hh, i: (0, 0)),
                  pl.BlockSpec((HEAD_DIM, tq), lambda b, hh, i: (hh, b * nq + i)),
                  pl.BlockSpec((seq, HEAD_DIM), lambda b, hh, i: (b, COL_DK // HEAD_DIM + hh)),
                  pl.BlockSpec((HEAD_DIM, seq), lambda b, hh, i: (hh, b))],
        out_specs=pl.BlockSpec((tq, HEAD_DIM), lambda b, hh, i: (b * nq + i, hh)),
        out_shape=jax.ShapeDtypeStruct((n, BRANCH_W), BF16),
        scratch_shapes=_flash_scratch(tk, 2 * tq),
        compiler_params=_params(("parallel", "parallel", "parallel")),
        name="attn_d",
    )(slopes_log2, lam_params, subln_g, qt, h, vt)


def _gate_merge_kernel(x_ref, wg_ref, o_ref, wb_ref, out_ref, acc_ref):
    i = pl.program_id(2)
    gate = _dot(x_ref[...], wg_ref[...])
    contrib = _dot(o_ref[...], wb_ref[...]) / (1.0 + jnp.exp(-gate))

    @pl.when(i == 0)
    def _():
        acc_ref[...] = contrib

    @pl.when(i > 0)
    def _():
        acc_ref[...] += contrib

    @pl.when(i == N_BRANCH - 1)
    def _():
        out_ref[...] = acc_ref[...].astype(out_ref.dtype)


def gate_merge(x_bf, w_in_bf, o_stack, w_branch_bf, layer, tm, tn):
    n, d = x_bf.shape
    gcol0 = MIX_COLS // tn
    per = d // tn
    return pl.pallas_call(
        _gate_merge_kernel,
        grid=(n // tm, d // tn, N_BRANCH),
        in_specs=[pl.BlockSpec((tm, d), lambda m, j, i: (m, 0)),
                  pl.BlockSpec((None, d, tn), lambda m, j, i: (layer, 0, gcol0 + i * per + j)),
                  pl.BlockSpec((None, tm, BRANCH_W), lambda m, j, i: (i, m, 0)),
                  pl.BlockSpec((None, None, BRANCH_W, tn), lambda m, j, i: (layer, i, 0, j))],
        out_specs=pl.BlockSpec((tm, tn), lambda m, j, i: (m, j)),
        out_shape=jax.ShapeDtypeStruct((n, d), BF16),
        scratch_shapes=[pltpu.VMEM((tm, tn), F32)],
        compiler_params=_params(("parallel", "parallel", "arbitrary")),
        name="gate_merge",
    )(x_bf, w_in_bf, o_stack, w_branch_bf)


def _layer_norm(y, g, b):
    mu = jnp.mean(y, axis=-1, keepdims=True)
    d = y - mu
    var = jnp.mean(d * d, axis=-1, keepdims=True)
    return d * lax.rsqrt(var + LN_EPS) * g + b


def _out_ln_kernel(m_ref, w_ref, x_ref, g_ref, b_ref, of_ref, ob_ref, *, alpha):
    k = pl.program_id(1)
    part = _dot(m_ref[...], w_ref[...])

    @pl.when(k == 0)
    def _():
        of_ref[...] = part

    @pl.when(k > 0)
    def _():
        of_ref[...] += part

    @pl.when(k == pl.num_programs(1) - 1)
    def _():
        out = _layer_norm(alpha * x_ref[...] + of_ref[...], g_ref[...], b_ref[...])
        of_ref[...] = out
        ob_ref[...] = out.astype(ob_ref.dtype)


def out_proj_ln(merged, w_out_bf, x, g, b, layer, alpha, tm, tk):
    n, d = x.shape
    return pl.pallas_call(
        functools.partial(_out_ln_kernel, alpha=alpha),
        grid=(n // tm, d // tk),
        in_specs=[pl.BlockSpec((tm, tk), lambda m, k: (m, k)),
                  pl.BlockSpec((None, tk, d), lambda m, k: (layer, k, 0)),
                  pl.BlockSpec((tm, d), lambda m, k: (m, 0)),
                  pl.BlockSpec((None, 1, d), lambda m, k: (layer, 0, 0)),
                  pl.BlockSpec((None, 1, d), lambda m, k: (layer, 0, 0))],
        out_specs=[pl.BlockSpec((tm, d), lambda m, k: (m, 0)), pl.BlockSpec((tm, d), lambda m, k: (m, 0))],
        out_shape=[jax.ShapeDtypeStruct((n, d), F32), jax.ShapeDtypeStruct((n, d), BF16)],
        compiler_params=_params(("parallel", "arbitrary")),
        name="out_proj_ln",
    )(merged, w_out_bf, x, g, b)


def _router_kernel(x_ref, whi_ref, wlo_ref, b_ref, o_ref):
    x = x_ref[...]
    x_hi = x.astype(BF16)
    x_lo = (x - x_hi.astype(F32)).astype(BF16)
    whi = whi_ref[...]
    logits = _dot(x_hi, whi) + _dot(x_lo, whi) + _dot(x_hi, wlo_ref[...]) + b_ref[...]
    lane = lax.broadcasted_iota(jnp.int32, logits.shape, 1)
    big = jnp.int32(LANES)
    gmask = lane < N_GROUPS
    gl = jnp.where(gmask, logits, NEG_BIG)
    gmax = jnp.max(gl, axis=-1, keepdims=True)
    g_top = jnp.min(jnp.where(gmask & (gl == gmax), lane, big), axis=-1, keepdims=True)
    g_w = 1.0 / jnp.sum(jnp.where(gmask, jnp.exp(gl - gmax), 0.0), axis=-1, keepdims=True)
    lo = N_GROUPS + g_top * EXPERTS_PER_GROUP
    emask = (lane >= lo) & (lane < lo + EXPERTS_PER_GROUP)
    e1 = jnp.where(emask, logits, NEG_BIG)
    v1 = jnp.max(e1, axis=-1, keepdims=True)
    i1 = jnp.min(jnp.where(emask & (e1 == v1), lane, big), axis=-1, keepdims=True)
    emask2 = emask & (lane != i1)
    e2 = jnp.where(emask2, logits, NEG_BIG)
    v2 = jnp.max(e2, axis=-1, keepdims=True)
    i2 = jnp.min(jnp.where(emask2 & (e2 == v2), lane, big), axis=-1, keepdims=True)
    t = jnp.exp(v2 - v1)
    w1 = g_w / (1.0 + t)
    w2 = g_w * t / (1.0 + t)
    comb = jnp.where(lane == i1, w1, 0.0) + jnp.where(lane == i2, w2, 0.0)
    ids = jnp.where(lane == 0, (i1 - N_GROUPS).astype(F32), 0.0) + jnp.where(lane == 1, (i2 - N_GROUPS).astype(F32), 0.0)
    o_ref[...] = comb + ids


def router(x, w_hi, w_lo, bias, tm):
    n, d = x.shape
    return pl.pallas_call(
        _router_kernel,
        grid=(n // tm,),
        in_specs=[pl.BlockSpec((tm, d), lambda m: (m, 0)),
                  pl.BlockSpec((d, LANES), lambda m: (0, 0)),
                  pl.BlockSpec((d, LANES), lambda m: (0, 0)),
                  pl.BlockSpec((1, LANES), lambda m: (0, 0))],
        out_specs=pl.BlockSpec((tm, LANES), lambda m: (m, 0)),
        out_shape=jax.ShapeDtypeStruct((n, LANES), F32),
        compiler_params=_params(("parallel",)),
        name="router",
    )(x, w_hi, w_lo, bias)


def _moe_dense_kernel(xb_ref, comb_ref, wg_ref, wu_ref, wd_ref, x_ref, g_ref, b_ref, of_ref, ob_ref, *, alpha):
    e = pl.program_id(1)
    xb = xb_ref[...]
    gate = _dot(xb, wg_ref[...])
    up = _dot(xb, wu_ref[...])
    lane = lax.broadcasted_iota(jnp.int32, comb_ref.shape, 1)
    c = jnp.sum(jnp.where(lane == e + N_GROUPS, comb_ref[...], 0.0), axis=-1, keepdims=True)
    hmid = (gate / (1.0 + jnp.exp(-gate))) * up * c
    y = _dot(hmid.astype(BF16), wd_ref[...])

    @pl.when(e == 0)
    def _():
        of_ref[...] = y

    @pl.when(e > 0)
    def _():
        of_ref[...] += y

    @pl.when(e == N_EXPERTS - 1)
    def _():
        out = _layer_norm(alpha * x_ref[...] + of_ref[...], g_ref[...], b_ref[...])
        of_ref[...] = out
        ob_ref[...] = out.astype(ob_ref.dtype)


def moe_dense_ln(x_bf, comb, wg_bf, wu_bf, wd_bf, x, g, b, layer, alpha, tm):
    n, d = x.shape
    f = wg_bf.shape[-1]
    return pl.pallas_call(
        functools.partial(_moe_dense_kernel, alpha=alpha),
        grid=(n // tm, N_EXPERTS),
        in_specs=[pl.BlockSpec((tm, d), lambda m, e: (m, 0)),
                  pl.BlockSpec((tm, LANES), lambda m, e: (m, 0)),
                  pl.BlockSpec((None, None, d, f), lambda m, e: (layer, e, 0, 0)),
                  pl.BlockSpec((None, None, d, f), lambda m, e: (layer, e, 0, 0)),
                  pl.BlockSpec((None, None, f, d), lambda m, e: (layer, e, 0, 0)),
                  pl.BlockSpec((tm, d), lambda m, e: (m, 0)),
                  pl.BlockSpec((None, 1, d), lambda m, e: (layer, 0, 0)),
                  pl.BlockSpec((None, 1, d), lambda m, e: (layer, 0, 0))],
        out_specs=[pl.BlockSpec((tm, d), lambda m, e: (m, 0)), pl.BlockSpec((tm, d), lambda m, e: (m, 0))],
        out_shape=[jax.ShapeDtypeStruct((n, d), F32), jax.ShapeDtypeStruct((n, d), BF16)],
        compiler_params=_params(("parallel", "arbitrary")),
        name="moe_dense_ln",
    )(x_bf, comb, wg_bf, wu_bf, wd_bf, x, g, b)


def _rope_tables(seq):
    t = jnp.arange(seq)
    row_id = (t // GRID_W).astype(F32)
    col_id = (t % GRID_W).astype(F32)
    nfreq = HEAD_DIM // 4
    inv = ROPE_THETA ** (-jnp.arange(nfreq, dtype=F32) / nfreq)
    ar = row_id[:, None] * inv
    ac = col_id[:, None] * inv
    cos = jnp.concatenate([jnp.cos(ar), jnp.cos(ar), jnp.cos(ac), jnp.cos(ac)], axis=-1)
    sin = jnp.concatenate([-jnp.sin(ar), jnp.sin(ar), -jnp.sin(ac), jnp.sin(ac)], axis=-1)
    return cos, sin


def _tile(n, pref):
    t = min(n, pref)
    while n % t:
        t //= 2
    return t


def kernel(x, w_in, attn_sink, qk_norm_q, qk_norm_k, ret_decay_fwd, ret_decay_bwd, ret_norm_g,
           diff_lambda_q1, diff_lambda_k1, diff_lambda_q2, diff_lambda_k2, diff_subln_g,
           w_branch, w_out, ln_mix_g, ln_mix_b, router_group, router_group_b, router_expert,
           router_expert_b, expert_w_gate, expert_w_up, expert_w_down, ln_ffn_g, ln_ffn_b):
    batch, seq, d = x.shape
    depth = w_in.shape[0]
    n = batch * seq
    alpha = (2.0 * depth) ** 0.25
    assert w_in.shape[2] == MIX_COLS + N_BRANCH * d and seq % BLOCK == 0 and seq % GRID_W == 0

    colscale = jnp.ones((w_in.shape[2],), F32)
    colscale = colscale.at[COL_AQ:COL_AQ + BRANCH_W].set(HEAD_DIM ** -0.5 * LOG2E)
    colscale = colscale.at[COL_DQ:COL_DQ + BRANCH_W].set(D_HALF ** -0.5 * LOG2E)
    w_in_bf = (w_in * colscale).astype(BF16)
    w_branch_bf = w_branch.astype(BF16)
    w_out_bf = w_out.astype(BF16)
    wg_bf = expert_w_gate.astype(BF16)
    wu_bf = expert_w_up.astype(BF16)
    wd_bf = expert_w_down.astype(BF16)
    ln_mix_g3, ln_mix_b3 = ln_mix_g[:, None, :], ln_mix_b[:, None, :]
    ln_ffn_g3, ln_ffn_b3 = ln_ffn_g[:, None, :], ln_ffn_b[:, None, :]

    cos_t, sin_t = _rope_tables(seq)
    slopes_log2 = jnp.asarray([LOG2E * 2.0 ** (-(i + 1.0)) for i in range(N_HEADS)], F32)

    tm_big = _tile(n, 1024)
    xf = x.reshape(n, d)
    xb = xf.astype(BF16)
    for l in range(depth):
        h = in_proj(xb, w_in_bf, l, MIX_COLS, tm_big, 1024)
        o_a = attn_a(h, attn_sink[l].astype(F32) * LOG2E, batch, seq)
        gains = jnp.concatenate([jnp.tile(qk_norm_q[l], N_HEADS) * (HEAD_DIM ** -0.5 * LOG2E),
                                 jnp.tile(qk_norm_k[l], KV_HEADS)])[None, :]
        qt_b, k_b = qk_prep(h, gains, cos_t, sin_t, batch, seq, _tile(seq, 512))
        vt_b = h[:, COL_BV:COL_BV + KV_HEADS * HEAD_DIM].T
        o_b = attn_b(qt_b, k_b, vt_b, batch, seq, _tile(seq, 256), _tile(seq // 2, 512))
        tabs_f = retention_tables(ret_decay_fwd[l], HEAD_DIM ** -0.5, False)
        tabs_b = retention_tables(ret_decay_bwd[l], HEAD_DIM ** -0.5, True)
        o_c = retention(h, tabs_f, tabs_b, ret_norm_g[l][None, :], batch, seq)
        lam_init = 0.8 - 0.6 * math.exp(-0.3 * l)
        lam_params = jnp.stack([diff_lambda_q1[l], diff_lambda_k1[l], diff_lambda_q2[l], diff_lambda_k2[l]]).astype(F32)
        qt_d = h[:, COL_DQ:COL_DQ + BRANCH_W].T
        vt_d = h[:, COL_DV:COL_DV + BRANCH_W].T
        o_d = attn_d(qt_d, h, vt_d, slopes_log2, lam_params, diff_subln_g[l][:, None], lam_init, batch, seq,
                     _tile(seq, 512), _tile(seq // 2, 512))
        merged = gate_merge(xb, w_in_bf, jnp.stack([o_a, o_b, o_c, o_d]), w_branch_bf, l, tm_big, 512)
        xf, xb = out_proj_ln(merged, w_out_bf, xf, ln_mix_g3, ln_mix_b3, l, alpha, _tile(n, 256), _tile(d, 1024))
        w_r = jnp.concatenate([router_group[l], router_expert[l]], axis=1)
        w_r = jnp.pad(w_r, ((0, 0), (0, LANES - w_r.shape[1])))
        w_r_hi = w_r.astype(BF16)
        w_r_lo = (w_r - w_r_hi.astype(F32)).astype(BF16)
        b_r = jnp.pad(jnp.concatenate([router_group_b[l], router_expert_b[l]]), (0, LANES - N_GROUPS - N_EXPERTS))[None, :]
        comb = router(xf, w_r_hi, w_r_lo, b_r, _tile(n, 512))
        xf, xb = moe_dense_ln(xb, comb, wg_bf, wu_bf, wd_bf, xf, ln_ffn_g3, ln_ffn_b3, l, alpha, _tile(n, 256))
    return xf.reshape(batch, seq, d)
```

```python
import functools
import math

import jax
import jax.numpy as jnp
from jax import lax
from jax.experimental import pallas as pl
from jax.experimental.pallas import tpu as pltpu

F32 = jnp.float32
BF16 = jnp.bfloat16

HEAD_DIM = 128
BLOCK = 128
GRID_W = 64
N_HEADS = 8
KV_HEADS = 2
D_HALF = HEAD_DIM // 2
N_BRANCH = 4
BRANCH_W = N_HEADS * HEAD_DIM
ROPE_THETA = 10000.0
N_GROUPS = 4
EXPERTS_PER_GROUP = 4
N_EXPERTS = N_GROUPS * EXPERTS_PER_GROUP
LN_EPS = 1e-5
RMS_EPS = 1e-6
NEG_BIG = -1e30
LOG2E = 1.4426950408889634
LANES = 128
VMEM_LIMIT = 56 * 1024 * 1024

COL_AQ, COL_AK, COL_AV = 0, 1024, 1280
COL_BQ, COL_BK, COL_BV = 1536, 2560, 2816
COL_CQ, COL_CK, COL_CV, COL_CG = 3072, 4096, 5120, 6144
COL_DQ, COL_DK, COL_DV = 7168, 8192, 9216
MIX_COLS = 10240


def _params(sem):
    return pltpu.CompilerParams(dimension_semantics=sem, vmem_limit_bytes=VMEM_LIMIT)


def _nt_dot(a, b):
    return lax.dot_general(a, b, (((1,), (1,)), ((), ())), preferred_element_type=F32)


def _tn_dot(a, b):
    return lax.dot_general(a, b, (((0,), (0,)), ((), ())), preferred_element_type=F32)


def _dot(a, b):
    return jnp.dot(a, b, preferred_element_type=F32)


def _mm_kernel(x_ref, w_ref, o_ref):
    o_ref[...] = _dot(x_ref[...], w_ref[...]).astype(o_ref.dtype)


def in_proj(x_bf, w_in_bf, layer, n_cols, tm, tn):
    n, k = x_bf.shape
    return pl.pallas_call(
        _mm_kernel,
        grid=(n // tm, n_cols // tn),
        in_specs=[pl.BlockSpec((tm, k), lambda i, j: (i, 0)),
                  pl.BlockSpec((None, k, tn), lambda i, j: (layer, 0, j))],
        out_specs=pl.BlockSpec((tm, tn), lambda i, j: (i, j)),
        out_shape=jax.ShapeDtypeStruct((n, n_cols), BF16),
        compiler_params=_params(("parallel", "parallel")),
        name="in_proj",
    )(x_bf, w_in_bf)


def _attn_a_kernel(sink_ref, q_ref, kp_ref, kc_ref, kn_ref, vp_ref, vc_ref, vn_ref, o_ref, *, slopes, nb):
    i = pl.program_id(1)
    rep = N_HEADS // KV_HEADS
    rows = rep * BLOCK
    row = lax.broadcasted_iota(jnp.int32, (rows, 3 * BLOCK), 0)
    col = lax.broadcasted_iota(jnp.int32, (rows, 3 * BLOCK), 1)
    rel = (row % BLOCK) - (col - BLOCK)
    dist = jnp.abs(rel)
    valid = (dist <= BLOCK) & ((col >= BLOCK) | (i > 0)) & ((col < 2 * BLOCK) | (i < nb - 1))
    distf = dist.astype(F32)
    rowc = lax.broadcasted_iota(jnp.int32, (rows, 1), 0)
    for g in range(KV_HEADS):
        q = jnp.concatenate([q_ref[:, (g * rep + r) * HEAD_DIM:(g * rep + r + 1) * HEAD_DIM]
                             for r in range(rep)], axis=0)
        ks = slice(g * HEAD_DIM, (g + 1) * HEAD_DIM)
        k = jnp.concatenate([kp_ref[:, ks], kc_ref[:, ks], kn_ref[:, ks]], axis=0)
        v = jnp.concatenate([vp_ref[:, ks], vc_ref[:, ks], vn_ref[:, ks]], axis=0)
        slope = jnp.full((rows, 1), slopes[g * rep], F32)
        sink = jnp.full((rows, 1), sink_ref[g * rep], F32)
        for r in range(1, rep):
            slope = jnp.where(rowc >= r * BLOCK, slopes[g * rep + r], slope)
            sink = jnp.where(rowc >= r * BLOCK, sink_ref[g * rep + r], sink)
        s = _nt_dot(q, k) - slope * distf
        s = jnp.where(valid, s, NEG_BIG)
        m = jnp.maximum(jnp.max(s, axis=-1, keepdims=True), sink)
        p = jnp.exp2(s - m)
        denom = jnp.sum(p, axis=-1, keepdims=True) + jnp.exp2(sink - m)
        o = _dot(p.astype(BF16), v) / denom
        for r in range(rep):
            hh = g * rep + r
            o_ref[:, hh * HEAD_DIM:(hh + 1) * HEAD_DIM] = o[r * BLOCK:(r + 1) * BLOCK].astype(o_ref.dtype)


def attn_a(h, sink_log2, batch, seq):
    n = h.shape[0]
    nb = seq // BLOCK
    slopes = tuple(LOG2E * 2.0 ** (-(i + 1.0)) for i in range(N_HEADS))
    kvw = KV_HEADS * HEAD_DIM
    kcol, vcol = COL_AK // kvw, COL_AV // kvw

    def prev(b, i):
        return b * nb + jnp.maximum(i - 1, 0)

    def nxt(b, i):
        return b * nb + jnp.minimum(i + 1, nb - 1)

    return pl.pallas_call(
        functools.partial(_attn_a_kernel, slopes=slopes, nb=nb),
        grid=(batch, nb),
        in_specs=[pl.BlockSpec(memory_space=pltpu.SMEM),
                  pl.BlockSpec((BLOCK, BRANCH_W), lambda b, i: (b * nb + i, COL_AQ // BRANCH_W)),
                  pl.BlockSpec((BLOCK, kvw), lambda b, i: (prev(b, i), kcol)),
                  pl.BlockSpec((BLOCK, kvw), lambda b, i: (b * nb + i, kcol)),
                  pl.BlockSpec((BLOCK, kvw), lambda b, i: (nxt(b, i), kcol)),
                  pl.BlockSpec((BLOCK, kvw), lambda b, i: (prev(b, i), vcol)),
                  pl.BlockSpec((BLOCK, kvw), lambda b, i: (b * nb + i, vcol)),
                  pl.BlockSpec((BLOCK, kvw), lambda b, i: (nxt(b, i), vcol))],
        out_specs=pl.BlockSpec((BLOCK, BRANCH_W), lambda b, i: (b * nb + i, 0)),
        out_shape=jax.ShapeDtypeStruct((n, BRANCH_W), BF16),
        compiler_params=_params(("parallel", "parallel")),
        name="attn_a",
    )(sink_log2, h, h, h, h, h, h, h)


def _norm_rope(x, g, cos, sin, first_quarter):
    y = x * lax.rsqrt(jnp.mean(x * x, axis=-1, keepdims=True) + RMS_EPS) * g
    partner = jnp.where(first_quarter, pltpu.roll(y, HEAD_DIM - HEAD_DIM // 4, 1), pltpu.roll(y, HEAD_DIM // 4, 1))
    return y * cos + partner * sin


def _qk_prep_kernel(q0_ref, q1_ref, k_ref, g_ref, cos_ref, sin_ref, qt_ref, ko_ref):
    lane = lax.broadcasted_iota(jnp.int32, (k_ref.shape[0], HEAD_DIM), 1)
    first_quarter = (lane % (HEAD_DIM // 2)) < (HEAD_DIM // 4)
    cos = cos_ref[...]
    sin = sin_ref[...]
    half = N_HEADS // 2
    for hh in range(N_HEADS):
        src = q0_ref if hh < half else q1_ref
        sl = slice((hh % half) * HEAD_DIM, (hh % half + 1) * HEAD_DIM)
        gs = slice(hh * HEAD_DIM, (hh + 1) * HEAD_DIM)
        y = _norm_rope(src[:, sl].astype(F32), g_ref[:, gs], cos, sin, first_quarter)
        qt_ref[gs, :] = y.T.astype(qt_ref.dtype)
    for hh in range(KV_HEADS):
        sl = slice(hh * HEAD_DIM, (hh + 1) * HEAD_DIM)
        gs = slice((N_HEADS + hh) * HEAD_DIM, (N_HEADS + hh + 1) * HEAD_DIM)
        ko_ref[:, sl] = _norm_rope(k_ref[:, sl].astype(F32), g_ref[:, gs], cos, sin, first_quarter).astype(ko_ref.dtype)


def qk_prep(h, gains, cos_t, sin_t, batch, seq, tm):
    n = h.shape[0]
    qw = BRANCH_W // 2
    kw = KV_HEADS * HEAD_DIM
    nblk = seq // tm
    return pl.pallas_call(
        _qk_prep_kernel,
        grid=(batch, nblk),
        in_specs=[pl.BlockSpec((tm, qw), lambda b, i: (b * nblk + i, COL_BQ // qw)),
                  pl.BlockSpec((tm, qw), lambda b, i: (b * nblk + i, COL_BQ // qw + 1)),
                  pl.BlockSpec((tm, kw), lambda b, i: (b * nblk + i, COL_BK // kw)),
                  pl.BlockSpec((1, BRANCH_W + kw), lambda b, i: (0, 0)),
                  pl.BlockSpec((tm, HEAD_DIM), lambda b, i: (i, 0)),
                  pl.BlockSpec((tm, HEAD_DIM), lambda b, i: (i, 0))],
        out_specs=[pl.BlockSpec((BRANCH_W, tm), lambda b, i: (0, b * nblk + i)),
                   pl.BlockSpec((tm, kw), lambda b, i: (b * nblk + i, 0))],
        out_shape=[jax.ShapeDtypeStruct((BRANCH_W, n), BF16), jax.ShapeDtypeStruct((n, kw), BF16)],
        compiler_params=_params(("parallel", "parallel")),
        name="qk_prep",
    )(h, h, h, gains, cos_t, sin_t)


ONES_ROWS = 16


def _flash_pipeline(n_chunks, score_fn, vt_fn, bufs, m_ref, acc_ref):
    assert n_chunks % 2 == 0
    (s0, x0), (s1, x1) = bufs

    def scores(c, s_ref, x_ref):
        s = score_fn(c)
        s_ref[...] = s
        x_ref[...] = jnp.max(s, axis=0, keepdims=True)

    def accumulate(c, s_ref, x_ref):
        m_old = m_ref[...]
        m_new = jnp.maximum(m_old, x_ref[...])
        a = jnp.exp2(m_old - m_new)
        p = jnp.exp2(s_ref[...] - m_new).astype(BF16)
        vt = vt_fn(c)
        vt1 = jnp.concatenate([vt, jnp.ones((ONES_ROWS, vt.shape[1]), vt.dtype)], axis=0)
        acc_ref[...] = a * acc_ref[...] + _dot(vt1, p)
        m_ref[...] = m_new

    m_ref[...] = jnp.full(m_ref.shape, NEG_BIG, F32)
    acc_ref[...] = jnp.zeros(acc_ref.shape, F32)
    scores(0, s0, x0)

    def body(i, carry):
        c = 2 * i
        scores(c + 1, s1, x1)
        accumulate(c, s0, x0)
        scores(jnp.minimum(c + 2, n_chunks - 1), s0, x0)
        accumulate(c + 1, s1, x1)
        return carry

    lax.fori_loop(0, n_chunks // 2, body, 0)


def _attn_b_kernel(qt_ref, k_ref, vt_ref, o_ref, m_ref, acc_ref, s0_ref, x0_ref, s1_ref, x1_ref, *, tk):
    rep = qt_ref.shape[0] // HEAD_DIM
    tq = qt_ref.shape[1]
    seq = k_ref.shape[0]
    qt = jnp.concatenate([qt_ref[r * HEAD_DIM:(r + 1) * HEAD_DIM, :] for r in range(rep)], axis=1)

    def score_fn(c):
        return _dot(k_ref[pl.ds(pl.multiple_of(c * tk, tk), tk), :], qt)

    def vt_fn(c):
        return vt_ref[:, pl.ds(pl.multiple_of(c * tk, tk), tk)]

    _flash_pipeline(seq // tk, score_fn, vt_fn, ((s0_ref, x0_ref), (s1_ref, x1_ref)), m_ref, acc_ref)
    o = acc_ref[:HEAD_DIM, :] / acc_ref[HEAD_DIM:HEAD_DIM + 1, :]
    for r in range(rep):
        o_ref[:, r * HEAD_DIM:(r + 1) * HEAD_DIM] = o[:, r * tq:(r + 1) * tq].T.astype(o_ref.dtype)


def _flash_scratch(tk, width):
    return [pltpu.VMEM((1, width), F32), pltpu.VMEM((HEAD_DIM + ONES_ROWS, width), F32),
            pltpu.VMEM((tk, width), F32), pltpu.VMEM((1, width), F32),
            pltpu.VMEM((tk, width), F32), pltpu.VMEM((1, width), F32)]


def attn_b(qt, kb, vt, batch, seq, tq, tk):
    n = kb.shape[0]
    rep = N_HEADS // KV_HEADS
    nq = seq // tq
    width = rep * tq
    return pl.pallas_call(
        functools.partial(_attn_b_kernel, tk=tk),
        grid=(batch, KV_HEADS, nq),
        in_specs=[pl.BlockSpec((rep * HEAD_DIM, tq), lambda b, g, i: (g, b * nq + i)),
                  pl.BlockSpec((seq, HEAD_DIM), lambda b, g, i: (b, g)),
                  pl.BlockSpec((HEAD_DIM, seq), lambda b, g, i: (g, b))],
        out_specs=pl.BlockSpec((tq, rep * HEAD_DIM), lambda b, g, i: (b * nq + i, g)),
        out_shape=jax.ShapeDtypeStruct((n, BRANCH_W), BF16),
        scratch_shapes=_flash_scratch(tk, width),
        compiler_params=_params(("parallel", "parallel", "parallel")),
        name="attn_b",
    )(qt, kb, vt)


def _retention_kernel(*refs, final):
    if final:
        (q_ref, k_ref, v_ref, inner_ref, qdec_ref, kdec_ref, cdec_ref, prev_ref, gate_ref, ng_ref,
         o_ref, state_ref) = refs
    else:
        q_ref, k_ref, v_ref, inner_ref, qdec_ref, kdec_ref, cdec_ref, o_ref, state_ref = refs

    @pl.when(pl.program_id(1) == 0)
    def _():
        state_ref[...] = jnp.zeros(state_ref.shape, F32)

    for hh in range(N_HEADS):
        sl = slice(hh * HEAD_DIM, (hh + 1) * HEAD_DIM)
        q = q_ref[:, sl]
        k = k_ref[:, sl]
        v = v_ref[:, sl]
        state = state_ref[hh]
        inner = _nt_dot(q, k) * inner_ref[hh]
        o = _dot(inner.astype(BF16), v) + _dot(q, state.astype(BF16)) * qdec_ref[hh]
        kd = (k.astype(F32) * kdec_ref[hh]).astype(BF16)
        state_ref[hh] = state * cdec_ref[hh] + _tn_dot(kd, v)
        if final:
            r = o + prev_ref[:, sl]
            mu = jnp.mean(r, axis=-1, keepdims=True)
            d = r - mu
            var = jnp.mean(d * d, axis=-1, keepdims=True)
            rn = d * lax.rsqrt(var + LN_EPS) * ng_ref[:, sl]
            gate = gate_ref[:, sl].astype(F32)
            silu = gate / (1.0 + jnp.exp(-gate))
            o_ref[:, sl] = (silu * rn).astype(o_ref.dtype)
        else:
            o_ref[:, sl] = o


def retention(h, tabs_f, tabs_b, norm_g, batch, seq):
    n = h.shape[0]
    nb = seq // BLOCK
    w = BRANCH_W
    tab_specs = [pl.BlockSpec((N_HEADS, BLOCK, BLOCK), lambda b, c: (0, 0, 0)),
                 pl.BlockSpec((N_HEADS, BLOCK, HEAD_DIM), lambda b, c: (0, 0, 0)),
                 pl.BlockSpec((N_HEADS, BLOCK, HEAD_DIM), lambda b, c: (0, 0, 0)),
                 pl.BlockSpec((N_HEADS, 1, HEAD_DIM), lambda b, c: (0, 0, 0))]

    def specs(rowmap):
        return [pl.BlockSpec((BLOCK, w), lambda b, c: (rowmap(b, c), COL_CQ // w)),
                pl.BlockSpec((BLOCK, w), lambda b, c: (rowmap(b, c), COL_CK // w)),
                pl.BlockSpec((BLOCK, w), lambda b, c: (rowmap(b, c), COL_CV // w))]

    fmap = lambda b, c: b * nb + c
    bmap = lambda b, c: b * nb + (nb - 1 - c)
    scratch = [pltpu.VMEM((N_HEADS, HEAD_DIM, HEAD_DIM), F32)]
    o_f = pl.pallas_call(
        functools.partial(_retention_kernel, final=False),
        grid=(batch, nb),
        in_specs=specs(fmap) + tab_specs,
        out_specs=pl.BlockSpec((BLOCK, w), lambda b, c: (fmap(b, c), 0)),
        out_shape=jax.ShapeDtypeStruct((n, w), F32),
        scratch_shapes=scratch,
        compiler_params=_params(("parallel", "arbitrary")),
        name="retention_fwd",
    )(h, h, h, *tabs_f)
    return pl.pallas_call(
        functools.partial(_retention_kernel, final=True),
        grid=(batch, nb),
        in_specs=specs(bmap) + tab_specs + [
            pl.BlockSpec((BLOCK, w), lambda b, c: (bmap(b, c), 0)),
            pl.BlockSpec((BLOCK, w), lambda b, c: (bmap(b, c), COL_CG // w)),
            pl.BlockSpec((1, w), lambda b, c: (0, 0))],
        out_specs=pl.BlockSpec((BLOCK, w), lambda b, c: (bmap(b, c), 0)),
        out_shape=jax.ShapeDtypeStruct((n, w), BF16),
        scratch_shapes=scratch,
        compiler_params=_params(("parallel", "arbitrary")),
        name="retention_bwd",
    )(h, h, h, *tabs_b, o_f, h, norm_g)


def retention_tables(dec, scale, backward):
    lg = jnp.log1p(-jnp.exp(dec.astype(F32)))
    pos = jnp.arange(BLOCK, dtype=F32)
    diff = pos[:, None] - pos[None, :]
    if backward:
        diff = -diff
        mask = diff > 0
        qexp = BLOCK - pos
        kexp = pos
    else:
        mask = diff >= 0
        qexp = pos + 1.0
        kexp = BLOCK - 1.0 - pos
    inner = jnp.where(mask[None], jnp.exp(lg[:, None, None] * jnp.where(mask, diff, 0.0)[None]), 0.0) * scale
    qdec = jnp.broadcast_to(jnp.exp(lg[:, None] * qexp)[:, :, None], (N_HEADS, BLOCK, HEAD_DIM))
    kdec = jnp.broadcast_to((jnp.exp(lg[:, None] * kexp) * scale)[:, :, None], (N_HEADS, BLOCK, HEAD_DIM))
    cdec = jnp.broadcast_to(jnp.exp(lg * BLOCK)[:, None, None], (N_HEADS, 1, HEAD_DIM))
    return inner, qdec, kdec, cdec


def _attn_d_kernel(slope_ref, lam_ref, g_ref, q_ref, k_ref, vt_ref, o_ref, m_ref, acc_ref,
                   s0_ref, x0_ref, s1_ref, x1_ref, *, tk, lam_init):
    tq = q_ref.shape[1]
    seq = k_ref.shape[0]
    slope = slope_ref[pl.program_id(1)]
    q0 = pl.program_id(2) * tq
    feat = lax.broadcasted_iota(jnp.int32, (HEAD_DIM, tq), 0)
    q = q_ref[...]
    zero = jnp.zeros_like(q)
    qt = jnp.concatenate([jnp.where(feat < D_HALF, q, zero), jnp.where(feat >= D_HALF, q, zero)], axis=1)
    rel = (lax.broadcasted_iota(jnp.int32, (tk, tq), 0) - lax.broadcasted_iota(jnp.int32, (tk, tq), 1)).astype(F32)

    def score_fn(c):
        off = pl.multiple_of(c * tk, tk)
        bias = slope * jnp.abs(rel + (off - q0).astype(F32))
        return _dot(k_ref[pl.ds(off, tk), :], qt) - jnp.concatenate([bias, bias], axis=1)

    def vt_fn(c):
        return vt_ref[:, pl.ds(pl.multiple_of(c * tk, tk), tk)]

    _flash_pipeline(seq // tk, score_fn, vt_fn, ((s0_ref, x0_ref), (s1_ref, x1_ref)), m_ref, acc_ref)
    lp = lam_ref[...]
    lam = (jnp.exp(jnp.sum(lp[0:1] * lp[1:2], axis=-1, keepdims=True))
           - jnp.exp(jnp.sum(lp[2:3] * lp[3:4], axis=-1, keepdims=True)) + lam_init)
    on = acc_ref[:HEAD_DIM, :] / acc_ref[HEAD_DIM:HEAD_DIM + 1, :]
    o = on[:, :tq] - lam * on[:, tq:]
    y = o * lax.rsqrt(jnp.mean(o * o, axis=0, keepdims=True) + RMS_EPS) * g_ref[...]
    o_ref[...] = (y * (1.0 - lam_init)).T.astype(o_ref.dtype)


def attn_d(qt, h, vt, slopes_log2, lam_params, subln_g, lam_init, batch, seq, tq, tk):
    n = h.shape[0]
    nq = seq // tq
    return pl.pallas_call(
        functools.partial(_attn_d_kernel, tk=tk, lam_init=lam_init),
        grid=(batch, N_HEADS, nq),
        in_specs=[pl.BlockSpec(memory_space=pltpu.SMEM),
                  pl.BlockSpec((4, D_HALF), lambda b, hh, i: (0, 0)),
                  pl.BlockSpec((HEAD_DIM, 1), lambda b, hh, i: (0, 0)),
                  pl.BlockSpec((HEAD_DIM, tq), lambda b, hh, i: (hh, b * nq + i)),
                  pl.BlockSpec((seq, HEAD_DIM), lambda b, hh, i: (b, COL_DK // HEAD_DIM + hh)),
                  pl.BlockSpec((HEAD_DIM, seq), lambda b, hh, i: (hh, b))],
        out_specs=pl.BlockSpec((tq, HEAD_DIM), lambda b, hh, i: (b * nq + i, hh)),
        out_shape=jax.ShapeDtypeStruct((n, BRANCH_W), BF16),
        scratch_shapes=_flash_scratch(tk, 2 * tq),
        compiler_params=_params(("parallel", "parallel", "parallel")),
        name="attn_d",
    )(slopes_log2, lam_params, subln_g, qt, h, vt)


def _gate_merge_kernel(x_ref, wg_ref, o_ref, wb_ref, out_ref, acc_ref):
    i = pl.program_id(2)
    gate = _dot(x_ref[...], wg_ref[...])
    contrib = _dot(o_ref[...], wb_ref[...]) / (1.0 + jnp.exp(-gate))

    @pl.when(i == 0)
    def _():
        acc_ref[...] = contrib

    @pl.when(i > 0)
    def _():
        acc_ref[...] += contrib

    @pl.when(i == N_BRANCH - 1)
    def _():
        out_ref[...] = acc_ref[...].astype(out_ref.dtype)


def gate_merge(x_bf, w_in_bf, o_stack, w_branch_bf, layer, tm, tn):
    n, d = x_bf.shape
    gcol0 = MIX_COLS // tn
    per = d // tn
    return pl.pallas_call(
        _gate_merge_kernel,
        grid=(n // tm, d // tn, N_BRANCH),
        in_specs=[pl.BlockSpec((tm, d), lambda m, j, i: (m, 0)),
                  pl.BlockSpec((None, d, tn), lambda m, j, i: (layer, 0, gcol0 + i * per + j)),
                  pl.BlockSpec((None, tm, BRANCH_W), lambda m, j, i: (i, m, 0)),
                  pl.BlockSpec((None, None, BRANCH_W, tn), lambda m, j, i: (layer, i, 0, j))],
        out_specs=pl.BlockSpec((tm, tn), lambda m, j, i: (m, j)),
        out_shape=jax.ShapeDtypeStruct((n, d), BF16),
        scratch_shapes=[pltpu.VMEM((tm, tn), F32)],
        compiler_params=_params(("parallel", "parallel", "arbitrary")),
        name="gate_merge",
    )(x_bf, w_in_bf, o_stack, w_branch_bf)


def _layer_norm(y, g, b):
    mu = jnp.mean(y, axis=-1, keepdims=True)
    d = y - mu
    var = jnp.mean(d * d, axis=-1, keepdims=True)
    return d * lax.rsqrt(var + LN_EPS) * g + b


def _out_ln_kernel(m_ref, w_ref, x_ref, g_ref, b_ref, of_ref, ob_ref, *, alpha):
    k = pl.program_id(1)
    part = _dot(m_ref[...], w_ref[...])

    @pl.when(k == 0)
    def _():
        of_ref[...] = part

    @pl.when(k > 0)
    def _():
        of_ref[...] += part

    @pl.when(k == pl.num_programs(1) - 1)
    def _():
        rows = min(of_ref.shape[0], LN_ROWS)

        def body(r, c):
            sl = pl.ds(pl.multiple_of(r * rows, rows), rows)
            out = _layer_norm(alpha * x_ref[sl, :] + of_ref[sl, :], g_ref[...], b_ref[...])
            of_ref[sl, :] = out
            ob_ref[sl, :] = out.astype(ob_ref.dtype)
            return c

        lax.fori_loop(0, of_ref.shape[0] // rows, body, 0)


LN_ROWS = 128


def out_proj_ln(merged, w_out_bf, x, g, b, layer, alpha, tm, tk):
    n, d = x.shape
    return pl.pallas_call(
        functools.partial(_out_ln_kernel, alpha=alpha),
        grid=(n // tm, d // tk),
        in_specs=[pl.BlockSpec((tm, tk), lambda m, k: (m, k)),
                  pl.BlockSpec((None, tk, d), lambda m, k: (layer, k, 0)),
                  pl.BlockSpec((tm, d), lambda m, k: (m, 0), pipeline_mode=pl.Buffered(1)),
                  pl.BlockSpec((None, 1, d), lambda m, k: (layer, 0, 0)),
                  pl.BlockSpec((None, 1, d), lambda m, k: (layer, 0, 0))],
        out_specs=[pl.BlockSpec((tm, d), lambda m, k: (m, 0)), pl.BlockSpec((tm, d), lambda m, k: (m, 0))],
        out_shape=[jax.ShapeDtypeStruct((n, d), F32), jax.ShapeDtypeStruct((n, d), BF16)],
        compiler_params=_params(("parallel", "arbitrary")),
        name="out_proj_ln",
    )(merged, w_out_bf, x, g, b)


def _router_kernel(x_ref, whi_ref, wlo_ref, b_ref, o_ref):
    x = x_ref[...]
    x_hi = x.astype(BF16)
    x_lo = (x - x_hi.astype(F32)).astype(BF16)
    whi = whi_ref[...]
    logits = _dot(x_hi, whi) + _dot(x_lo, whi) + _dot(x_hi, wlo_ref[...]) + b_ref[...]
    lane = lax.broadcasted_iota(jnp.int32, logits.shape, 1)
    big = jnp.int32(LANES)
    gmask = lane < N_GROUPS
    gl = jnp.where(gmask, logits, NEG_BIG)
    gmax = jnp.max(gl, axis=-1, keepdims=True)
    g_top = jnp.min(jnp.where(gmask & (gl == gmax), lane, big), axis=-1, keepdims=True)
    g_w = 1.0 / jnp.sum(jnp.where(gmask, jnp.exp(gl - gmax), 0.0), axis=-1, keepdims=True)
    lo = N_GROUPS + g_top * EXPERTS_PER_GROUP
    emask = (lane >= lo) & (lane < lo + EXPERTS_PER_GROUP)
    e1 = jnp.where(emask, logits, NEG_BIG)
    v1 = jnp.max(e1, axis=-1, keepdims=True)
    i1 = jnp.min(jnp.where(emask & (e1 == v1), lane, big), axis=-1, keepdims=True)
    emask2 = emask & (lane != i1)
    e2 = jnp.where(emask2, logits, NEG_BIG)
    v2 = jnp.max(e2, axis=-1, keepdims=True)
    i2 = jnp.min(jnp.where(emask2 & (e2 == v2), lane, big), axis=-1, keepdims=True)
    t = jnp.exp(v2 - v1)
    w1 = g_w / (1.0 + t)
    w2 = g_w * t / (1.0 + t)
    o_ref[...] = (jnp.where(lane == 0, (i1 - N_GROUPS).astype(F32), 0.0)
                  + jnp.where(lane == 1, (i2 - N_GROUPS).astype(F32), 0.0)
                  + jnp.where(lane == 2, w1, 0.0) + jnp.where(lane == 3, w2, 0.0))


def router(x, w_hi, w_lo, bias, tm):
    n, d = x.shape
    return pl.pallas_call(
        _router_kernel,
        grid=(n // tm,),
        in_specs=[pl.BlockSpec((tm, d), lambda m: (m, 0)),
                  pl.BlockSpec((d, LANES), lambda m: (0, 0)),
                  pl.BlockSpec((d, LANES), lambda m: (0, 0)),
                  pl.BlockSpec((1, LANES), lambda m: (0, 0))],
        out_specs=pl.BlockSpec((tm, LANES), lambda m: (m, 0)),
        out_shape=jax.ShapeDtypeStruct((n, LANES), F32),
        compiler_params=_params(("parallel",)),
        name="router",
    )(x, w_hi, w_lo, bias)


MOE_TM = 256


def moe_routing(route, n_tok, tm):
    n_pairs = 2 * n_tok
    e = route[:, 0:2].astype(jnp.int32).reshape(n_pairs)
    w = route[:, 2:4].reshape(n_pairs)
    order = jnp.argsort(e, stable=True)
    counts = jnp.sum((e[:, None] == jnp.arange(N_EXPERTS)[None, :]).astype(jnp.int32), axis=0)
    tiles_per = (counts + tm - 1) // tm
    tile_end = jnp.cumsum(tiles_per)
    tile_start = tile_end - tiles_per
    n_tiles = tile_end[-1]
    t_max = n_pairs // tm + N_EXPERTS
    t_idx = jnp.arange(t_max)
    tile_expert = jnp.minimum(jnp.searchsorted(tile_end, t_idx, side="right"), N_EXPERTS - 1).astype(jnp.int32)
    first = jnp.cumsum(counts) - counts
    slot = jnp.arange(t_max * tm)
    s_tile = slot // tm
    s_exp = tile_expert[s_tile]
    rank = slot - tile_start[s_exp] * tm
    valid = (s_tile < n_tiles) & (rank < counts[s_exp])
    pair = order[jnp.clip(first[s_exp] + rank, 0, n_pairs - 1)]
    tok = jnp.where(valid, pair // 2, 0).astype(jnp.int32)
    spare = jnp.cumsum(~valid) - (~valid)
    dst = jnp.where(valid, pair, n_pairs + spare).astype(jnp.int32)
    wslot = jnp.where(valid, w[pair], 0.0)
    return tile_expert, n_tiles.astype(jnp.int32)[None], tok, dst, jnp.broadcast_to(wslot[:, None], (t_max * tm, LANES))


def _moe_kernel(te_ref, nt_ref, tok_ref, dst_ref, x_hbm, w_ref, wgu_ref, wd_ref, out_hbm,
                xbuf, ybuf, gsem, ssem):
    t = pl.program_id(0)
    nt = nt_ref[0]
    tm = xbuf.shape[1]
    f = wd_ref.shape[0]

    def gather_row(tile, r, slot):
        tok = tok_ref[tile * tm + r]
        return pltpu.make_async_copy(x_hbm.at[pl.ds(tok, 1), :], xbuf.at[slot, pl.ds(r, 1), :], gsem.at[slot])

    def scatter_row(tile, r, slot):
        d = dst_ref[tile * tm + r]
        return pltpu.make_async_copy(ybuf.at[slot, pl.ds(r, 1), :], out_hbm.at[pl.ds(d, 1), :], ssem.at[slot])

    def for_rows(fn):
        def body(r, c):
            fn(r)
            return c
        lax.fori_loop(0, tm, body, 0, unroll=8)

    @pl.when((t == 0) & (nt > 0))
    def _():
        for_rows(lambda r: gather_row(0, r, 0).start())

    @pl.when(t + 1 < nt)
    def _():
        for_rows(lambda r: gather_row(t + 1, r, (t + 1) % 2).start())

    @pl.when(t < nt)
    def _():
        slot = t % 2
        for_rows(lambda r: gather_row(t, r, slot).wait())
        xb = xbuf[slot].astype(BF16)
        gu = _dot(xb, wgu_ref[...])
        gate = gu[:, :f]
        w = w_ref[...]
        hmid = (gate / (1.0 + jnp.exp(-gate))) * gu[:, f:] * jnp.concatenate([w] * (f // LANES), axis=1)
        y = _dot(hmid.astype(BF16), wd_ref[...])

        @pl.when(t >= 2)
        def _():
            for_rows(lambda r: scatter_row(t - 2, r, slot).wait())

        ybuf[slot] = y
        for_rows(lambda r: scatter_row(t, r, slot).start())

    @pl.when(t == pl.num_programs(0) - 1)
    def _():
        for back in (1, 2):
            @pl.when(nt >= back)
            def _():
                for_rows(lambda r: scatter_row(nt - back, r, (nt - back) % 2).wait())


def moe_experts(x, routing, wgu_bf, wd_bf, layer, tm):
    n, d = x.shape
    f = wd_bf.shape[-2]
    tile_expert, n_tiles, tok, dst, wslot = routing
    t_max = tile_expert.shape[0]
    grid_spec = pltpu.PrefetchScalarGridSpec(
        num_scalar_prefetch=4,
        grid=(t_max,),
        in_specs=[pl.BlockSpec(memory_space=pl.ANY),
                  pl.BlockSpec((tm, LANES), lambda t, te, nt, tk, ds: (t, 0)),
                  pl.BlockSpec((None, None, d, 2 * f), lambda t, te, nt, tk, ds: (layer, te[t], 0, 0)),
                  pl.BlockSpec((None, None, f, d), lambda t, te, nt, tk, ds: (layer, te[t], 0, 0))],
        out_specs=pl.BlockSpec(memory_space=pl.ANY),
        scratch_shapes=[pltpu.VMEM((2, tm, d), F32), pltpu.VMEM((2, tm, d), F32),
                        pltpu.SemaphoreType.DMA((2,)), pltpu.SemaphoreType.DMA((2,))])
    return pl.pallas_call(
        _moe_kernel,
        grid_spec=grid_spec,
        out_shape=jax.ShapeDtypeStruct((t_max * tm, d), F32),
        compiler_params=_params(("arbitrary",)),
        name="moe_experts",
    )(tile_expert, n_tiles, tok, dst, x, wslot, wgu_bf, wd_bf)


def _pair_ln_kernel(y_ref, x_ref, g_ref, b_ref, of_ref, ob_ref, *, alpha):
    d = x_ref.shape[1]
    y = y_ref[:, :d] + y_ref[:, d:]
    out = _layer_norm(alpha * x_ref[...] + y, g_ref[...], b_ref[...])
    of_ref[...] = out
    ob_ref[...] = out.astype(ob_ref.dtype)


def pair_sum_ln(y_pairs, x, g, b, layer, alpha, tm):
    n, d = x.shape
    y2 = y_pairs.reshape(y_pairs.shape[0] // 2, 2 * d)
    return pl.pallas_call(
        functools.partial(_pair_ln_kernel, alpha=alpha),
        grid=(n // tm,),
        in_specs=[pl.BlockSpec((tm, 2 * d), lambda m: (m, 0)),
                  pl.BlockSpec((tm, d), lambda m: (m, 0)),
                  pl.BlockSpec((None, 1, d), lambda m: (layer, 0, 0)),
                  pl.BlockSpec((None, 1, d), lambda m: (layer, 0, 0))],
        out_specs=[pl.BlockSpec((tm, d), lambda m: (m, 0)), pl.BlockSpec((tm, d), lambda m: (m, 0))],
        out_shape=[jax.ShapeDtypeStruct((n, d), F32), jax.ShapeDtypeStruct((n, d), BF16)],
        compiler_params=_params(("parallel",)),
        name="pair_sum_ln",
    )(y2, x, g, b)


def _rope_tables(seq):
    t = jnp.arange(seq)
    row_id = (t // GRID_W).astype(F32)
    col_id = (t % GRID_W).astype(F32)
    nfreq = HEAD_DIM // 4
    inv = ROPE_THETA ** (-jnp.arange(nfreq, dtype=F32) / nfreq)
    ar = row_id[:, None] * inv
    ac = col_id[:, None] * inv
    cos = jnp.concatenate([jnp.cos(ar), jnp.cos(ar), jnp.cos(ac), jnp.cos(ac)], axis=-1)
    sin = jnp.concatenate([-jnp.sin(ar), jnp.sin(ar), -jnp.sin(ac), jnp.sin(ac)], axis=-1)
    return cos, sin


def _tile(n, pref):
    t = min(n, pref)
    while n % t:
        t //= 2
    return t


def kernel(x, w_in, attn_sink, qk_norm_q, qk_norm_k, ret_decay_fwd, ret_decay_bwd, ret_norm_g,
           diff_lambda_q1, diff_lambda_k1, diff_lambda_q2, diff_lambda_k2, diff_subln_g,
           w_branch, w_out, ln_mix_g, ln_mix_b, router_group, router_group_b, router_expert,
           router_expert_b, expert_w_gate, expert_w_up, expert_w_down, ln_ffn_g, ln_ffn_b):
    batch, seq, d = x.shape
    depth = w_in.shape[0]
    n = batch * seq
    alpha = (2.0 * depth) ** 0.25
    assert w_in.shape[2] == MIX_COLS + N_BRANCH * d and seq % BLOCK == 0 and seq % GRID_W == 0

    colscale = jnp.ones((w_in.shape[2],), F32)
    colscale = colscale.at[COL_AQ:COL_AQ + BRANCH_W].set(HEAD_DIM ** -0.5 * LOG2E)
    colscale = colscale.at[COL_DQ:COL_DQ + BRANCH_W].set(D_HALF ** -0.5 * LOG2E)
    w_in_bf = (w_in * colscale).astype(BF16)
    w_branch_bf = w_branch.astype(BF16)
    w_out_bf = w_out.astype(BF16)
    wgu_bf = jnp.concatenate([expert_w_gate.astype(BF16), expert_w_up.astype(BF16)], axis=-1)
    wd_bf = expert_w_down.astype(BF16)
    ln_mix_g3, ln_mix_b3 = ln_mix_g[:, None, :], ln_mix_b[:, None, :]
    ln_ffn_g3, ln_ffn_b3 = ln_ffn_g[:, None, :], ln_ffn_b[:, None, :]

    cos_t, sin_t = _rope_tables(seq)
    slopes_log2 = jnp.asarray([LOG2E * 2.0 ** (-(i + 1.0)) for i in range(N_HEADS)], F32)

    tm_big = _tile(n, 1024)
    xf = x.reshape(n, d)
    xb = xf.astype(BF16)
    for l in range(depth):
        h = in_proj(xb, w_in_bf, l, MIX_COLS, tm_big, 1024)
        o_a = attn_a(h, attn_sink[l].astype(F32) * LOG2E, batch, seq)
        gains = jnp.concatenate([jnp.tile(qk_norm_q[l], N_HEADS) * (HEAD_DIM ** -0.5 * LOG2E),
                                 jnp.tile(qk_norm_k[l], KV_HEADS)])[None, :]
        qt_b, k_b = qk_prep(h, gains, cos_t, sin_t, batch, seq, _tile(seq, 512))
        vt_b = h[:, COL_BV:COL_BV + KV_HEADS * HEAD_DIM].T
        o_b = attn_b(qt_b, k_b, vt_b, batch, seq, _tile(seq, 256), _tile(seq // 2, 512))
        tabs_f = retention_tables(ret_decay_fwd[l], HEAD_DIM ** -0.5, False)
        tabs_b = retention_tables(ret_decay_bwd[l], HEAD_DIM ** -0.5, True)
        o_c = retention(h, tabs_f, tabs_b, ret_norm_g[l][None, :], batch, seq)
        lam_init = 0.8 - 0.6 * math.exp(-0.3 * l)
        lam_params = jnp.stack([diff_lambda_q1[l], diff_lambda_k1[l], diff_lambda_q2[l], diff_lambda_k2[l]]).astype(F32)
        qt_d = h[:, COL_DQ:COL_DQ + BRANCH_W].T
        vt_d = h[:, COL_DV:COL_DV + BRANCH_W].T
        o_d = attn_d(qt_d, h, vt_d, slopes_log2, lam_params, diff_subln_g[l][:, None], lam_init, batch, seq,
                     _tile(seq, 512), _tile(seq // 2, 512))
        merged = gate_merge(xb, w_in_bf, jnp.stack([o_a, o_b, o_c, o_d]), w_branch_bf, l, tm_big, 512)
        xf, xb = out_proj_ln(merged, w_out_bf, xf, ln_mix_g3, ln_mix_b3, l, alpha, _tile(n, 512), _tile(d, 512))
        w_r = jnp.concatenate([router_group[l], router_expert[l]], axis=1)
        w_r = jnp.pad(w_r, ((0, 0), (0, LANES - w_r.shape[1])))
        w_r_hi = w_r.astype(BF16)
        w_r_lo = (w_r - w_r_hi.astype(F32)).astype(BF16)
        b_r = jnp.pad(jnp.concatenate([router_group_b[l], router_expert_b[l]]), (0, LANES - N_GROUPS - N_EXPERTS))[None, :]
        route = router(xf, w_r_hi, w_r_lo, b_r, _tile(n, 512))
        y_pairs = moe_experts(xf, moe_routing(route, n, MOE_TM), wgu_bf, wd_bf, l, MOE_TM)
        xf, xb = pair_sum_ln(y_pairs, xf, ln_ffn_g3, ln_ffn_b3, l, alpha, _tile(n, 256))
    return xf.reshape(batch, seq, d)
```

```python
import functools
import math

import jax
import jax.numpy as jnp
from jax import lax
from jax.experimental import pallas as pl
from jax.experimental.pallas import tpu as pltpu

F32 = jnp.float32
BF16 = jnp.bfloat16

HEAD_DIM = 128
BLOCK = 128
GRID_W = 64
N_HEADS = 8
KV_HEADS = 2
D_HALF = HEAD_DIM // 2
N_BRANCH = 4
BRANCH_W = N_HEADS * HEAD_DIM
ROPE_THETA = 10000.0
N_GROUPS = 4
EXPERTS_PER_GROUP = 4
N_EXPERTS = N_GROUPS * EXPERTS_PER_GROUP
LN_EPS = 1e-5
RMS_EPS = 1e-6
NEG_BIG = -1e30
LOG2E = 1.4426950408889634
LANES = 128
VMEM_LIMIT = 56 * 1024 * 1024

COL_AQ, COL_AK, COL_AV = 0, 1024, 1280
COL_BQ, COL_BK, COL_BV = 1536, 2560, 2816
COL_CQ, COL_CK, COL_CV, COL_CG = 3072, 4096, 5120, 6144
COL_DQ, COL_DK, COL_DV = 7168, 8192, 9216
MIX_COLS = 10240


def _params(sem):
    return pltpu.CompilerParams(dimension_semantics=sem, vmem_limit_bytes=VMEM_LIMIT)


def _nt_dot(a, b):
    return lax.dot_general(a, b, (((1,), (1,)), ((), ())), preferred_element_type=F32)


def _tn_dot(a, b):
    return lax.dot_general(a, b, (((0,), (0,)), ((), ())), preferred_element_type=F32)


def _dot(a, b):
    return jnp.dot(a, b, preferred_element_type=F32)


def _mm_kernel(x_ref, w_ref, o_ref):
    o_ref[...] = _dot(x_ref[...], w_ref[...]).astype(o_ref.dtype)


def in_proj(x_bf, w_in_bf, layer, n_cols, tm, tn):
    n, k = x_bf.shape
    return pl.pallas_call(
        _mm_kernel,
        grid=(n // tm, n_cols // tn),
        in_specs=[pl.BlockSpec((tm, k), lambda i, j: (i, 0)),
                  pl.BlockSpec((None, k, tn), lambda i, j: (layer, 0, j))],
        out_specs=pl.BlockSpec((tm, tn), lambda i, j: (i, j)),
        out_shape=jax.ShapeDtypeStruct((n, n_cols), BF16),
        compiler_params=_params(("parallel", "parallel")),
        name="in_proj",
    )(x_bf, w_in_bf)


def _attn_a_kernel(sink_ref, q_ref, kp_ref, kc_ref, kn_ref, vp_ref, vc_ref, vn_ref, o_ref, *, slopes, nb):
    i = pl.program_id(1)
    rep = N_HEADS // KV_HEADS
    rows = rep * BLOCK
    row = lax.broadcasted_iota(jnp.int32, (rows, 3 * BLOCK), 0)
    col = lax.broadcasted_iota(jnp.int32, (rows, 3 * BLOCK), 1)
    rel = (row % BLOCK) - (col - BLOCK)
    dist = jnp.abs(rel)
    valid = (dist <= BLOCK) & ((col >= BLOCK) | (i > 0)) & ((col < 2 * BLOCK) | (i < nb - 1))
    distf = dist.astype(F32)
    rowc = lax.broadcasted_iota(jnp.int32, (rows, 1), 0)
    for g in range(KV_HEADS):
        q = jnp.concatenate([q_ref[:, (g * rep + r) * HEAD_DIM:(g * rep + r + 1) * HEAD_DIM]
                             for r in range(rep)], axis=0)
        ks = slice(g * HEAD_DIM, (g + 1) * HEAD_DIM)
        k = jnp.concatenate([kp_ref[:, ks], kc_ref[:, ks], kn_ref[:, ks]], axis=0)
        v = jnp.concatenate([vp_ref[:, ks], vc_ref[:, ks], vn_ref[:, ks]], axis=0)
        slope = jnp.full((rows, 1), slopes[g * rep], F32)
        sink = jnp.full((rows, 1), sink_ref[g * rep], F32)
        for r in range(1, rep):
            slope = jnp.where(rowc >= r * BLOCK, slopes[g * rep + r], slope)
            sink = jnp.where(rowc >= r * BLOCK, sink_ref[g * rep + r], sink)
        s = _nt_dot(q, k) - slope * distf
        s = jnp.where(valid, s, NEG_BIG)
        m = jnp.maximum(jnp.max(s, axis=-1, keepdims=True), sink)
        p = jnp.exp2(s - m)
        denom = jnp.sum(p, axis=-1, keepdims=True) + jnp.exp2(sink - m)
        o = _dot(p.astype(BF16), v) / denom
        for r in range(rep):
            hh = g * rep + r
            o_ref[:, hh * HEAD_DIM:(hh + 1) * HEAD_DIM] = o[r * BLOCK:(r + 1) * BLOCK].astype(o_ref.dtype)


def attn_a(h, sink_log2, batch, seq):
    n = h.shape[0]
    nb = seq // BLOCK
    slopes = tuple(LOG2E * 2.0 ** (-(i + 1.0)) for i in range(N_HEADS))
    kvw = KV_HEADS * HEAD_DIM
    kcol, vcol = COL_AK // kvw, COL_AV // kvw

    def prev(b, i):
        return b * nb + jnp.maximum(i - 1, 0)

    def nxt(b, i):
        return b * nb + jnp.minimum(i + 1, nb - 1)

    return pl.pallas_call(
        functools.partial(_attn_a_kernel, slopes=slopes, nb=nb),
        grid=(batch, nb),
        in_specs=[pl.BlockSpec(memory_space=pltpu.SMEM),
                  pl.BlockSpec((BLOCK, BRANCH_W), lambda b, i: (b * nb + i, COL_AQ // BRANCH_W)),
                  pl.BlockSpec((BLOCK, kvw), lambda b, i: (prev(b, i), kcol)),
                  pl.BlockSpec((BLOCK, kvw), lambda b, i: (b * nb + i, kcol)),
                  pl.BlockSpec((BLOCK, kvw), lambda b, i: (nxt(b, i), kcol)),
                  pl.BlockSpec((BLOCK, kvw), lambda b, i: (prev(b, i), vcol)),
                  pl.BlockSpec((BLOCK, kvw), lambda b, i: (b * nb + i, vcol)),
                  pl.BlockSpec((BLOCK, kvw), lambda b, i: (nxt(b, i), vcol))],
        out_specs=pl.BlockSpec((BLOCK, BRANCH_W), lambda b, i: (b * nb + i, 0)),
        out_shape=jax.ShapeDtypeStruct((n, BRANCH_W), BF16),
        compiler_params=_params(("parallel", "parallel")),
        name="attn_a",
    )(sink_log2, h, h, h, h, h, h, h)


def _norm_rope(x, g, cos, sin, first_quarter):
    y = x * lax.rsqrt(jnp.mean(x * x, axis=-1, keepdims=True) + RMS_EPS) * g
    partner = jnp.where(first_quarter, pltpu.roll(y, HEAD_DIM - HEAD_DIM // 4, 1), pltpu.roll(y, HEAD_DIM // 4, 1))
    return y * cos + partner * sin


def _qk_prep_kernel(q0_ref, q1_ref, k_ref, g_ref, cos_ref, sin_ref, qt_ref, ko_ref):
    lane = lax.broadcasted_iota(jnp.int32, (k_ref.shape[0], HEAD_DIM), 1)
    first_quarter = (lane % (HEAD_DIM // 2)) < (HEAD_DIM // 4)
    cos = cos_ref[...]
    sin = sin_ref[...]
    half = N_HEADS // 2
    for hh in range(N_HEADS):
        src = q0_ref if hh < half else q1_ref
        sl = slice((hh % half) * HEAD_DIM, (hh % half + 1) * HEAD_DIM)
        gs = slice(hh * HEAD_DIM, (hh + 1) * HEAD_DIM)
        y = _norm_rope(src[:, sl].astype(F32), g_ref[:, gs], cos, sin, first_quarter)
        qt_ref[gs, :] = y.T.astype(qt_ref.dtype)
    for hh in range(KV_HEADS):
        sl = slice(hh * HEAD_DIM, (hh + 1) * HEAD_DIM)
        gs = slice((N_HEADS + hh) * HEAD_DIM, (N_HEADS + hh + 1) * HEAD_DIM)
        ko_ref[:, sl] = _norm_rope(k_ref[:, sl].astype(F32), g_ref[:, gs], cos, sin, first_quarter).astype(ko_ref.dtype)


def qk_prep(h, gains, cos_t, sin_t, batch, seq, tm):
    n = h.shape[0]
    qw = BRANCH_W // 2
    kw = KV_HEADS * HEAD_DIM
    nblk = seq // tm
    return pl.pallas_call(
        _qk_prep_kernel,
        grid=(batch, nblk),
        in_specs=[pl.BlockSpec((tm, qw), lambda b, i: (b * nblk + i, COL_BQ // qw)),
                  pl.BlockSpec((tm, qw), lambda b, i: (b * nblk + i, COL_BQ // qw + 1)),
                  pl.BlockSpec((tm, kw), lambda b, i: (b * nblk + i, COL_BK // kw)),
                  pl.BlockSpec((1, BRANCH_W + kw), lambda b, i: (0, 0)),
                  pl.BlockSpec((tm, HEAD_DIM), lambda b, i: (i, 0)),
                  pl.BlockSpec((tm, HEAD_DIM), lambda b, i: (i, 0))],
        out_specs=[pl.BlockSpec((BRANCH_W, tm), lambda b, i: (0, b * nblk + i)),
                   pl.BlockSpec((tm, kw), lambda b, i: (b * nblk + i, 0))],
        out_shape=[jax.ShapeDtypeStruct((BRANCH_W, n), BF16), jax.ShapeDtypeStruct((n, kw), BF16)],
        compiler_params=_params(("parallel", "parallel")),
        name="qk_prep",
    )(h, h, h, gains, cos_t, sin_t)


ONES_ROWS = 16


def _flash_pipeline(n_chunks, score_fn, vt_fn, bufs, m_ref, acc_ref):
    per_trip = FLASH_CHUNKS_PER_TRIP if n_chunks % FLASH_CHUNKS_PER_TRIP == 0 else 2
    assert n_chunks % per_trip == 0
    scores, accumulate = _flash_stages(m_ref, acc_ref)
    scores(score_fn(0), bufs[0])

    def body(i, carry):
        c = per_trip * i
        for j in range(per_trip):
            scores(score_fn(jnp.minimum(c + j + 1, n_chunks - 1)), bufs[(j + 1) % 2])
            accumulate(vt_fn(c + j), bufs[j % 2])
        return carry

    lax.fori_loop(0, n_chunks // per_trip, body, 0)


FLASH_CHUNKS_PER_TRIP = 4


def _flash_init(m_ref, acc_ref):
    m_ref[...] = jnp.full(m_ref.shape, NEG_BIG, F32)
    acc_ref[...] = jnp.zeros(acc_ref.shape, F32)


def _flash_stages(m_ref, acc_ref):
    def scores(s_and_offset, buf):
        s, offset = s_and_offset
        s_ref, x_ref, c_ref = buf
        s_ref[...] = s
        x_ref[...] = jnp.max(s, axis=0, keepdims=True) + offset
        c_ref[...] = jnp.full(c_ref.shape, offset, F32)

    def accumulate(vt, buf):
        s_ref, x_ref, c_ref = buf
        m_old = m_ref[...]
        m_new = jnp.maximum(m_old, x_ref[...])
        a = jnp.exp2(m_old - m_new)
        p = jnp.exp2(s_ref[...] - (m_new - c_ref[...])).astype(BF16)
        vt1 = jnp.concatenate([vt, jnp.ones((ONES_ROWS, vt.shape[1]), vt.dtype)], axis=0)
        acc_ref[...] = a * acc_ref[...] + _dot(vt1, p)
        m_ref[...] = m_new

    return scores, accumulate


def _attn_b_kernel(qt_ref, k_ref, vt_ref, o_ref, m_ref, acc_ref, *bufs, tk):
    rep = qt_ref.shape[0] // HEAD_DIM
    tq = qt_ref.shape[1]
    seq = k_ref.shape[0]
    qt = jnp.concatenate([qt_ref[r * HEAD_DIM:(r + 1) * HEAD_DIM, :] for r in range(rep)], axis=1)

    def score_fn(c):
        return _dot(k_ref[pl.ds(pl.multiple_of(c * tk, tk), tk), :], qt), 0.0

    def vt_fn(c):
        return vt_ref[:, pl.ds(pl.multiple_of(c * tk, tk), tk)]

    _flash_init(m_ref, acc_ref)
    _flash_pipeline(seq // tk, score_fn, vt_fn, (bufs[:3], bufs[3:]), m_ref, acc_ref)
    o = acc_ref[:HEAD_DIM, :] / acc_ref[HEAD_DIM:HEAD_DIM + 1, :]
    for r in range(rep):
        o_ref[:, r * HEAD_DIM:(r + 1) * HEAD_DIM] = o[:, r * tq:(r + 1) * tq].T.astype(o_ref.dtype)


def _flash_scratch(tk, width):
    row = pltpu.VMEM((1, width), F32)
    return [row, pltpu.VMEM((HEAD_DIM + ONES_ROWS, width), F32),
            pltpu.VMEM((tk, width), F32), row, row,
            pltpu.VMEM((tk, width), F32), row, row]


def attn_b(qt, kb, vt, batch, seq, tq, tk):
    n = kb.shape[0]
    rep = N_HEADS // KV_HEADS
    nq = seq // tq
    width = rep * tq
    return pl.pallas_call(
        functools.partial(_attn_b_kernel, tk=tk),
        grid=(batch, KV_HEADS, nq),
        in_specs=[pl.BlockSpec((rep * HEAD_DIM, tq), lambda b, g, i: (g, b * nq + i)),
                  pl.BlockSpec((seq, HEAD_DIM), lambda b, g, i: (b, g)),
                  pl.BlockSpec((HEAD_DIM, seq), lambda b, g, i: (g, b))],
        out_specs=pl.BlockSpec((tq, rep * HEAD_DIM), lambda b, g, i: (b * nq + i, g)),
        out_shape=jax.ShapeDtypeStruct((n, BRANCH_W), BF16),
        scratch_shapes=_flash_scratch(tk, width),
        compiler_params=_params(("parallel", "parallel", "parallel")),
        name="attn_b",
    )(qt, kb, vt)


def _retention_kernel(*refs, final):
    if final:
        (q_ref, k_ref, v_ref, inner_ref, qdec_ref, kdec_ref, cdec_ref, prev_ref, gate_ref, ng_ref,
         o_ref, state_ref) = refs
    else:
        q_ref, k_ref, v_ref, inner_ref, qdec_ref, kdec_ref, cdec_ref, o_ref, state_ref = refs

    @pl.when(pl.program_id(1) == 0)
    def _():
        state_ref[...] = jnp.zeros(state_ref.shape, F32)

    for hh in range(N_HEADS):
        sl = slice(hh * HEAD_DIM, (hh + 1) * HEAD_DIM)
        q = q_ref[:, sl]
        k = k_ref[:, sl]
        v = v_ref[:, sl]
        state = state_ref[hh]
        inner = _nt_dot(q, k) * inner_ref[hh]
        o = _dot(inner.astype(BF16), v) + _dot(q, state.astype(BF16)) * qdec_ref[hh]
        kd = (k.astype(F32) * kdec_ref[hh]).astype(BF16)
        state_ref[hh] = state * cdec_ref[hh] + _tn_dot(kd, v)
        if final:
            r = o + prev_ref[:, sl]
            mu = jnp.mean(r, axis=-1, keepdims=True)
            d = r - mu
            var = jnp.mean(d * d, axis=-1, keepdims=True)
            rn = d * lax.rsqrt(var + LN_EPS) * ng_ref[:, sl]
            gate = gate_ref[:, sl].astype(F32)
            silu = gate / (1.0 + jnp.exp(-gate))
            o_ref[:, sl] = (silu * rn).astype(o_ref.dtype)
        else:
            o_ref[:, sl] = o


def retention(h, tabs_f, tabs_b, norm_g, batch, seq):
    n = h.shape[0]
    nb = seq // BLOCK
    w = BRANCH_W
    tab_specs = [pl.BlockSpec((N_HEADS, BLOCK, BLOCK), lambda b, c: (0, 0, 0)),
                 pl.BlockSpec((N_HEADS, BLOCK, HEAD_DIM), lambda b, c: (0, 0, 0)),
                 pl.BlockSpec((N_HEADS, BLOCK, HEAD_DIM), lambda b, c: (0, 0, 0)),
                 pl.BlockSpec((N_HEADS, 1, HEAD_DIM), lambda b, c: (0, 0, 0))]

    def specs(rowmap):
        return [pl.BlockSpec((BLOCK, w), lambda b, c: (rowmap(b, c), COL_CQ // w)),
                pl.BlockSpec((BLOCK, w), lambda b, c: (rowmap(b, c), COL_CK // w)),
                pl.BlockSpec((BLOCK, w), lambda b, c: (rowmap(b, c), COL_CV // w))]

    fmap = lambda b, c: b * nb + c
    bmap = lambda b, c: b * nb + (nb - 1 - c)
    scratch = [pltpu.VMEM((N_HEADS, HEAD_DIM, HEAD_DIM), F32)]
    o_f = pl.pallas_call(
        functools.partial(_retention_kernel, final=False),
        grid=(batch, nb),
        in_specs=specs(fmap) + tab_specs,
        out_specs=pl.BlockSpec((BLOCK, w), lambda b, c: (fmap(b, c), 0)),
        out_shape=jax.ShapeDtypeStruct((n, w), F32),
        scratch_shapes=scratch,
        compiler_params=_params(("parallel", "arbitrary")),
        name="retention_fwd",
    )(h, h, h, *tabs_f)
    return pl.pallas_call(
        functools.partial(_retention_kernel, final=True),
        grid=(batch, nb),
        in_specs=specs(bmap) + tab_specs + [
            pl.BlockSpec((BLOCK, w), lambda b, c: (bmap(b, c), 0)),
            pl.BlockSpec((BLOCK, w), lambda b, c: (bmap(b, c), COL_CG // w)),
            pl.BlockSpec((1, w), lambda b, c: (0, 0))],
        out_specs=pl.BlockSpec((BLOCK, w), lambda b, c: (bmap(b, c), 0)),
        out_shape=jax.ShapeDtypeStruct((n, w), BF16),
        scratch_shapes=scratch,
        compiler_params=_params(("parallel", "arbitrary")),
        name="retention_bwd",
    )(h, h, h, *tabs_b, o_f, h, norm_g)


def retention_tables(dec, scale, backward):
    lg = jnp.log1p(-jnp.exp(dec.astype(F32)))
    pos = jnp.arange(BLOCK, dtype=F32)
    diff = pos[:, None] - pos[None, :]
    if backward:
        diff = -diff
        mask = diff > 0
        qexp = BLOCK - pos
        kexp = pos
    else:
        mask = diff >= 0
        qexp = pos + 1.0
        kexp = BLOCK - 1.0 - pos
    inner = jnp.where(mask[None], jnp.exp(lg[:, None, None] * jnp.where(mask, diff, 0.0)[None]), 0.0) * scale
    qdec = jnp.broadcast_to(jnp.exp(lg[:, None] * qexp)[:, :, None], (N_HEADS, BLOCK, HEAD_DIM))
    kdec = jnp.broadcast_to((jnp.exp(lg[:, None] * kexp) * scale)[:, :, None], (N_HEADS, BLOCK, HEAD_DIM))
    cdec = jnp.broadcast_to(jnp.exp(lg * BLOCK)[:, None, None], (N_HEADS, 1, HEAD_DIM))
    return inner, qdec, kdec, cdec


def _attn_d_kernel(slope_ref, lam_ref, g_ref, q_ref, k_ref, vt_ref, o_ref, m_ref, acc_ref, *bufs, tk, lam_init):
    tq = q_ref.shape[1]
    seq = k_ref.shape[0]
    slope = slope_ref[pl.program_id(1)]
    q0 = pl.program_id(2) * tq
    feat = lax.broadcasted_iota(jnp.int32, (HEAD_DIM, tq), 0)
    q = q_ref[...]
    zero = jnp.zeros_like(q)
    qt = jnp.concatenate([jnp.where(feat < D_HALF, q, zero), jnp.where(feat >= D_HALF, q, zero)], axis=1)
    rel = (lax.broadcasted_iota(jnp.int32, (tk, tq), 0) - lax.broadcasted_iota(jnp.int32, (tk, tq), 1)).astype(F32)

    def score_fn(c):
        off = pl.multiple_of(c * tk, tk)
        bias = slope * jnp.abs(rel + (off - q0).astype(F32))
        return _dot(k_ref[pl.ds(off, tk), :], qt) - jnp.concatenate([bias, bias], axis=1), 0.0

    def vt_fn(c):
        return vt_ref[:, pl.ds(pl.multiple_of(c * tk, tk), tk)]

    _flash_init(m_ref, acc_ref)
    _flash_pipeline(seq // tk, score_fn, vt_fn, (bufs[:3], bufs[3:]), m_ref, acc_ref)
    lp = lam_ref[...]
    lam = (jnp.exp(jnp.sum(lp[0:1] * lp[1:2], axis=-1, keepdims=True))
           - jnp.exp(jnp.sum(lp[2:3] * lp[3:4], axis=-1, keepdims=True)) + lam_init)
    on = acc_ref[:HEAD_DIM, :] / acc_ref[HEAD_DIM:HEAD_DIM + 1, :]
    o = on[:, :tq] - lam * on[:, tq:]
    y = o * lax.rsqrt(jnp.mean(o * o, axis=0, keepdims=True) + RMS_EPS) * g_ref[...]
    o_ref[...] = (y * (1.0 - lam_init)).T.astype(o_ref.dtype)


def attn_d(qt, h, vt, slopes_log2, lam_params, subln_g, lam_init, batch, seq, tq, tk):
    n = h.shape[0]
    nq = seq // tq
    return pl.pallas_call(
        functools.partial(_attn_d_kernel, tk=tk, lam_init=lam_init),
        grid=(batch, N_HEADS, nq),
        in_specs=[pl.BlockSpec(memory_space=pltpu.SMEM),
                  pl.BlockSpec((4, D_HALF), lambda b, hh, i: (0, 0)),
                  pl.BlockSpec((HEAD_DIM, 1), lambda b, hh, i: (0, 0)),
                  pl.BlockSpec((HEAD_DIM, tq), lambda b, hh, i: (hh, b * nq + i)),
                  pl.BlockSpec((seq, HEAD_DIM), lambda b, hh, i: (b, COL_DK // HEAD_DIM + hh)),
                  pl.BlockSpec((HEAD_DIM, seq), lambda b, hh, i: (hh, b))],
        out_specs=pl.BlockSpec((tq, HEAD_DIM), lambda b, hh, i: (b * nq + i, hh)),
        out_shape=jax.ShapeDtypeStruct((n, BRANCH_W), BF16),
        scratch_shapes=_flash_scratch(tk, 2 * tq),
        compiler_params=_params(("parallel", "parallel", "parallel")),
        name="attn_d",
    )(slopes_log2, lam_params, subln_g, qt, h, vt)


def _gate_merge_kernel(x_ref, wg_ref, o_ref, wb_ref, out_ref, acc_ref):
    i = pl.program_id(2)
    gate = _dot(x_ref[...], wg_ref[...])
    contrib = _dot(o_ref[...], wb_ref[...]) / (1.0 + jnp.exp(-gate))

    @pl.when(i == 0)
    def _():
        acc_ref[...] = contrib

    @pl.when(i > 0)
    def _():
        acc_ref[...] += contrib

    @pl.when(i == N_BRANCH - 1)
    def _():
        out_ref[...] = acc_ref[...].astype(out_ref.dtype)


def gate_merge(x_bf, w_in_bf, o_stack, w_branch_bf, layer, tm, tn):
    n, d = x_bf.shape
    gcol0 = MIX_COLS // tn
    per = d // tn
    return pl.pallas_call(
        _gate_merge_kernel,
        grid=(n // tm, d // tn, N_BRANCH),
        in_specs=[pl.BlockSpec((tm, d), lambda m, j, i: (m, 0)),
                  pl.BlockSpec((None, d, tn), lambda m, j, i: (layer, 0, gcol0 + i * per + j)),
                  pl.BlockSpec((None, tm, BRANCH_W), lambda m, j, i: (i, m, 0)),
                  pl.BlockSpec((None, None, BRANCH_W, tn), lambda m, j, i: (layer, i, 0, j))],
        out_specs=pl.BlockSpec((tm, tn), lambda m, j, i: (m, j)),
        out_shape=jax.ShapeDtypeStruct((n, d), BF16),
        scratch_shapes=[pltpu.VMEM((tm, tn), F32)],
        compiler_params=_params(("parallel", "parallel", "arbitrary")),
        name="gate_merge",
    )(x_bf, w_in_bf, o_stack, w_branch_bf)


def _layer_norm(y, g, b):
    mu = jnp.mean(y, axis=-1, keepdims=True)
    d = y - mu
    var = jnp.mean(d * d, axis=-1, keepdims=True)
    return d * lax.rsqrt(var + LN_EPS) * g + b


def _out_ln_kernel(m_ref, w_ref, x_ref, g_ref, b_ref, of_ref, ob_ref, *, alpha):
    k = pl.program_id(1)

    @pl.when(k == 0)
    def _():
        of_ref[...] = jnp.zeros(of_ref.shape, F32)

    of_ref[...] += _dot(m_ref[...], w_ref[...])

    @pl.when(k == pl.num_programs(1) - 1)
    def _():
        rows = min(of_ref.shape[0], LN_ROWS)

        def body(r, c):
            sl = pl.ds(pl.multiple_of(r * rows, rows), rows)
            out = _layer_norm(alpha * x_ref[sl, :] + of_ref[sl, :], g_ref[...], b_ref[...])
            of_ref[sl, :] = out
            ob_ref[sl, :] = out.astype(ob_ref.dtype)
            return c

        lax.fori_loop(0, of_ref.shape[0] // rows, body, 0)


LN_ROWS = 128


def out_proj_ln(merged, w_out_bf, x, g, b, layer, alpha, tm, tk):
    n, d = x.shape
    return pl.pallas_call(
        functools.partial(_out_ln_kernel, alpha=alpha),
        grid=(n // tm, d // tk),
        in_specs=[pl.BlockSpec((tm, tk), lambda m, k: (m, k)),
                  pl.BlockSpec((None, tk, d), lambda m, k: (layer, k, 0)),
                  pl.BlockSpec((tm, d), lambda m, k: (m, 0), pipeline_mode=pl.Buffered(1)),
                  pl.BlockSpec((None, 1, d), lambda m, k: (layer, 0, 0)),
                  pl.BlockSpec((None, 1, d), lambda m, k: (layer, 0, 0))],
        out_specs=[pl.BlockSpec((tm, d), lambda m, k: (m, 0)), pl.BlockSpec((tm, d), lambda m, k: (m, 0))],
        out_shape=[jax.ShapeDtypeStruct((n, d), F32), jax.ShapeDtypeStruct((n, d), BF16)],
        compiler_params=_params(("parallel", "arbitrary")),
        name="out_proj_ln",
    )(merged, w_out_bf, x, g, b)


def _router_kernel(x_ref, whi_ref, wlo_ref, b_ref, o_ref):
    x = x_ref[...]
    x_hi = x.astype(BF16)
    x_lo = (x - x_hi.astype(F32)).astype(BF16)
    whi = whi_ref[...]
    logits = _dot(x_hi, whi) + _dot(x_lo, whi) + _dot(x_hi, wlo_ref[...]) + b_ref[...]
    lane = lax.broadcasted_iota(jnp.int32, logits.shape, 1)
    big = jnp.int32(LANES)
    gmask = lane < N_GROUPS
    gl = jnp.where(gmask, logits, NEG_BIG)
    gmax = jnp.max(gl, axis=-1, keepdims=True)
    g_top = jnp.min(jnp.where(gmask & (gl == gmax), lane, big), axis=-1, keepdims=True)
    g_w = 1.0 / jnp.sum(jnp.where(gmask, jnp.exp(gl - gmax), 0.0), axis=-1, keepdims=True)
    lo = N_GROUPS + g_top * EXPERTS_PER_GROUP
    emask = (lane >= lo) & (lane < lo + EXPERTS_PER_GROUP)
    e1 = jnp.where(emask, logits, NEG_BIG)
    v1 = jnp.max(e1, axis=-1, keepdims=True)
    i1 = jnp.min(jnp.where(emask & (e1 == v1), lane, big), axis=-1, keepdims=True)
    emask2 = emask & (lane != i1)
    e2 = jnp.where(emask2, logits, NEG_BIG)
    v2 = jnp.max(e2, axis=-1, keepdims=True)
    i2 = jnp.min(jnp.where(emask2 & (e2 == v2), lane, big), axis=-1, keepdims=True)
    t = jnp.exp(v2 - v1)
    w1 = g_w / (1.0 + t)
    w2 = g_w * t / (1.0 + t)
    o_ref[...] = (jnp.where(lane == 0, (i1 - N_GROUPS).astype(F32), 0.0)
                  + jnp.where(lane == 1, (i2 - N_GROUPS).astype(F32), 0.0)
                  + jnp.where(lane == 2, w1, 0.0) + jnp.where(lane == 3, w2, 0.0))


def router(x, w_hi, w_lo, bias, tm):
    n, d = x.shape
    return pl.pallas_call(
        _router_kernel,
        grid=(n // tm,),
        in_specs=[pl.BlockSpec((tm, d), lambda m: (m, 0)),
                  pl.BlockSpec((d, LANES), lambda m: (0, 0)),
                  pl.BlockSpec((d, LANES), lambda m: (0, 0)),
                  pl.BlockSpec((1, LANES), lambda m: (0, 0))],
        out_specs=pl.BlockSpec((tm, LANES), lambda m: (m, 0)),
        out_shape=jax.ShapeDtypeStruct((n, LANES), F32),
        compiler_params=_params(("parallel",)),
        name="router",
    )(x, w_hi, w_lo, bias)


MOE_TM = 256


def moe_routing(route, n_tok, tm):
    n_pairs = 2 * n_tok
    e = route[:, 0:2].astype(jnp.int32).reshape(n_pairs)
    w = route[:, 2:4].reshape(n_pairs)
    _, pair_s, w_s = lax.sort((e, jnp.arange(n_pairs, dtype=jnp.int32), w), num_keys=1, is_stable=True)
    counts = jnp.sum((e[:, None] == jnp.arange(N_EXPERTS)[None, :]).astype(jnp.int32), axis=0)
    tiles_per = (counts + tm - 1) // tm
    tile_end = jnp.cumsum(tiles_per)
    tile_start = tile_end - tiles_per
    n_tiles = tile_end[-1]
    first = jnp.cumsum(counts) - counts
    t_max = n_pairs // tm + N_EXPERTS
    t_idx = jnp.arange(t_max)
    used = t_idx < n_tiles
    tile_expert = jnp.minimum(jnp.searchsorted(tile_end, t_idx, side="right"), N_EXPERTS - 1).astype(jnp.int32)
    base = (t_idx - tile_start[tile_expert]) * tm
    before = jnp.where(used, first[tile_expert] + base, n_pairs)
    nvalid = jnp.where(used, jnp.clip(counts[tile_expert] - base, 0, tm), 0)
    window = lambda a: jax.vmap(lambda s: lax.dynamic_slice(jnp.pad(a, (0, tm)), (s,), (tm,)))(before)
    pair, wpair = window(pair_s), window(w_s)
    r = jnp.arange(tm)[None, :]
    valid = r < nvalid[:, None]
    tok = jnp.where(valid, pair // 2, 0)
    spare = (t_idx * tm - before)[:, None] + (r - nvalid[:, None])
    dst = jnp.where(valid, (pair % 2) * n_tok + pair // 2, n_pairs + spare)
    wslot = jnp.where(valid, wpair, 0.0).reshape(t_max * tm)
    return (tile_expert, n_tiles.astype(jnp.int32)[None], tok.astype(jnp.int32).reshape(t_max * tm),
            dst.astype(jnp.int32).reshape(t_max * tm), jnp.broadcast_to(wslot[:, None], (t_max * tm, LANES)))


def _moe_kernel(te_ref, nt_ref, tok_ref, dst_ref, x_hbm, w_ref, wgu_ref, wd_ref, out_hbm,
                xbuf, ybuf, gsem, ssem):
    t = pl.program_id(0)
    nt = nt_ref[0]
    tm = xbuf.shape[1]
    f = wd_ref.shape[0]

    def gather_row(tile, r, slot):
        tok = tok_ref[tile * tm + r]
        return pltpu.make_async_copy(x_hbm.at[pl.ds(tok, 1), :], xbuf.at[slot, pl.ds(r, 1), :], gsem.at[slot])

    def scatter_row(tile, r, slot):
        d = dst_ref[tile * tm + r]
        return pltpu.make_async_copy(ybuf.at[slot, pl.ds(r, 1), :], out_hbm.at[pl.ds(d, 1), :], ssem.at[slot])

    def for_rows(fn):
        def body(r, c):
            fn(r)
            return c
        lax.fori_loop(0, tm, body, 0, unroll=8)

    @pl.when((t == 0) & (nt > 0))
    def _():
        for_rows(lambda r: gather_row(0, r, 0).start())

    @pl.when(t + 1 < nt)
    def _():
        for_rows(lambda r: gather_row(t + 1, r, (t + 1) % 2).start())

    @pl.when(t < nt)
    def _():
        slot = t % 2
        for_rows(lambda r: gather_row(t, r, slot).wait())
        xb = xbuf[slot].astype(BF16)
        gu = _dot(xb, wgu_ref[...])
        gate = gu[:, :f]
        w = w_ref[...]
        hmid = (gate / (1.0 + jnp.exp(-gate))) * gu[:, f:] * jnp.concatenate([w] * (f // LANES), axis=1)
        y = _dot(hmid.astype(BF16), wd_ref[...])

        @pl.when(t >= 2)
        def _():
            for_rows(lambda r: scatter_row(t - 2, r, slot).wait())

        ybuf[slot] = y
        for_rows(lambda r: scatter_row(t, r, slot).start())

    @pl.when(t == pl.num_programs(0) - 1)
    def _():
        for back in (1, 2):
            @pl.when(nt >= back)
            def _():
                for_rows(lambda r: scatter_row(nt - back, r, (nt - back) % 2).wait())


def moe_experts(x, routing, wgu_bf, wd_bf, layer, tm):
    n, d = x.shape
    f = wd_bf.shape[-2]
    tile_expert, n_tiles, tok, dst, wslot = routing
    t_max = tile_expert.shape[0]
    grid_spec = pltpu.PrefetchScalarGridSpec(
        num_scalar_prefetch=4,
        grid=(t_max,),
        in_specs=[pl.BlockSpec(memory_space=pl.ANY),
                  pl.BlockSpec((tm, LANES), lambda t, te, nt, tk, ds: (t, 0)),
                  pl.BlockSpec((None, None, d, 2 * f), lambda t, te, nt, tk, ds: (layer, te[t], 0, 0)),
                  pl.BlockSpec((None, None, f, d), lambda t, te, nt, tk, ds: (layer, te[t], 0, 0))],
        out_specs=pl.BlockSpec(memory_space=pl.ANY),
        scratch_shapes=[pltpu.VMEM((2, tm, d), F32), pltpu.VMEM((2, tm, d), F32),
                        pltpu.SemaphoreType.DMA((2,)), pltpu.SemaphoreType.DMA((2,))])
    return pl.pallas_call(
        _moe_kernel,
        grid_spec=grid_spec,
        out_shape=jax.ShapeDtypeStruct((t_max * tm, d), F32),
        compiler_params=_params(("arbitrary",)),
        name="moe_experts",
    )(tile_expert, n_tiles, tok, dst, x, wslot, wgu_bf, wd_bf)


def _pair_ln_kernel(y0_ref, y1_ref, x_ref, g_ref, b_ref, of_ref, ob_ref, *, alpha):
    out = _layer_norm(alpha * x_ref[...] + (y0_ref[...] + y1_ref[...]), g_ref[...], b_ref[...])
    of_ref[...] = out
    ob_ref[...] = out.astype(ob_ref.dtype)


def pair_sum_ln(y_pairs, x, g, b, layer, alpha, tm):
    n, d = x.shape
    nblk = n // tm
    return pl.pallas_call(
        functools.partial(_pair_ln_kernel, alpha=alpha),
        grid=(nblk,),
        in_specs=[pl.BlockSpec((tm, d), lambda m: (m, 0)),
                  pl.BlockSpec((tm, d), lambda m: (nblk + m, 0)),
                  pl.BlockSpec((tm, d), lambda m: (m, 0)),
                  pl.BlockSpec((None, 1, d), lambda m: (layer, 0, 0)),
                  pl.BlockSpec((None, 1, d), lambda m: (layer, 0, 0))],
        out_specs=[pl.BlockSpec((tm, d), lambda m: (m, 0)), pl.BlockSpec((tm, d), lambda m: (m, 0))],
        out_shape=[jax.ShapeDtypeStruct((n, d), F32), jax.ShapeDtypeStruct((n, d), BF16)],
        compiler_params=_params(("parallel",)),
        name="pair_sum_ln",
    )(y_pairs, y_pairs, x, g, b)


def _rope_tables(seq):
    t = jnp.arange(seq)
    row_id = (t // GRID_W).astype(F32)
    col_id = (t % GRID_W).astype(F32)
    nfreq = HEAD_DIM // 4
    inv = ROPE_THETA ** (-jnp.arange(nfreq, dtype=F32) / nfreq)
    ar = row_id[:, None] * inv
    ac = col_id[:, None] * inv
    cos = jnp.concatenate([jnp.cos(ar), jnp.cos(ar), jnp.cos(ac), jnp.cos(ac)], axis=-1)
    sin = jnp.concatenate([-jnp.sin(ar), jnp.sin(ar), -jnp.sin(ac), jnp.sin(ac)], axis=-1)
    return cos, sin


def _tile(n, pref):
    t = min(n, pref)
    while n % t:
        t //= 2
    return t


def kernel(x, w_in, attn_sink, qk_norm_q, qk_norm_k, ret_decay_fwd, ret_decay_bwd, ret_norm_g,
           diff_lambda_q1, diff_lambda_k1, diff_lambda_q2, diff_lambda_k2, diff_subln_g,
           w_branch, w_out, ln_mix_g, ln_mix_b, router_group, router_group_b, router_expert,
           router_expert_b, expert_w_gate, expert_w_up, expert_w_down, ln_ffn_g, ln_ffn_b):
    batch, seq, d = x.shape
    depth = w_in.shape[0]
    n = batch * seq
    alpha = (2.0 * depth) ** 0.25
    assert w_in.shape[2] == MIX_COLS + N_BRANCH * d and seq % BLOCK == 0 and seq % GRID_W == 0

    colscale = jnp.ones((w_in.shape[2],), F32)
    colscale = colscale.at[COL_AQ:COL_AQ + BRANCH_W].set(HEAD_DIM ** -0.5 * LOG2E)
    colscale = colscale.at[COL_DQ:COL_DQ + BRANCH_W].set(D_HALF ** -0.5 * LOG2E)
    w_in_bf = (w_in * colscale).astype(BF16)
    w_branch_bf = w_branch.astype(BF16)
    w_out_bf = w_out.astype(BF16)
    wgu_bf = jnp.concatenate([expert_w_gate.astype(BF16), expert_w_up.astype(BF16)], axis=-1)
    wd_bf = expert_w_down.astype(BF16)
    ln_mix_g3, ln_mix_b3 = ln_mix_g[:, None, :], ln_mix_b[:, None, :]
    ln_ffn_g3, ln_ffn_b3 = ln_ffn_g[:, None, :], ln_ffn_b[:, None, :]

    cos_t, sin_t = _rope_tables(seq)
    slopes_log2 = jnp.asarray([LOG2E * 2.0 ** (-(i + 1.0)) for i in range(N_HEADS)], F32)

    tm_big = _tile(n, 1024)
    xf = x.reshape(n, d)
    xb = xf.astype(BF16)
    for l in range(depth):
        h = in_proj(xb, w_in_bf, l, MIX_COLS, tm_big, 1024)
        o_a = attn_a(h, attn_sink[l].astype(F32) * LOG2E, batch, seq)
        gains = jnp.concatenate([jnp.tile(qk_norm_q[l], N_HEADS) * (HEAD_DIM ** -0.5 * LOG2E),
                                 jnp.tile(qk_norm_k[l], KV_HEADS)])[None, :]
        qt_b, k_b = qk_prep(h, gains, cos_t, sin_t, batch, seq, _tile(seq, 512))
        vt_b = h[:, COL_BV:COL_BV + KV_HEADS * HEAD_DIM].T
        o_b = attn_b(qt_b, k_b, vt_b, batch, seq, _tile(seq, 256), _tile(seq // 2, 512))
        tabs_f = retention_tables(ret_decay_fwd[l], HEAD_DIM ** -0.5, False)
        tabs_b = retention_tables(ret_decay_bwd[l], HEAD_DIM ** -0.5, True)
        o_c = retention(h, tabs_f, tabs_b, ret_norm_g[l][None, :], batch, seq)
        lam_init = 0.8 - 0.6 * math.exp(-0.3 * l)
        lam_params = jnp.stack([diff_lambda_q1[l], diff_lambda_k1[l], diff_lambda_q2[l], diff_lambda_k2[l]]).astype(F32)
        qt_d = h[:, COL_DQ:COL_DQ + BRANCH_W].T
        vt_d = h[:, COL_DV:COL_DV + BRANCH_W].T
        o_d = attn_d(qt_d, h, vt_d, slopes_log2, lam_params, diff_subln_g[l][:, None], lam_init, batch, seq,
                     _tile(seq, 512), _tile(seq // 2, 512))
        merged = gate_merge(xb, w_in_bf, jnp.stack([o_a, o_b, o_c, o_d]), w_branch_bf, l, tm_big, 512)
        xf, xb = out_proj_ln(merged, w_out_bf, xf, ln_mix_g3, ln_mix_b3, l, alpha, _tile(n, 512), _tile(d, 512))
        w_r = jnp.concatenate([router_group[l], router_expert[l]], axis=1)
        w_r = jnp.pad(w_r, ((0, 0), (0, LANES - w_r.shape[1])))
        w_r_hi = w_r.astype(BF16)
        w_r_lo = (w_r - w_r_hi.astype(F32)).astype(BF16)
        b_r = jnp.pad(jnp.concatenate([router_group_b[l], router_expert_b[l]]), (0, LANES - N_GROUPS - N_EXPERTS))[None, :]
        route = router(xf, w_r_hi, w_r_lo, b_r, _tile(n, 512))
        y_pairs = moe_experts(xf, moe_routing(route, n, MOE_TM), wgu_bf, wd_bf, l, MOE_TM)
        xf, xb = pair_sum_ln(y_pairs, xf, ln_ffn_g3, ln_ffn_b3, l, alpha, _tile(n, 256))
    return xf.reshape(batch, seq, d)
```

```python
import functools
import math

import jax
import jax.numpy as jnp
from jax import lax
from jax.experimental import pallas as pl
from jax.experimental.pallas import tpu as pltpu

F32 = jnp.float32
BF16 = jnp.bfloat16

HEAD_DIM = 128
BLOCK = 128
GRID_W = 64
N_HEADS = 8
KV_HEADS = 2
D_HALF = HEAD_DIM // 2
N_BRANCH = 4
BRANCH_W = N_HEADS * HEAD_DIM
ROPE_THETA = 10000.0
N_GROUPS = 4
EXPERTS_PER_GROUP = 4
N_EXPERTS = N_GROUPS * EXPERTS_PER_GROUP
LN_EPS = 1e-5
RMS_EPS = 1e-6
NEG_BIG = -1e30
LOG2E = 1.4426950408889634
LANES = 128
VMEM_LIMIT = 56 * 1024 * 1024

COL_AQ, COL_AK, COL_AV = 0, 1024, 1280
COL_BQ, COL_BK, COL_BV = 1536, 2560, 2816
COL_CQ, COL_CK, COL_CV, COL_CG = 3072, 4096, 5120, 6144
COL_DQ, COL_DK, COL_DV = 7168, 8192, 9216
MIX_COLS = 10240


def _params(sem):
    return pltpu.CompilerParams(dimension_semantics=sem, vmem_limit_bytes=VMEM_LIMIT)


def _nt_dot(a, b):
    return lax.dot_general(a, b, (((1,), (1,)), ((), ())), preferred_element_type=F32)


def _tn_dot(a, b):
    return lax.dot_general(a, b, (((0,), (0,)), ((), ())), preferred_element_type=F32)


def _dot(a, b):
    return jnp.dot(a, b, preferred_element_type=F32)


def _mm_kernel(x_ref, w_ref, o_ref):
    o_ref[...] = _dot(x_ref[...], w_ref[...]).astype(o_ref.dtype)


def in_proj(x_bf, w_in_bf, layer, n_cols, tm, tn):
    n, k = x_bf.shape
    return pl.pallas_call(
        _mm_kernel,
        grid=(n // tm, n_cols // tn),
        in_specs=[pl.BlockSpec((tm, k), lambda i, j: (i, 0)),
                  pl.BlockSpec((None, k, tn), lambda i, j: (layer, 0, j))],
        out_specs=pl.BlockSpec((tm, tn), lambda i, j: (i, j)),
        out_shape=jax.ShapeDtypeStruct((n, n_cols), BF16),
        compiler_params=_params(("parallel", "parallel")),
        name="in_proj",
    )(x_bf, w_in_bf)


def _attn_a_kernel(sink_ref, q_ref, kp_ref, kc_ref, kn_ref, vp_ref, vc_ref, vn_ref, o_ref, *, slopes, nb):
    i = pl.program_id(1)
    rep = N_HEADS // KV_HEADS
    rows = rep * BLOCK
    row = lax.broadcasted_iota(jnp.int32, (rows, 3 * BLOCK), 0)
    col = lax.broadcasted_iota(jnp.int32, (rows, 3 * BLOCK), 1)
    rel = (row % BLOCK) - (col - BLOCK)
    dist = jnp.abs(rel)
    valid = (dist <= BLOCK) & ((col >= BLOCK) | (i > 0)) & ((col < 2 * BLOCK) | (i < nb - 1))
    distf = dist.astype(F32)
    rowc = lax.broadcasted_iota(jnp.int32, (rows, 1), 0)
    for g in range(KV_HEADS):
        q = jnp.concatenate([q_ref[:, (g * rep + r) * HEAD_DIM:(g * rep + r + 1) * HEAD_DIM]
                             for r in range(rep)], axis=0)
        ks = slice(g * HEAD_DIM, (g + 1) * HEAD_DIM)
        k = jnp.concatenate([kp_ref[:, ks], kc_ref[:, ks], kn_ref[:, ks]], axis=0)
        v = jnp.concatenate([vp_ref[:, ks], vc_ref[:, ks], vn_ref[:, ks]], axis=0)
        slope = jnp.full((rows, 1), slopes[g * rep], F32)
        sink = jnp.full((rows, 1), sink_ref[g * rep], F32)
        for r in range(1, rep):
            slope = jnp.where(rowc >= r * BLOCK, slopes[g * rep + r], slope)
            sink = jnp.where(rowc >= r * BLOCK, sink_ref[g * rep + r], sink)
        s = _nt_dot(q, k) - slope * distf
        s = jnp.where(valid, s, NEG_BIG)
        m = jnp.maximum(jnp.max(s, axis=-1, keepdims=True), sink)
        p = jnp.exp2(s - m)
        denom = jnp.sum(p, axis=-1, keepdims=True) + jnp.exp2(sink - m)
        o = _dot(p.astype(BF16), v) / denom
        for r in range(rep):
            hh = g * rep + r
            o_ref[:, hh * HEAD_DIM:(hh + 1) * HEAD_DIM] = o[r * BLOCK:(r + 1) * BLOCK].astype(o_ref.dtype)


def attn_a(h, sink_log2, batch, seq):
    n = h.shape[0]
    nb = seq // BLOCK
    slopes = tuple(LOG2E * 2.0 ** (-(i + 1.0)) for i in range(N_HEADS))
    kvw = KV_HEADS * HEAD_DIM
    kcol, vcol = COL_AK // kvw, COL_AV // kvw

    def prev(b, i):
        return b * nb + jnp.maximum(i - 1, 0)

    def nxt(b, i):
        return b * nb + jnp.minimum(i + 1, nb - 1)

    return pl.pallas_call(
        functools.partial(_attn_a_kernel, slopes=slopes, nb=nb),
        grid=(batch, nb),
        in_specs=[pl.BlockSpec(memory_space=pltpu.SMEM),
                  pl.BlockSpec((BLOCK, BRANCH_W), lambda b, i: (b * nb + i, COL_AQ // BRANCH_W)),
                  pl.BlockSpec((BLOCK, kvw), lambda b, i: (prev(b, i), kcol)),
                  pl.BlockSpec((BLOCK, kvw), lambda b, i: (b * nb + i, kcol)),
                  pl.BlockSpec((BLOCK, kvw), lambda b, i: (nxt(b, i), kcol)),
                  pl.BlockSpec((BLOCK, kvw), lambda b, i: (prev(b, i), vcol)),
                  pl.BlockSpec((BLOCK, kvw), lambda b, i: (b * nb + i, vcol)),
                  pl.BlockSpec((BLOCK, kvw), lambda b, i: (nxt(b, i), vcol))],
        out_specs=pl.BlockSpec((None, BLOCK, BRANCH_W), lambda b, i: (0, b * nb + i, 0)),
        out_shape=jax.ShapeDtypeStruct((N_BRANCH, n, BRANCH_W), BF16),
        compiler_params=_params(("parallel", "parallel")),
        name="attn_a",
    )(sink_log2, h, h, h, h, h, h, h)


def _norm_rope(x, g, cos, sin, first_quarter):
    y = x * lax.rsqrt(jnp.mean(x * x, axis=-1, keepdims=True) + RMS_EPS) * g
    partner = jnp.where(first_quarter, pltpu.roll(y, HEAD_DIM - HEAD_DIM // 4, 1), pltpu.roll(y, HEAD_DIM // 4, 1))
    return y * cos + partner * sin


def _qk_prep_kernel(q0_ref, q1_ref, k_ref, g_ref, cos_ref, sin_ref, qt_ref, ko_ref):
    lane = lax.broadcasted_iota(jnp.int32, (k_ref.shape[0], HEAD_DIM), 1)
    first_quarter = (lane % (HEAD_DIM // 2)) < (HEAD_DIM // 4)
    cos = cos_ref[...]
    sin = sin_ref[...]
    half = N_HEADS // 2
    for hh in range(N_HEADS):
        src = q0_ref if hh < half else q1_ref
        sl = slice((hh % half) * HEAD_DIM, (hh % half + 1) * HEAD_DIM)
        gs = slice(hh * HEAD_DIM, (hh + 1) * HEAD_DIM)
        y = _norm_rope(src[:, sl].astype(F32), g_ref[:, gs], cos, sin, first_quarter)
        qt_ref[gs, :] = y.T.astype(qt_ref.dtype)
    for hh in range(KV_HEADS):
        sl = slice(hh * HEAD_DIM, (hh + 1) * HEAD_DIM)
        gs = slice((N_HEADS + hh) * HEAD_DIM, (N_HEADS + hh + 1) * HEAD_DIM)
        ko_ref[:, sl] = _norm_rope(k_ref[:, sl].astype(F32), g_ref[:, gs], cos, sin, first_quarter).astype(ko_ref.dtype)


def qk_prep(h, gains, cos_t, sin_t, batch, seq, tm):
    n = h.shape[0]
    qw = BRANCH_W // 2
    kw = KV_HEADS * HEAD_DIM
    nblk = seq // tm
    return pl.pallas_call(
        _qk_prep_kernel,
        grid=(batch, nblk),
        in_specs=[pl.BlockSpec((tm, qw), lambda b, i: (b * nblk + i, COL_BQ // qw)),
                  pl.BlockSpec((tm, qw), lambda b, i: (b * nblk + i, COL_BQ // qw + 1)),
                  pl.BlockSpec((tm, kw), lambda b, i: (b * nblk + i, COL_BK // kw)),
                  pl.BlockSpec((1, BRANCH_W + kw), lambda b, i: (0, 0)),
                  pl.BlockSpec((tm, HEAD_DIM), lambda b, i: (i, 0)),
                  pl.BlockSpec((tm, HEAD_DIM), lambda b, i: (i, 0))],
        out_specs=[pl.BlockSpec((BRANCH_W, tm), lambda b, i: (0, b * nblk + i)),
                   pl.BlockSpec((tm, kw), lambda b, i: (b * nblk + i, 0))],
        out_shape=[jax.ShapeDtypeStruct((BRANCH_W, n), BF16), jax.ShapeDtypeStruct((n, kw), BF16)],
        compiler_params=_params(("parallel", "parallel")),
        name="qk_prep",
    )(h, h, h, gains, cos_t, sin_t)


ONES_ROWS = 16


def _flash_pipeline(n_chunks, score_fn, vt_fn, bufs, m_ref, acc_ref):
    per_trip = FLASH_CHUNKS_PER_TRIP if n_chunks % FLASH_CHUNKS_PER_TRIP == 0 else 2
    assert n_chunks % per_trip == 0
    scores, accumulate = _flash_stages(m_ref, acc_ref)
    scores(score_fn(0), bufs[0])

    def body(i, carry):
        c = per_trip * i
        for j in range(per_trip):
            scores(score_fn(jnp.minimum(c + j + 1, n_chunks - 1)), bufs[(j + 1) % 2])
            accumulate(vt_fn(c + j), bufs[j % 2])
        return carry

    lax.fori_loop(0, n_chunks // per_trip, body, 0)


FLASH_CHUNKS_PER_TRIP = 4


def _flash_init(m_ref, acc_ref):
    m_ref[...] = jnp.full(m_ref.shape, NEG_BIG, F32)
    acc_ref[...] = jnp.zeros(acc_ref.shape, F32)


def _flash_stages(m_ref, acc_ref):
    def scores(s_and_offset, buf):
        s, offset = s_and_offset
        s_ref, x_ref, c_ref = buf
        s_ref[...] = s
        x_ref[...] = jnp.max(s, axis=0, keepdims=True) + offset
        c_ref[...] = jnp.full(c_ref.shape, offset, F32)

    def accumulate(vt, buf):
        s_ref, x_ref, c_ref = buf
        m_old = m_ref[...]
        m_new = jnp.maximum(m_old, x_ref[...])
        a = jnp.exp2(m_old - m_new)
        p = jnp.exp2(s_ref[...] - (m_new - c_ref[...])).astype(BF16)
        vt1 = jnp.concatenate([vt, jnp.ones((ONES_ROWS, vt.shape[1]), vt.dtype)], axis=0)
        acc_ref[...] = a * acc_ref[...] + _dot(vt1, p)
        m_ref[...] = m_new

    return scores, accumulate


def _attn_b_kernel(qt_ref, k_ref, vt_ref, stack_ref, o_ref, m_ref, acc_ref, *bufs, tk):
    rep = qt_ref.shape[0] // HEAD_DIM
    tq = qt_ref.shape[1]
    seq = k_ref.shape[0]
    qt = jnp.concatenate([qt_ref[r * HEAD_DIM:(r + 1) * HEAD_DIM, :] for r in range(rep)], axis=1)

    def score_fn(c):
        return _dot(k_ref[pl.ds(pl.multiple_of(c * tk, tk), tk), :], qt), 0.0

    def vt_fn(c):
        return vt_ref[:, pl.ds(pl.multiple_of(c * tk, tk), tk)]

    _flash_init(m_ref, acc_ref)
    _flash_pipeline(seq // tk, score_fn, vt_fn, (bufs[:3], bufs[3:]), m_ref, acc_ref)
    o = acc_ref[:HEAD_DIM, :] / acc_ref[HEAD_DIM:HEAD_DIM + 1, :]
    for r in range(rep):
        o_ref[:, r * HEAD_DIM:(r + 1) * HEAD_DIM] = o[:, r * tq:(r + 1) * tq].T.astype(o_ref.dtype)


def _flash_scratch(tk, width):
    row = pltpu.VMEM((1, width), F32)
    return [row, pltpu.VMEM((HEAD_DIM + ONES_ROWS, width), F32),
            pltpu.VMEM((tk, width), F32), row, row,
            pltpu.VMEM((tk, width), F32), row, row]


def attn_b(qt, kb, vt, o_stack, batch, seq, tq, tk):
    n = kb.shape[0]
    rep = N_HEADS // KV_HEADS
    nq = seq // tq
    width = rep * tq
    return pl.pallas_call(
        functools.partial(_attn_b_kernel, tk=tk),
        grid=(batch, KV_HEADS, nq),
        in_specs=[pl.BlockSpec((rep * HEAD_DIM, tq), lambda b, g, i: (g, b * nq + i)),
                  pl.BlockSpec((seq, HEAD_DIM), lambda b, g, i: (b, g)),
                  pl.BlockSpec((HEAD_DIM, seq), lambda b, g, i: (g, b)),
                  pl.BlockSpec(memory_space=pl.ANY)],
        out_specs=pl.BlockSpec((None, tq, rep * HEAD_DIM), lambda b, g, i: (1, b * nq + i, g)),
        out_shape=jax.ShapeDtypeStruct(o_stack.shape, o_stack.dtype),
        input_output_aliases={3: 0},
        scratch_shapes=_flash_scratch(tk, width),
        compiler_params=_params(("parallel", "parallel", "parallel")),
        name="attn_b",
    )(qt, kb, vt, o_stack)


def _retention_kernel(*refs, final):
    if final:
        (q_ref, k_ref, v_ref, inner_ref, qdec_ref, kdec_ref, cdec_ref, prev_ref, gate_ref, ng_ref, stack_ref,
         o_ref, state_ref) = refs
    else:
        q_ref, k_ref, v_ref, inner_ref, qdec_ref, kdec_ref, cdec_ref, o_ref, state_ref = refs

    @pl.when(pl.program_id(1) == 0)
    def _():
        state_ref[...] = jnp.zeros(state_ref.shape, F32)

    for hh in range(N_HEADS):
        sl = slice(hh * HEAD_DIM, (hh + 1) * HEAD_DIM)
        q = q_ref[:, sl]
        k = k_ref[:, sl]
        v = v_ref[:, sl]
        state = state_ref[hh]
        inner = _nt_dot(q, k) * inner_ref[hh]
        o = _dot(inner.astype(BF16), v) + _dot(q, state.astype(BF16)) * qdec_ref[hh]
        kd = (k.astype(F32) * kdec_ref[hh]).astype(BF16)
        state_ref[hh] = state * cdec_ref[hh] + _tn_dot(kd, v)
        if final:
            r = o + prev_ref[:, sl]
            mu = jnp.mean(r, axis=-1, keepdims=True)
            d = r - mu
            var = jnp.mean(d * d, axis=-1, keepdims=True)
            rn = d * lax.rsqrt(var + LN_EPS) * ng_ref[:, sl]
            gate = gate_ref[:, sl].astype(F32)
            silu = gate / (1.0 + jnp.exp(-gate))
            o_ref[:, sl] = (silu * rn).astype(o_ref.dtype)
        else:
            o_ref[:, sl] = o


def retention(h, tabs_f, tabs_b, norm_g, o_stack, batch, seq):
    n = h.shape[0]
    nb = seq // BLOCK
    w = BRANCH_W
    tab_specs = [pl.BlockSpec((N_HEADS, BLOCK, BLOCK), lambda b, c: (0, 0, 0)),
                 pl.BlockSpec((N_HEADS, BLOCK, HEAD_DIM), lambda b, c: (0, 0, 0)),
                 pl.BlockSpec((N_HEADS, BLOCK, HEAD_DIM), lambda b, c: (0, 0, 0)),
                 pl.BlockSpec((N_HEADS, 1, HEAD_DIM), lambda b, c: (0, 0, 0))]

    def specs(rowmap):
        return [pl.BlockSpec((BLOCK, w), lambda b, c: (rowmap(b, c), COL_CQ // w)),
                pl.BlockSpec((BLOCK, w), lambda b, c: (rowmap(b, c), COL_CK // w)),
                pl.BlockSpec((BLOCK, w), lambda b, c: (rowmap(b, c), COL_CV // w))]

    fmap = lambda b, c: b * nb + c
    bmap = lambda b, c: b * nb + (nb - 1 - c)
    scratch = [pltpu.VMEM((N_HEADS, HEAD_DIM, HEAD_DIM), F32)]
    o_f = pl.pallas_call(
        functools.partial(_retention_kernel, final=False),
        grid=(batch, nb),
        in_specs=specs(fmap) + tab_specs,
        out_specs=pl.BlockSpec((BLOCK, w), lambda b, c: (fmap(b, c), 0)),
        out_shape=jax.ShapeDtypeStruct((n, w), F32),
        scratch_shapes=scratch,
        compiler_params=_params(("parallel", "arbitrary")),
        name="retention_fwd",
    )(h, h, h, *tabs_f)
    return pl.pallas_call(
        functools.partial(_retention_kernel, final=True),
        grid=(batch, nb),
        in_specs=specs(bmap) + tab_specs + [
            pl.BlockSpec((BLOCK, w), lambda b, c: (bmap(b, c), 0)),
            pl.BlockSpec((BLOCK, w), lambda b, c: (bmap(b, c), COL_CG // w)),
            pl.BlockSpec((1, w), lambda b, c: (0, 0)),
            pl.BlockSpec(memory_space=pl.ANY)],
        out_specs=pl.BlockSpec((None, BLOCK, w), lambda b, c: (2, bmap(b, c), 0)),
        out_shape=jax.ShapeDtypeStruct(o_stack.shape, o_stack.dtype),
        input_output_aliases={10: 0},
        scratch_shapes=scratch,
        compiler_params=_params(("parallel", "arbitrary")),
        name="retention_bwd",
    )(h, h, h, *tabs_b, o_f, h, norm_g, o_stack)


def retention_tables(dec, scale, backward):
    lg = jnp.log1p(-jnp.exp(dec.astype(F32)))
    pos = jnp.arange(BLOCK, dtype=F32)
    diff = pos[:, None] - pos[None, :]
    if backward:
        diff = -diff
        mask = diff > 0
        qexp = BLOCK - pos
        kexp = pos
    else:
        mask = diff >= 0
        qexp = pos + 1.0
        kexp = BLOCK - 1.0 - pos
    inner = jnp.where(mask[None], jnp.exp(lg[:, None, None] * jnp.where(mask, diff, 0.0)[None]), 0.0) * scale
    qdec = jnp.broadcast_to(jnp.exp(lg[:, None] * qexp)[:, :, None], (N_HEADS, BLOCK, HEAD_DIM))
    kdec = jnp.broadcast_to((jnp.exp(lg[:, None] * kexp) * scale)[:, :, None], (N_HEADS, BLOCK, HEAD_DIM))
    cdec = jnp.broadcast_to(jnp.exp(lg * BLOCK)[:, None, None], (N_HEADS, 1, HEAD_DIM))
    return inner, qdec, kdec, cdec


def _attn_d_kernel(slope_ref, lam_ref, g_ref, q_ref, k_ref, vt_ref, stack_ref, o_ref, m_ref, acc_ref, *bufs,
                   tk, lam_init):
    tq = q_ref.shape[1]
    seq = k_ref.shape[0]
    slope = slope_ref[pl.program_id(1)]
    q0 = pl.program_id(2) * tq
    feat = lax.broadcasted_iota(jnp.int32, (HEAD_DIM, tq), 0)
    q = q_ref[...]
    zero = jnp.zeros_like(q)
    qt = jnp.concatenate([jnp.where(feat < D_HALF, q, zero), jnp.where(feat >= D_HALF, q, zero)], axis=1)
    rel = (lax.broadcasted_iota(jnp.int32, (tk, tq), 0) - lax.broadcasted_iota(jnp.int32, (tk, tq), 1)).astype(F32)

    def score_fn(c):
        off = pl.multiple_of(c * tk, tk)
        bias = slope * jnp.abs(rel + (off - q0).astype(F32))
        return _dot(k_ref[pl.ds(off, tk), :], qt) - jnp.concatenate([bias, bias], axis=1), 0.0

    def vt_fn(c):
        return vt_ref[:, pl.ds(pl.multiple_of(c * tk, tk), tk)]

    _flash_init(m_ref, acc_ref)
    _flash_pipeline(seq // tk, score_fn, vt_fn, (bufs[:3], bufs[3:]), m_ref, acc_ref)
    lp = lam_ref[...]
    lam = (jnp.exp(jnp.sum(lp[0:1] * lp[1:2], axis=-1, keepdims=True))
           - jnp.exp(jnp.sum(lp[2:3] * lp[3:4], axis=-1, keepdims=True)) + lam_init)
    on = acc_ref[:HEAD_DIM, :] / acc_ref[HEAD_DIM:HEAD_DIM + 1, :]
    o = on[:, :tq] - lam * on[:, tq:]
    y = o * lax.rsqrt(jnp.mean(o * o, axis=0, keepdims=True) + RMS_EPS) * g_ref[...]
    o_ref[...] = (y * (1.0 - lam_init)).T.astype(o_ref.dtype)


def attn_d(qt, h, vt, o_stack, slopes_log2, lam_params, subln_g, lam_init, batch, seq, tq, tk):
    n = h.shape[0]
    nq = seq // tq
    return pl.pallas_call(
        functools.partial(_attn_d_kernel, tk=tk, lam_init=lam_init),
        grid=(batch, N_HEADS, nq),
        in_specs=[pl.BlockSpec(memory_space=pltpu.SMEM),
                  pl.BlockSpec((4, D_HALF), lambda b, hh, i: (0, 0)),
                  pl.BlockSpec((HEAD_DIM, 1), lambda b, hh, i: (0, 0)),
                  pl.BlockSpec((HEAD_DIM, tq), lambda b, hh, i: (hh, b * nq + i)),
                  pl.BlockSpec((seq, HEAD_DIM), lambda b, hh, i: (b, COL_DK // HEAD_DIM + hh)),
                  pl.BlockSpec((HEAD_DIM, seq), lambda b, hh, i: (hh, b)),
                  pl.BlockSpec(memory_space=pl.ANY)],
        out_specs=pl.BlockSpec((None, tq, HEAD_DIM), lambda b, hh, i: (3, b * nq + i, hh)),
        out_shape=jax.ShapeDtypeStruct(o_stack.shape, o_stack.dtype),
        input_output_aliases={6: 0},
        scratch_shapes=_flash_scratch(tk, 2 * tq),
        compiler_params=_params(("parallel", "parallel", "parallel")),
        name="attn_d",
    )(slopes_log2, lam_params, subln_g, qt, h, vt, o_stack)


def _gate_merge_kernel(x_ref, *refs):
    wg_refs, o_refs, wb_refs = refs[:N_BRANCH], refs[N_BRANCH:2 * N_BRANCH], refs[2 * N_BRANCH:3 * N_BRANCH]
    out_ref = refs[3 * N_BRANCH]
    x = x_ref[...]
    acc = None
    for i in range(N_BRANCH):
        gate = _dot(x, wg_refs[i][...])
        contrib = _dot(o_refs[i][...], wb_refs[i][...]) / (1.0 + jnp.exp(-gate))
        acc = contrib if acc is None else acc + contrib
    out_ref[...] = acc.astype(out_ref.dtype)


def gate_merge(x_bf, w_in_bf, o_stack, w_branch_bf, layer, tm, tn):
    n, d = x_bf.shape
    gcol0 = MIX_COLS // tn
    per = d // tn
    once = dict(pipeline_mode=pl.Buffered(1))
    wg_specs = [pl.BlockSpec((None, d, tn), functools.partial(lambda m, j, i: (layer, 0, gcol0 + i * per + j), i=i))
                for i in range(N_BRANCH)]
    o_specs = [pl.BlockSpec((None, tm, BRANCH_W), functools.partial(lambda m, j, i: (i, m, 0), i=i), **once)
               for i in range(N_BRANCH)]
    wb_specs = [pl.BlockSpec((None, None, BRANCH_W, tn), functools.partial(lambda m, j, i: (layer, i, 0, j), i=i))
                for i in range(N_BRANCH)]
    return pl.pallas_call(
        _gate_merge_kernel,
        grid=(n // tm, d // tn),
        in_specs=[pl.BlockSpec((tm, d), lambda m, j: (m, 0), **once)] + wg_specs + o_specs + wb_specs,
        out_specs=pl.BlockSpec((tm, tn), lambda m, j: (m, j)),
        out_shape=jax.ShapeDtypeStruct((n, d), BF16),
        compiler_params=_params(("parallel", "parallel")),
        name="gate_merge",
    )(x_bf, *([w_in_bf] * N_BRANCH), *([o_stack] * N_BRANCH), *([w_branch_bf] * N_BRANCH))


def _layer_norm(y, g, b):
    mu = jnp.mean(y, axis=-1, keepdims=True)
    d = y - mu
    var = jnp.mean(d * d, axis=-1, keepdims=True)
    return d * lax.rsqrt(var + LN_EPS) * g + b


def _out_ln_kernel(m_ref, w_ref, x_ref, g_ref, b_ref, of_ref, ob_ref, *, alpha):
    k = pl.program_id(1)

    @pl.when(k == 0)
    def _():
        of_ref[...] = jnp.zeros(of_ref.shape, F32)

    of_ref[...] += _dot(m_ref[...], w_ref[...])

    @pl.when(k == pl.num_programs(1) - 1)
    def _():
        rows = min(of_ref.shape[0], LN_ROWS)

        def body(r, c):
            sl = pl.ds(pl.multiple_of(r * rows, rows), rows)
            out = _layer_norm(alpha * x_ref[sl, :] + of_ref[sl, :], g_ref[...], b_ref[...])
            of_ref[sl, :] = out
            ob_ref[sl, :] = out.astype(ob_ref.dtype)
            return c

        lax.fori_loop(0, of_ref.shape[0] // rows, body, 0)


LN_ROWS = 128


def out_proj_ln(merged, w_out_bf, x, g, b, layer, alpha, tm, tk):
    n, d = x.shape
    return pl.pallas_call(
        functools.partial(_out_ln_kernel, alpha=alpha),
        grid=(n // tm, d // tk),
        in_specs=[pl.BlockSpec((tm, tk), lambda m, k: (m, k)),
                  pl.BlockSpec((None, tk, d), lambda m, k: (layer, k, 0)),
                  pl.BlockSpec((tm, d), lambda m, k: (m, 0), pipeline_mode=pl.Buffered(1)),
                  pl.BlockSpec((None, 1, d), lambda m, k: (layer, 0, 0)),
                  pl.BlockSpec((None, 1, d), lambda m, k: (layer, 0, 0))],
        out_specs=[pl.BlockSpec((tm, d), lambda m, k: (m, 0)), pl.BlockSpec((tm, d), lambda m, k: (m, 0))],
        out_shape=[jax.ShapeDtypeStruct((n, d), F32), jax.ShapeDtypeStruct((n, d), BF16)],
        compiler_params=_params(("parallel", "arbitrary")),
        name="out_proj_ln",
    )(merged, w_out_bf, x, g, b)


def _router_kernel(x_ref, whi_ref, wlo_ref, b_ref, o_ref):
    x = x_ref[...]
    x_hi = x.astype(BF16)
    x_lo = (x - x_hi.astype(F32)).astype(BF16)
    whi = whi_ref[...]
    logits = _dot(x_hi, whi) + _dot(x_lo, whi) + _dot(x_hi, wlo_ref[...]) + b_ref[...]
    lane = lax.broadcasted_iota(jnp.int32, logits.shape, 1)
    big = jnp.int32(LANES)
    gmask = lane < N_GROUPS
    gl = jnp.where(gmask, logits, NEG_BIG)
    gmax = jnp.max(gl, axis=-1, keepdims=True)
    g_top = jnp.min(jnp.where(gmask & (gl == gmax), lane, big), axis=-1, keepdims=True)
    g_w = 1.0 / jnp.sum(jnp.where(gmask, jnp.exp(gl - gmax), 0.0), axis=-1, keepdims=True)
    lo = N_GROUPS + g_top * EXPERTS_PER_GROUP
    emask = (lane >= lo) & (lane < lo + EXPERTS_PER_GROUP)
    e1 = jnp.where(emask, logits, NEG_BIG)
    v1 = jnp.max(e1, axis=-1, keepdims=True)
    i1 = jnp.min(jnp.where(emask & (e1 == v1), lane, big), axis=-1, keepdims=True)
    emask2 = emask & (lane != i1)
    e2 = jnp.where(emask2, logits, NEG_BIG)
    v2 = jnp.max(e2, axis=-1, keepdims=True)
    i2 = jnp.min(jnp.where(emask2 & (e2 == v2), lane, big), axis=-1, keepdims=True)
    t = jnp.exp(v2 - v1)
    w1 = g_w / (1.0 + t)
    w2 = g_w * t / (1.0 + t)
    o_ref[...] = (jnp.where(lane == 0, (i1 - N_GROUPS).astype(F32), 0.0)
                  + jnp.where(lane == 1, (i2 - N_GROUPS).astype(F32), 0.0)
                  + jnp.where(lane == 2, w1, 0.0) + jnp.where(lane == 3, w2, 0.0))


def router(x, w_hi, w_lo, bias, tm):
    n, d = x.shape
    return pl.pallas_call(
        _router_kernel,
        grid=(n // tm,),
        in_specs=[pl.BlockSpec((tm, d), lambda m: (m, 0)),
                  pl.BlockSpec((d, LANES), lambda m: (0, 0)),
                  pl.BlockSpec((d, LANES), lambda m: (0, 0)),
                  pl.BlockSpec((1, LANES), lambda m: (0, 0))],
        out_specs=pl.BlockSpec((tm, LANES), lambda m: (m, 0)),
        out_shape=jax.ShapeDtypeStruct((n, LANES), F32),
        compiler_params=_params(("parallel",)),
        name="router",
    )(x, w_hi, w_lo, bias)


MOE_TM = 256


def moe_routing(route, n_tok, tm):
    n_pairs = 2 * n_tok
    e = route[:, 0:2].astype(jnp.int32).reshape(n_pairs)
    w = route[:, 2:4].reshape(n_pairs)
    counts = jnp.sum((e[:, None] == jnp.arange(N_EXPERTS)[None, :]).astype(jnp.int32), axis=0)
    tiles_per = (counts + tm - 1) // tm
    tile_end = jnp.cumsum(tiles_per)
    n_tiles = tile_end[-1]
    t_max = n_pairs // tm + N_EXPERTS
    tile_expert = jnp.minimum(jnp.searchsorted(tile_end, jnp.arange(t_max), side="right"), N_EXPERTS - 1).astype(jnp.int32)
    pad_exp = jnp.repeat(jnp.arange(N_EXPERTS, dtype=jnp.int32), tm)
    pad_rank = jnp.tile(jnp.arange(tm, dtype=jnp.int32), N_EXPERTS)
    pad_key = jnp.where(pad_rank < (tiles_per * tm - counts)[pad_exp], 2 * pad_exp + 1, 2 * N_EXPERTS)
    keys = jnp.concatenate([2 * e, pad_key])
    ident = jnp.concatenate([jnp.arange(n_pairs, dtype=jnp.int32),
                             n_pairs + jnp.arange(N_EXPERTS * tm, dtype=jnp.int32)])
    _, ident, wslot = lax.sort((keys, ident, jnp.concatenate([w, jnp.zeros((N_EXPERTS * tm,), w.dtype)])),
                               num_keys=1, is_stable=True)
    valid = ident < n_pairs
    tok = jnp.where(valid, ident // 2, 0)
    dst = jnp.where(valid, (ident % 2) * n_tok + ident // 2, ident)
    return (tile_expert, n_tiles.astype(jnp.int32)[None], tok.astype(jnp.int32), dst.astype(jnp.int32),
            jnp.broadcast_to(wslot[:, None], (t_max * tm, LANES)))


def _moe_kernel(te_ref, nt_ref, tok_ref, dst_ref, x_hbm, w_ref, wgu_ref, wd_ref, out_hbm,
                xbuf, ybuf, gsem, ssem):
    t = pl.program_id(0)
    nt = nt_ref[0]
    tm = xbuf.shape[1]
    f = wd_ref.shape[0]

    def gather_row(tile, r, slot):
        tok = tok_ref[tile * tm + r]
        return pltpu.make_async_copy(x_hbm.at[pl.ds(tok, 1), :], xbuf.at[slot, pl.ds(r, 1), :], gsem.at[slot])

    def scatter_row(tile, r, slot):
        d = dst_ref[tile * tm + r]
        return pltpu.make_async_copy(ybuf.at[slot, pl.ds(r, 1), :], out_hbm.at[pl.ds(d, 1), :], ssem.at[slot])

    def for_rows(fn):
        def body(r, c):
            fn(r)
            return c
        lax.fori_loop(0, tm, body, 0, unroll=8)

    @pl.when((t == 0) & (nt > 0))
    def _():
        for_rows(lambda r: gather_row(0, r, 0).start())

    @pl.when(t + 1 < nt)
    def _():
        for_rows(lambda r: gather_row(t + 1, r, (t + 1) % 2).start())

    @pl.when(t < nt)
    def _():
        slot = t % 2
        for_rows(lambda r: gather_row(t, r, slot).wait())
        xb = xbuf[slot].astype(BF16)
        gu = _dot(xb, wgu_ref[...])
        gate = gu[:, :f]
        w = w_ref[...]
        hmid = (gate / (1.0 + jnp.exp(-gate))) * gu[:, f:] * jnp.concatenate([w] * (f // LANES), axis=1)
        y = _dot(hmid.astype(BF16), wd_ref[...])

        @pl.when(t >= 2)
        def _():
            for_rows(lambda r: scatter_row(t - 2, r, slot).wait())

        ybuf[slot] = y
        for_rows(lambda r: scatter_row(t, r, slot).start())

    @pl.when(t == pl.num_programs(0) - 1)
    def _():
        for back in (1, 2):
            @pl.when(nt >= back)
            def _():
                for_rows(lambda r: scatter_row(nt - back, r, (nt - back) % 2).wait())


def moe_experts(x, routing, wgu_bf, wd_bf, layer, tm):
    n, d = x.shape
    f = wd_bf.shape[-2]
    tile_expert, n_tiles, tok, dst, wslot = routing
    t_max = tile_expert.shape[0]
    grid_spec = pltpu.PrefetchScalarGridSpec(
        num_scalar_prefetch=4,
        grid=(t_max,),
        in_specs=[pl.BlockSpec(memory_space=pl.ANY),
                  pl.BlockSpec((tm, LANES), lambda t, te, nt, tk, ds: (t, 0)),
                  pl.BlockSpec((None, None, d, 2 * f), lambda t, te, nt, tk, ds: (layer, te[t], 0, 0)),
                  pl.BlockSpec((None, None, f, d), lambda t, te, nt, tk, ds: (layer, te[t], 0, 0))],
        out_specs=pl.BlockSpec(memory_space=pl.ANY),
        scratch_shapes=[pltpu.VMEM((2, tm, d), F32), pltpu.VMEM((2, tm, d), F32),
                        pltpu.SemaphoreType.DMA((2,)), pltpu.SemaphoreType.DMA((2,))])
    return pl.pallas_call(
        _moe_kernel,
        grid_spec=grid_spec,
        out_shape=jax.ShapeDtypeStruct((t_max * tm, d), F32),
        compiler_params=_params(("arbitrary",)),
        name="moe_experts",
    )(tile_expert, n_tiles, tok, dst, x, wslot, wgu_bf, wd_bf)


def _pair_ln_kernel(y0_ref, y1_ref, x_ref, g_ref, b_ref, of_ref, ob_ref, *, alpha):
    out = _layer_norm(alpha * x_ref[...] + (y0_ref[...] + y1_ref[...]), g_ref[...], b_ref[...])
    of_ref[...] = out
    ob_ref[...] = out.astype(ob_ref.dtype)


def pair_sum_ln(y_pairs, x, g, b, layer, alpha, tm):
    n, d = x.shape
    nblk = n // tm
    return pl.pallas_call(
        functools.partial(_pair_ln_kernel, alpha=alpha),
        grid=(nblk,),
        in_specs=[pl.BlockSpec((tm, d), lambda m: (m, 0)),
                  pl.BlockSpec((tm, d), lambda m: (nblk + m, 0)),
                  pl.BlockSpec((tm, d), lambda m: (m, 0)),
                  pl.BlockSpec((None, 1, d), lambda m: (layer, 0, 0)),
                  pl.BlockSpec((None, 1, d), lambda m: (layer, 0, 0))],
        out_specs=[pl.BlockSpec((tm, d), lambda m: (m, 0)), pl.BlockSpec((tm, d), lambda m: (m, 0))],
        out_shape=[jax.ShapeDtypeStruct((n, d), F32), jax.ShapeDtypeStruct((n, d), BF16)],
        compiler_params=_params(("parallel",)),
        name="pair_sum_ln",
    )(y_pairs, y_pairs, x, g, b)


def _rope_tables(seq):
    t = jnp.arange(seq)
    row_id = (t // GRID_W).astype(F32)
    col_id = (t % GRID_W).astype(F32)
    nfreq = HEAD_DIM // 4
    inv = ROPE_THETA ** (-jnp.arange(nfreq, dtype=F32) / nfreq)
    ar = row_id[:, None] * inv
    ac = col_id[:, None] * inv
    cos = jnp.concatenate([jnp.cos(ar), jnp.cos(ar), jnp.cos(ac), jnp.cos(ac)], axis=-1)
    sin = jnp.concatenate([-jnp.sin(ar), jnp.sin(ar), -jnp.sin(ac), jnp.sin(ac)], axis=-1)
    return cos, sin


def _tile(n, pref):
    t = min(n, pref)
    while n % t:
        t //= 2
    return t


def kernel(x, w_in, attn_sink, qk_norm_q, qk_norm_k, ret_decay_fwd, ret_decay_bwd, ret_norm_g,
           diff_lambda_q1, diff_lambda_k1, diff_lambda_q2, diff_lambda_k2, diff_subln_g,
           w_branch, w_out, ln_mix_g, ln_mix_b, router_group, router_group_b, router_expert,
           router_expert_b, expert_w_gate, expert_w_up, expert_w_down, ln_ffn_g, ln_ffn_b):
    batch, seq, d = x.shape
    depth = w_in.shape[0]
    n = batch * seq
    alpha = (2.0 * depth) ** 0.25
    assert w_in.shape[2] == MIX_COLS + N_BRANCH * d and seq % BLOCK == 0 and seq % GRID_W == 0

    colscale = jnp.ones((w_in.shape[2],), F32)
    colscale = colscale.at[COL_AQ:COL_AQ + BRANCH_W].set(HEAD_DIM ** -0.5 * LOG2E)
    colscale = colscale.at[COL_DQ:COL_DQ + BRANCH_W].set(D_HALF ** -0.5 * LOG2E)
    w_in_bf = (w_in * colscale).astype(BF16)
    w_branch_bf = w_branch.astype(BF16)
    w_out_bf = w_out.astype(BF16)
    wgu_bf = jnp.concatenate([expert_w_gate.astype(BF16), expert_w_up.astype(BF16)], axis=-1)
    wd_bf = expert_w_down.astype(BF16)
    ln_mix_g3, ln_mix_b3 = ln_mix_g[:, None, :], ln_mix_b[:, None, :]
    ln_ffn_g3, ln_ffn_b3 = ln_ffn_g[:, None, :], ln_ffn_b[:, None, :]

    cos_t, sin_t = _rope_tables(seq)
    slopes_log2 = jnp.asarray([LOG2E * 2.0 ** (-(i + 1.0)) for i in range(N_HEADS)], F32)

    tm_big = _tile(n, 1024)
    xf = x.reshape(n, d)
    xb = xf.astype(BF16)
    for l in range(depth):
        h = in_proj(xb, w_in_bf, l, MIX_COLS, tm_big, 1024)
        o_all = attn_a(h, attn_sink[l].astype(F32) * LOG2E, batch, seq)
        gains = jnp.concatenate([jnp.tile(qk_norm_q[l], N_HEADS) * (HEAD_DIM ** -0.5 * LOG2E),
                                 jnp.tile(qk_norm_k[l], KV_HEADS)])[None, :]
        qt_b, k_b = qk_prep(h, gains, cos_t, sin_t, batch, seq, _tile(seq, 512))
        vt_b = h[:, COL_BV:COL_BV + KV_HEADS * HEAD_DIM].T
        o_all = attn_b(qt_b, k_b, vt_b, o_all, batch, seq, _tile(seq, 256), _tile(seq // 2, 512))
        tabs_f = retention_tables(ret_decay_fwd[l], HEAD_DIM ** -0.5, False)
        tabs_b = retention_tables(ret_decay_bwd[l], HEAD_DIM ** -0.5, True)
        o_all = retention(h, tabs_f, tabs_b, ret_norm_g[l][None, :], o_all, batch, seq)
        lam_init = 0.8 - 0.6 * math.exp(-0.3 * l)
        lam_params = jnp.stack([diff_lambda_q1[l], diff_lambda_k1[l], diff_lambda_q2[l], diff_lambda_k2[l]]).astype(F32)
        qt_d = h[:, COL_DQ:COL_DQ + BRANCH_W].T
        vt_d = h[:, COL_DV:COL_DV + BRANCH_W].T
        o_all = attn_d(qt_d, h, vt_d, o_all, slopes_log2, lam_params, diff_subln_g[l][:, None], lam_init, batch, seq,
                       _tile(seq, 512), _tile(seq // 2, 512))
        merged = gate_merge(xb, w_in_bf, o_all, w_branch_bf, l, tm_big, 256)
        xf, xb = out_proj_ln(merged, w_out_bf, xf, ln_mix_g3, ln_mix_b3, l, alpha, _tile(n, 512), _tile(d, 512))
        w_r = jnp.concatenate([router_group[l], router_expert[l]], axis=1)
        w_r = jnp.pad(w_r, ((0, 0), (0, LANES - w_r.shape[1])))
        w_r_hi = w_r.astype(BF16)
        w_r_lo = (w_r - w_r_hi.astype(F32)).astype(BF16)
        b_r = jnp.pad(jnp.concatenate([router_group_b[l], router_expert_b[l]]), (0, LANES - N_GROUPS - N_EXPERTS))[None, :]
        route = router(xf, w_r_hi, w_r_lo, b_r, _tile(n, 512))
        y_pairs = moe_experts(xf, moe_routing(route, n, MOE_TM), wgu_bf, wd_bf, l, MOE_TM)
        xf, xb = pair_sum_ln(y_pairs, xf, ln_ffn_g3, ln_ffn_b3, l, alpha, _tile(n, 256))
    return xf.reshape(batch, seq, d)
```

```python
import functools
import math

import jax
import jax.numpy as jnp
from jax import lax
from jax.experimental import pallas as pl
from jax.experimental.pallas import tpu as pltpu

F32 = jnp.float32
BF16 = jnp.bfloat16

HEAD_DIM = 128
BLOCK = 128
GRID_W = 64
N_HEADS = 8
KV_HEADS = 2
D_HALF = HEAD_DIM // 2
N_BRANCH = 4
BRANCH_W = N_HEADS * HEAD_DIM
ROPE_THETA = 10000.0
N_GROUPS = 4
EXPERTS_PER_GROUP = 4
N_EXPERTS = N_GROUPS * EXPERTS_PER_GROUP
LN_EPS = 1e-5
RMS_EPS = 1e-6
NEG_BIG = -1e30
LOG2E = 1.4426950408889634
LANES = 128
VMEM_LIMIT = 56 * 1024 * 1024

COL_AQ, COL_AK, COL_AV = 0, 1024, 1280
COL_BQ, COL_BK, COL_BV = 1536, 2560, 2816
COL_CQ, COL_CK, COL_CV, COL_CG = 3072, 4096, 5120, 6144
COL_DQ, COL_DK, COL_DV = 7168, 8192, 9216
MIX_COLS = 10240


def _params(sem):
    return pltpu.CompilerParams(dimension_semantics=sem, vmem_limit_bytes=VMEM_LIMIT)


def _nt_dot(a, b):
    return lax.dot_general(a, b, (((1,), (1,)), ((), ())), preferred_element_type=F32)


def _tn_dot(a, b):
    return lax.dot_general(a, b, (((0,), (0,)), ((), ())), preferred_element_type=F32)


def _dot(a, b):
    return jnp.dot(a, b, preferred_element_type=F32)


def _mm_kernel(x_ref, w_ref, o_ref):
    o_ref[...] = _dot(x_ref[...], w_ref[...]).astype(o_ref.dtype)


def in_proj(x_bf, w_in_bf, layer, n_cols, tm, tn):
    n, k = x_bf.shape
    return pl.pallas_call(
        _mm_kernel,
        grid=(n // tm, n_cols // tn),
        in_specs=[pl.BlockSpec((tm, k), lambda i, j: (i, 0)),
                  pl.BlockSpec((None, k, tn), lambda i, j: (layer, 0, j))],
        out_specs=pl.BlockSpec((tm, tn), lambda i, j: (i, j)),
        out_shape=jax.ShapeDtypeStruct((n, n_cols), BF16),
        compiler_params=_params(("parallel", "parallel")),
        name="in_proj",
    )(x_bf, w_in_bf)


def _attn_a_kernel(sink_ref, q_ref, kp_ref, kc_ref, kn_ref, vp_ref, vc_ref, vn_ref, o_ref, *, slopes, nb):
    i = pl.program_id(1)
    rep = N_HEADS // KV_HEADS
    rows = rep * BLOCK
    row = lax.broadcasted_iota(jnp.int32, (rows, 3 * BLOCK), 0)
    col = lax.broadcasted_iota(jnp.int32, (rows, 3 * BLOCK), 1)
    rel = (row % BLOCK) - (col - BLOCK)
    dist = jnp.abs(rel)
    valid = (dist <= BLOCK) & ((col >= BLOCK) | (i > 0)) & ((col < 2 * BLOCK) | (i < nb - 1))
    distf = dist.astype(F32)
    rowc = lax.broadcasted_iota(jnp.int32, (rows, 1), 0)
    for g in range(KV_HEADS):
        q = jnp.concatenate([q_ref[:, (g * rep + r) * HEAD_DIM:(g * rep + r + 1) * HEAD_DIM]
                             for r in range(rep)], axis=0)
        ks = slice(g * HEAD_DIM, (g + 1) * HEAD_DIM)
        k = jnp.concatenate([kp_ref[:, ks], kc_ref[:, ks], kn_ref[:, ks]], axis=0)
        v = jnp.concatenate([vp_ref[:, ks], vc_ref[:, ks], vn_ref[:, ks]], axis=0)
        slope = jnp.full((rows, 1), slopes[g * rep], F32)
        sink = jnp.full((rows, 1), sink_ref[g * rep], F32)
        for r in range(1, rep):
            slope = jnp.where(rowc >= r * BLOCK, slopes[g * rep + r], slope)
            sink = jnp.where(rowc >= r * BLOCK, sink_ref[g * rep + r], sink)
        s = _nt_dot(q, k) - slope * distf
        s = jnp.where(valid, s, NEG_BIG)
        m = jnp.maximum(jnp.max(s, axis=-1, keepdims=True), sink)
        p = jnp.exp2(s - m)
        denom = jnp.sum(p, axis=-1, keepdims=True) + jnp.exp2(sink - m)
        o = _dot(p.astype(BF16), v) / denom
        for r in range(rep):
            hh = g * rep + r
            o_ref[:, hh * HEAD_DIM:(hh + 1) * HEAD_DIM] = o[r * BLOCK:(r + 1) * BLOCK].astype(o_ref.dtype)


def attn_a(h, sink_log2, batch, seq):
    n = h.shape[0]
    nb = seq // BLOCK
    slopes = tuple(LOG2E * 2.0 ** (-(i + 1.0)) for i in range(N_HEADS))
    kvw = KV_HEADS * HEAD_DIM
    kcol, vcol = COL_AK // kvw, COL_AV // kvw

    def prev(b, i):
        return b * nb + jnp.maximum(i - 1, 0)

    def nxt(b, i):
        return b * nb + jnp.minimum(i + 1, nb - 1)

    return pl.pallas_call(
        functools.partial(_attn_a_kernel, slopes=slopes, nb=nb),
        grid=(batch, nb),
        in_specs=[pl.BlockSpec(memory_space=pltpu.SMEM),
                  pl.BlockSpec((BLOCK, BRANCH_W), lambda b, i: (b * nb + i, COL_AQ // BRANCH_W)),
                  pl.BlockSpec((BLOCK, kvw), lambda b, i: (prev(b, i), kcol)),
                  pl.BlockSpec((BLOCK, kvw), lambda b, i: (b * nb + i, kcol)),
                  pl.BlockSpec((BLOCK, kvw), lambda b, i: (nxt(b, i), kcol)),
                  pl.BlockSpec((BLOCK, kvw), lambda b, i: (prev(b, i), vcol)),
                  pl.BlockSpec((BLOCK, kvw), lambda b, i: (b * nb + i, vcol)),
                  pl.BlockSpec((BLOCK, kvw), lambda b, i: (nxt(b, i), vcol))],
        out_specs=pl.BlockSpec((None, BLOCK, BRANCH_W), lambda b, i: (0, b * nb + i, 0)),
        out_shape=jax.ShapeDtypeStruct((N_BRANCH, n, BRANCH_W), BF16),
        compiler_params=_params(("parallel", "parallel")),
        name="attn_a",
    )(sink_log2, h, h, h, h, h, h, h)


def _norm_rope(x, g, cos, sin, first_quarter):
    y = x * lax.rsqrt(jnp.mean(x * x, axis=-1, keepdims=True) + RMS_EPS) * g
    partner = jnp.where(first_quarter, pltpu.roll(y, HEAD_DIM - HEAD_DIM // 4, 1), pltpu.roll(y, HEAD_DIM // 4, 1))
    return y * cos + partner * sin


def _qk_prep_kernel(q0_ref, q1_ref, k_ref, g_ref, cos_ref, sin_ref, qt_ref, ko_ref):
    lane = lax.broadcasted_iota(jnp.int32, (k_ref.shape[0], HEAD_DIM), 1)
    first_quarter = (lane % (HEAD_DIM // 2)) < (HEAD_DIM // 4)
    cos = cos_ref[...]
    sin = sin_ref[...]
    half = N_HEADS // 2
    for hh in range(N_HEADS):
        src = q0_ref if hh < half else q1_ref
        sl = slice((hh % half) * HEAD_DIM, (hh % half + 1) * HEAD_DIM)
        gs = slice(hh * HEAD_DIM, (hh + 1) * HEAD_DIM)
        y = _norm_rope(src[:, sl].astype(F32), g_ref[:, gs], cos, sin, first_quarter)
        qt_ref[gs, :] = y.T.astype(qt_ref.dtype)
    for hh in range(KV_HEADS):
        sl = slice(hh * HEAD_DIM, (hh + 1) * HEAD_DIM)
        gs = slice((N_HEADS + hh) * HEAD_DIM, (N_HEADS + hh + 1) * HEAD_DIM)
        ko_ref[:, sl] = _norm_rope(k_ref[:, sl].astype(F32), g_ref[:, gs], cos, sin, first_quarter).astype(ko_ref.dtype)


def qk_prep(h, gains, cos_t, sin_t, batch, seq, tm):
    n = h.shape[0]
    qw = BRANCH_W // 2
    kw = KV_HEADS * HEAD_DIM
    nblk = seq // tm
    return pl.pallas_call(
        _qk_prep_kernel,
        grid=(batch, nblk),
        in_specs=[pl.BlockSpec((tm, qw), lambda b, i: (b * nblk + i, COL_BQ // qw)),
                  pl.BlockSpec((tm, qw), lambda b, i: (b * nblk + i, COL_BQ // qw + 1)),
                  pl.BlockSpec((tm, kw), lambda b, i: (b * nblk + i, COL_BK // kw)),
                  pl.BlockSpec((1, BRANCH_W + kw), lambda b, i: (0, 0)),
                  pl.BlockSpec((tm, HEAD_DIM), lambda b, i: (i, 0)),
                  pl.BlockSpec((tm, HEAD_DIM), lambda b, i: (i, 0))],
        out_specs=[pl.BlockSpec((BRANCH_W, tm), lambda b, i: (0, b * nblk + i)),
                   pl.BlockSpec((tm, kw), lambda b, i: (b * nblk + i, 0))],
        out_shape=[jax.ShapeDtypeStruct((BRANCH_W, n), BF16), jax.ShapeDtypeStruct((n, kw), BF16)],
        compiler_params=_params(("parallel", "parallel")),
        name="qk_prep",
    )(h, h, h, gains, cos_t, sin_t)


ONES_ROWS = 16


def _flash_pipeline(n_chunks, score_fn, vt_fn, bufs, m_ref, acc_ref):
    per_trip = FLASH_CHUNKS_PER_TRIP if n_chunks % FLASH_CHUNKS_PER_TRIP == 0 else 2
    assert n_chunks % per_trip == 0
    scores, accumulate = _flash_stages(m_ref, acc_ref)
    scores(score_fn(0), bufs[0])

    def body(i, carry):
        c = per_trip * i
        for j in range(per_trip):
            scores(score_fn(jnp.minimum(c + j + 1, n_chunks - 1)), bufs[(j + 1) % 2])
            accumulate(vt_fn(c + j), bufs[j % 2])
        return carry

    lax.fori_loop(0, n_chunks // per_trip, body, 0)


FLASH_CHUNKS_PER_TRIP = 4


def _flash_init(m_ref, acc_ref):
    m_ref[...] = jnp.full(m_ref.shape, NEG_BIG, F32)
    acc_ref[...] = jnp.zeros(acc_ref.shape, F32)


def _flash_stages(m_ref, acc_ref):
    def scores(s_and_offset, buf):
        s, offset = s_and_offset
        s_ref, x_ref, c_ref = buf
        s_ref[...] = s
        x_ref[...] = jnp.max(s, axis=0, keepdims=True) + offset
        c_ref[...] = jnp.full(c_ref.shape, offset, F32)

    def accumulate(vt, buf):
        s_ref, x_ref, c_ref = buf
        m_old = m_ref[...]
        m_new = jnp.maximum(m_old, x_ref[...])
        a = jnp.exp2(m_old - m_new)
        p = jnp.exp2(s_ref[...] - (m_new - c_ref[...])).astype(BF16)
        vt1 = jnp.concatenate([vt, jnp.ones((ONES_ROWS, vt.shape[1]), vt.dtype)], axis=0)
        acc_ref[...] = a * acc_ref[...] + _dot(vt1, p)
        m_ref[...] = m_new

    return scores, accumulate


def _attn_b_kernel(qt_ref, k_ref, vt_ref, stack_ref, o_ref, m_ref, acc_ref, *bufs, tk):
    rep = qt_ref.shape[0] // HEAD_DIM
    tq = qt_ref.shape[1]
    seq = k_ref.shape[0]
    qt = jnp.concatenate([qt_ref[r * HEAD_DIM:(r + 1) * HEAD_DIM, :] for r in range(rep)], axis=1)

    def score_fn(c):
        return _dot(k_ref[pl.ds(pl.multiple_of(c * tk, tk), tk), :], qt), 0.0

    def vt_fn(c):
        return vt_ref[:, pl.ds(pl.multiple_of(c * tk, tk), tk)]

    _flash_init(m_ref, acc_ref)
    _flash_pipeline(seq // tk, score_fn, vt_fn, (bufs[:3], bufs[3:]), m_ref, acc_ref)
    o = acc_ref[:HEAD_DIM, :] / acc_ref[HEAD_DIM:HEAD_DIM + 1, :]
    for r in range(rep):
        o_ref[:, r * HEAD_DIM:(r + 1) * HEAD_DIM] = o[:, r * tq:(r + 1) * tq].T.astype(o_ref.dtype)


def _flash_scratch(tk, width):
    row = pltpu.VMEM((1, width), F32)
    return [row, pltpu.VMEM((HEAD_DIM + ONES_ROWS, width), F32),
            pltpu.VMEM((tk, width), F32), row, row,
            pltpu.VMEM((tk, width), F32), row, row]


def attn_b(qt, kb, vt, o_stack, batch, seq, tq, tk):
    n = kb.shape[0]
    rep = N_HEADS // KV_HEADS
    nq = seq // tq
    width = rep * tq
    return pl.pallas_call(
        functools.partial(_attn_b_kernel, tk=tk),
        grid=(batch, KV_HEADS, nq),
        in_specs=[pl.BlockSpec((rep * HEAD_DIM, tq), lambda b, g, i: (g, b * nq + i)),
                  pl.BlockSpec((seq, HEAD_DIM), lambda b, g, i: (b, g)),
                  pl.BlockSpec((HEAD_DIM, seq), lambda b, g, i: (g, b)),
                  pl.BlockSpec(memory_space=pl.ANY)],
        out_specs=pl.BlockSpec((None, tq, rep * HEAD_DIM), lambda b, g, i: (1, b * nq + i, g)),
        out_shape=jax.ShapeDtypeStruct(o_stack.shape, o_stack.dtype),
        input_output_aliases={3: 0},
        scratch_shapes=_flash_scratch(tk, width),
        compiler_params=_params(("parallel", "parallel", "parallel")),
        name="attn_b",
    )(qt, kb, vt, o_stack)


def _retention_kernel(*refs, final):
    if final:
        (q_ref, k_ref, v_ref, inner_ref, qdec_ref, kdec_ref, cdec_ref, prev_ref, gate_ref, ng_ref, stack_ref,
         o_ref, state_ref) = refs
    else:
        q_ref, k_ref, v_ref, inner_ref, qdec_ref, kdec_ref, cdec_ref, o_ref, state_ref = refs

    @pl.when(pl.program_id(1) == 0)
    def _():
        state_ref[...] = jnp.zeros(state_ref.shape, F32)

    for hh in range(N_HEADS):
        sl = slice(hh * HEAD_DIM, (hh + 1) * HEAD_DIM)
        q = q_ref[:, sl]
        k = k_ref[:, sl]
        v = v_ref[:, sl]
        state = state_ref[hh]
        inner = _nt_dot(q, k) * inner_ref[hh]
        o = _dot(inner.astype(BF16), v) + _dot(q, state.astype(BF16)) * qdec_ref[hh]
        kd = (k.astype(F32) * kdec_ref[hh]).astype(BF16)
        state_ref[hh] = state * cdec_ref[hh] + _tn_dot(kd, v)
        if final:
            r = o + prev_ref[:, sl]
            mu = jnp.mean(r, axis=-1, keepdims=True)
            d = r - mu
            var = jnp.mean(d * d, axis=-1, keepdims=True)
            rn = d * lax.rsqrt(var + LN_EPS) * ng_ref[:, sl]
            gate = gate_ref[:, sl].astype(F32)
            silu = gate / (1.0 + jnp.exp(-gate))
            o_ref[:, sl] = (silu * rn).astype(o_ref.dtype)
        else:
            o_ref[:, sl] = o


def retention(h, tabs_f, tabs_b, norm_g, o_stack, batch, seq):
    n = h.shape[0]
    nb = seq // BLOCK
    w = BRANCH_W
    tab_specs = [pl.BlockSpec((N_HEADS, BLOCK, BLOCK), lambda b, c: (0, 0, 0)),
                 pl.BlockSpec((N_HEADS, BLOCK, HEAD_DIM), lambda b, c: (0, 0, 0)),
                 pl.BlockSpec((N_HEADS, BLOCK, HEAD_DIM), lambda b, c: (0, 0, 0)),
                 pl.BlockSpec((N_HEADS, 1, HEAD_DIM), lambda b, c: (0, 0, 0))]

    def specs(rowmap):
        return [pl.BlockSpec((BLOCK, w), lambda b, c: (rowmap(b, c), COL_CQ // w)),
                pl.BlockSpec((BLOCK, w), lambda b, c: (rowmap(b, c), COL_CK // w)),
                pl.BlockSpec((BLOCK, w), lambda b, c: (rowmap(b, c), COL_CV // w))]

    fmap = lambda b, c: b * nb + c
    bmap = lambda b, c: b * nb + (nb - 1 - c)
    scratch = [pltpu.VMEM((N_HEADS, HEAD_DIM, HEAD_DIM), F32)]
    o_f = pl.pallas_call(
        functools.partial(_retention_kernel, final=False),
        grid=(batch, nb),
        in_specs=specs(fmap) + tab_specs,
        out_specs=pl.BlockSpec((BLOCK, w), lambda b, c: (fmap(b, c), 0)),
        out_shape=jax.ShapeDtypeStruct((n, w), F32),
        scratch_shapes=scratch,
        compiler_params=_params(("parallel", "arbitrary")),
        name="retention_fwd",
    )(h, h, h, *tabs_f)
    return pl.pallas_call(
        functools.partial(_retention_kernel, final=True),
        grid=(batch, nb),
        in_specs=specs(bmap) + tab_specs + [
            pl.BlockSpec((BLOCK, w), lambda b, c: (bmap(b, c), 0)),
            pl.BlockSpec((BLOCK, w), lambda b, c: (bmap(b, c), COL_CG // w)),
            pl.BlockSpec((1, w), lambda b, c: (0, 0)),
            pl.BlockSpec(memory_space=pl.ANY)],
        out_specs=pl.BlockSpec((None, BLOCK, w), lambda b, c: (2, bmap(b, c), 0)),
        out_shape=jax.ShapeDtypeStruct(o_stack.shape, o_stack.dtype),
        input_output_aliases={10: 0},
        scratch_shapes=scratch,
        compiler_params=_params(("parallel", "arbitrary")),
        name="retention_bwd",
    )(h, h, h, *tabs_b, o_f, h, norm_g, o_stack)


def retention_tables(dec, scale, backward):
    lg = jnp.log1p(-jnp.exp(dec.astype(F32)))
    pos = jnp.arange(BLOCK, dtype=F32)
    diff = pos[:, None] - pos[None, :]
    if backward:
        diff = -diff
        mask = diff > 0
        qexp = BLOCK - pos
        kexp = pos
    else:
        mask = diff >= 0
        qexp = pos + 1.0
        kexp = BLOCK - 1.0 - pos
    inner = jnp.where(mask[None], jnp.exp(lg[:, None, None] * jnp.where(mask, diff, 0.0)[None]), 0.0) * scale
    qdec = jnp.broadcast_to(jnp.exp(lg[:, None] * qexp)[:, :, None], (N_HEADS, BLOCK, HEAD_DIM))
    kdec = jnp.broadcast_to((jnp.exp(lg[:, None] * kexp) * scale)[:, :, None], (N_HEADS, BLOCK, HEAD_DIM))
    cdec = jnp.broadcast_to(jnp.exp(lg * BLOCK)[:, None, None], (N_HEADS, 1, HEAD_DIM))
    return inner, qdec, kdec, cdec


def _attn_d_kernel(slope_ref, lam_ref, g_ref, q_ref, k_ref, vt_ref, stack_ref, o_ref, m_ref, acc_ref, *bufs,
                   tk, lam_init):
    tq = q_ref.shape[1]
    seq = k_ref.shape[0]
    slope = slope_ref[pl.program_id(1)]
    q0 = pl.program_id(2) * tq
    feat = lax.broadcasted_iota(jnp.int32, (HEAD_DIM, tq), 0)
    q = q_ref[...]
    zero = jnp.zeros_like(q)
    qt = jnp.concatenate([jnp.where(feat < D_HALF, q, zero), jnp.where(feat >= D_HALF, q, zero)], axis=1)
    rel = (lax.broadcasted_iota(jnp.int32, (tk, tq), 0) - lax.broadcasted_iota(jnp.int32, (tk, tq), 1)).astype(F32)

    def score_fn(c):
        off = pl.multiple_of(c * tk, tk)
        bias = slope * jnp.abs(rel + (off - q0).astype(F32))
        return _dot(k_ref[pl.ds(off, tk), :], qt) - jnp.concatenate([bias, bias], axis=1), 0.0

    def vt_fn(c):
        return vt_ref[:, pl.ds(pl.multiple_of(c * tk, tk), tk)]

    _flash_init(m_ref, acc_ref)
    _flash_pipeline(seq // tk, score_fn, vt_fn, (bufs[:3], bufs[3:]), m_ref, acc_ref)
    lp = lam_ref[...]
    lam = (jnp.exp(jnp.sum(lp[0:1] * lp[1:2], axis=-1, keepdims=True))
           - jnp.exp(jnp.sum(lp[2:3] * lp[3:4], axis=-1, keepdims=True)) + lam_init)
    on = acc_ref[:HEAD_DIM, :] / acc_ref[HEAD_DIM:HEAD_DIM + 1, :]
    o = on[:, :tq] - lam * on[:, tq:]
    y = o * lax.rsqrt(jnp.mean(o * o, axis=0, keepdims=True) + RMS_EPS) * g_ref[...]
    o_ref[...] = (y * (1.0 - lam_init)).T.astype(o_ref.dtype)


def attn_d(qt, h, vt, o_stack, slopes_log2, lam_params, subln_g, lam_init, batch, seq, tq, tk):
    n = h.shape[0]
    nq = seq // tq
    return pl.pallas_call(
        functools.partial(_attn_d_kernel, tk=tk, lam_init=lam_init),
        grid=(batch, N_HEADS, nq),
        in_specs=[pl.BlockSpec(memory_space=pltpu.SMEM),
                  pl.BlockSpec((4, D_HALF), lambda b, hh, i: (0, 0)),
                  pl.BlockSpec((HEAD_DIM, 1), lambda b, hh, i: (0, 0)),
                  pl.BlockSpec((HEAD_DIM, tq), lambda b, hh, i: (hh, b * nq + i)),
                  pl.BlockSpec((seq, HEAD_DIM), lambda b, hh, i: (b, COL_DK // HEAD_DIM + hh)),
                  pl.BlockSpec((HEAD_DIM, seq), lambda b, hh, i: (hh, b)),
                  pl.BlockSpec(memory_space=pl.ANY)],
        out_specs=pl.BlockSpec((None, tq, HEAD_DIM), lambda b, hh, i: (3, b * nq + i, hh)),
        out_shape=jax.ShapeDtypeStruct(o_stack.shape, o_stack.dtype),
        input_output_aliases={6: 0},
        scratch_shapes=_flash_scratch(tk, 2 * tq),
        compiler_params=_params(("parallel", "parallel", "parallel")),
        name="attn_d",
    )(slopes_log2, lam_params, subln_g, qt, h, vt, o_stack)


def _gate_merge_kernel(x_ref, *refs):
    wg_refs, o_refs, wb_refs = refs[:N_BRANCH], refs[N_BRANCH:2 * N_BRANCH], refs[2 * N_BRANCH:3 * N_BRANCH]
    out_ref = refs[3 * N_BRANCH]
    x = x_ref[...]
    acc = None
    for i in range(N_BRANCH):
        gate = _dot(x, wg_refs[i][...])
        contrib = _dot(o_refs[i][...], wb_refs[i][...]) / (1.0 + jnp.exp(-gate))
        acc = contrib if acc is None else acc + contrib
    out_ref[...] = acc.astype(out_ref.dtype)


def gate_merge(x_bf, w_in_bf, o_stack, w_branch_bf, layer, tm, tn):
    n, d = x_bf.shape
    gcol0 = MIX_COLS // tn
    per = d // tn
    once = dict(pipeline_mode=pl.Buffered(1))
    wg_specs = [pl.BlockSpec((None, d, tn), functools.partial(lambda m, j, i: (layer, 0, gcol0 + i * per + j), i=i))
                for i in range(N_BRANCH)]
    o_specs = [pl.BlockSpec((None, tm, BRANCH_W), functools.partial(lambda m, j, i: (i, m, 0), i=i), **once)
               for i in range(N_BRANCH)]
    wb_specs = [pl.BlockSpec((None, None, BRANCH_W, tn), functools.partial(lambda m, j, i: (layer, i, 0, j), i=i))
                for i in range(N_BRANCH)]
    return pl.pallas_call(
        _gate_merge_kernel,
        grid=(n // tm, d // tn),
        in_specs=[pl.BlockSpec((tm, d), lambda m, j: (m, 0))] + wg_specs + o_specs + wb_specs,
        out_specs=pl.BlockSpec((tm, tn), lambda m, j: (m, j)),
        out_shape=jax.ShapeDtypeStruct((n, d), BF16),
        compiler_params=_params(("parallel", "parallel")),
        name="gate_merge",
    )(x_bf, *([w_in_bf] * N_BRANCH), *([o_stack] * N_BRANCH), *([w_branch_bf] * N_BRANCH))


def _layer_norm(y, g, b):
    mu = jnp.mean(y, axis=-1, keepdims=True)
    d = y - mu
    var = jnp.mean(d * d, axis=-1, keepdims=True)
    return d * lax.rsqrt(var + LN_EPS) * g + b


def _out_ln_kernel(m_ref, w_ref, x_ref, g_ref, b_ref, of_ref, ob_ref, *, alpha):
    k = pl.program_id(1)

    @pl.when(k == 0)
    def _():
        of_ref[...] = jnp.zeros(of_ref.shape, F32)

    of_ref[...] += _dot(m_ref[...], w_ref[...])

    @pl.when(k == pl.num_programs(1) - 1)
    def _():
        rows = min(of_ref.shape[0], LN_ROWS)

        def body(r, c):
            sl = pl.ds(pl.multiple_of(r * rows, rows), rows)
            out = _layer_norm(alpha * x_ref[sl, :] + of_ref[sl, :], g_ref[...], b_ref[...])
            of_ref[sl, :] = out
            ob_ref[sl, :] = out.astype(ob_ref.dtype)
            return c

        lax.fori_loop(0, of_ref.shape[0] // rows, body, 0)


LN_ROWS = 128


def out_proj_ln(merged, w_out_bf, x, g, b, layer, alpha, tm, tk):
    n, d = x.shape
    return pl.pallas_call(
        functools.partial(_out_ln_kernel, alpha=alpha),
        grid=(n // tm, d // tk),
        in_specs=[pl.BlockSpec((tm, tk), lambda m, k: (m, k)),
                  pl.BlockSpec((None, tk, d), lambda m, k: (layer, k, 0)),
                  pl.BlockSpec((tm, d), lambda m, k: (m, 0), pipeline_mode=pl.Buffered(1)),
                  pl.BlockSpec((None, 1, d), lambda m, k: (layer, 0, 0)),
                  pl.BlockSpec((None, 1, d), lambda m, k: (layer, 0, 0))],
        out_specs=[pl.BlockSpec((tm, d), lambda m, k: (m, 0)), pl.BlockSpec((tm, d), lambda m, k: (m, 0))],
        out_shape=[jax.ShapeDtypeStruct((n, d), F32), jax.ShapeDtypeStruct((n, d), BF16)],
        compiler_params=_params(("parallel", "arbitrary")),
        name="out_proj_ln",
    )(merged, w_out_bf, x, g, b)


def _router_kernel(x_ref, whi_ref, wlo_ref, b_ref, o_ref):
    x = x_ref[...]
    x_hi = x.astype(BF16)
    x_lo = (x - x_hi.astype(F32)).astype(BF16)
    whi = whi_ref[...]
    logits = _dot(x_hi, whi) + _dot(x_lo, whi) + _dot(x_hi, wlo_ref[...]) + b_ref[...]
    lane = lax.broadcasted_iota(jnp.int32, logits.shape, 1)
    big = jnp.int32(LANES)
    gmask = lane < N_GROUPS
    gl = jnp.where(gmask, logits, NEG_BIG)
    gmax = jnp.max(gl, axis=-1, keepdims=True)
    g_top = jnp.min(jnp.where(gmask & (gl == gmax), lane, big), axis=-1, keepdims=True)
    g_w = 1.0 / jnp.sum(jnp.where(gmask, jnp.exp(gl - gmax), 0.0), axis=-1, keepdims=True)
    lo = N_GROUPS + g_top * EXPERTS_PER_GROUP
    emask = (lane >= lo) & (lane < lo + EXPERTS_PER_GROUP)
    e1 = jnp.where(emask, logits, NEG_BIG)
    v1 = jnp.max(e1, axis=-1, keepdims=True)
    i1 = jnp.min(jnp.where(emask & (e1 == v1), lane, big), axis=-1, keepdims=True)
    emask2 = emask & (lane != i1)
    e2 = jnp.where(emask2, logits, NEG_BIG)
    v2 = jnp.max(e2, axis=-1, keepdims=True)
    i2 = jnp.min(jnp.where(emask2 & (e2 == v2), lane, big), axis=-1, keepdims=True)
    t = jnp.exp(v2 - v1)
    w1 = g_w / (1.0 + t)
    w2 = g_w * t / (1.0 + t)
    o_ref[...] = (jnp.where(lane == 0, (i1 - N_GROUPS).astype(F32), 0.0)
                  + jnp.where(lane == 1, (i2 - N_GROUPS).astype(F32), 0.0)
                  + jnp.where(lane == 2, w1, 0.0) + jnp.where(lane == 3, w2, 0.0))


def router(x, w_hi, w_lo, bias, tm):
    n, d = x.shape
    return pl.pallas_call(
        _router_kernel,
        grid=(n // tm,),
        in_specs=[pl.BlockSpec((tm, d), lambda m: (m, 0)),
                  pl.BlockSpec((d, LANES), lambda m: (0, 0)),
                  pl.BlockSpec((d, LANES), lambda m: (0, 0)),
                  pl.BlockSpec((1, LANES), lambda m: (0, 0))],
        out_specs=pl.BlockSpec((tm, LANES), lambda m: (m, 0)),
        out_shape=jax.ShapeDtypeStruct((n, LANES), F32),
        compiler_params=_params(("parallel",)),
        name="router",
    )(x, w_hi, w_lo, bias)


MOE_TM = 256


def moe_routing(route, n_tok, tm):
    n_pairs = 2 * n_tok
    e = route[:, 0:2].astype(jnp.int32).reshape(n_pairs)
    w = route[:, 2:4].reshape(n_pairs)
    counts = jnp.sum((e[:, None] == jnp.arange(N_EXPERTS)[None, :]).astype(jnp.int32), axis=0)
    tiles_per = (counts + tm - 1) // tm
    tile_end = jnp.cumsum(tiles_per)
    n_tiles = tile_end[-1]
    t_max = n_pairs // tm + N_EXPERTS
    tile_expert = jnp.minimum(jnp.searchsorted(tile_end, jnp.arange(t_max), side="right"), N_EXPERTS - 1).astype(jnp.int32)
    pad_exp = jnp.repeat(jnp.arange(N_EXPERTS, dtype=jnp.int32), tm)
    pad_rank = jnp.tile(jnp.arange(tm, dtype=jnp.int32), N_EXPERTS)
    pad_key = jnp.where(pad_rank < (tiles_per * tm - counts)[pad_exp], 2 * pad_exp + 1, 2 * N_EXPERTS)
    keys = jnp.concatenate([2 * e, pad_key])
    ident = jnp.concatenate([jnp.arange(n_pairs, dtype=jnp.int32),
                             n_pairs + jnp.arange(N_EXPERTS * tm, dtype=jnp.int32)])
    _, ident, wslot = lax.sort((keys, ident, jnp.concatenate([w, jnp.zeros((N_EXPERTS * tm,), w.dtype)])),
                               num_keys=1, is_stable=True)
    valid = ident < n_pairs
    tok = jnp.where(valid, ident // 2, 0)
    dst = jnp.where(valid, (ident % 2) * n_tok + ident // 2, ident)
    return (tile_expert, n_tiles.astype(jnp.int32)[None], tok.astype(jnp.int32), dst.astype(jnp.int32),
            jnp.broadcast_to(wslot[:, None], (t_max * tm, LANES)))


def _moe_kernel(te_ref, nt_ref, tok_ref, dst_ref, x_hbm, w_ref, wgu_ref, wd_ref, out_hbm,
                xbuf, ybuf, xb_ref, gsem, ssem):
    t = pl.program_id(0)
    nt = nt_ref[0]
    tm = xbuf.shape[1]
    f = wd_ref.shape[0]

    def gather_row(tile, r, slot):
        tok = tok_ref[tile * tm + r]
        return pltpu.make_async_copy(x_hbm.at[pl.ds(tok, 1), :], xbuf.at[slot, pl.ds(r, 1), :], gsem.at[slot])

    def scatter_row(tile, r, slot):
        d = dst_ref[tile * tm + r]
        return pltpu.make_async_copy(ybuf.at[slot, pl.ds(r, 1), :], out_hbm.at[pl.ds(d, 1), :], ssem.at[slot])

    def for_rows(fn):
        def body(r, c):
            fn(r)
            return c
        lax.fori_loop(0, tm, body, 0, unroll=8)

    def straight_line_rows(fn):
        for r in range(tm):
            fn(r)

    def tile_step(has_prev):
        slot = t % 2
        other = 1 - slot
        for_rows(lambda r: gather_row(t, r, slot).wait())
        xb_ref[...] = xbuf[slot].astype(BF16)
        nxt = jnp.minimum(t + 1, nt - 1)
        straight_line_rows(lambda r: gather_row(nxt, r, other).start())
        if has_prev:
            straight_line_rows(lambda r: scatter_row(t - 1, r, other).start())
        gu = _dot(xb_ref[...], wgu_ref[...])
        gate = gu[:, :f]
        w = w_ref[...]
        hmid = (gate / (1.0 + jnp.exp(-gate))) * gu[:, f:] * jnp.concatenate([w] * (f // LANES), axis=1)
        y = _dot(hmid.astype(BF16), wd_ref[...])
        if has_prev:
            @pl.when(t >= 2)
            def _():
                for_rows(lambda r: scatter_row(t - 2, r, slot).wait())
        ybuf[slot] = y

    @pl.when((t == 0) & (nt > 0))
    def _():
        for_rows(lambda r: gather_row(0, r, 0).start())
        tile_step(False)

    @pl.when((t > 0) & (t < nt))
    def _():
        tile_step(True)

    @pl.when((t == pl.num_programs(0) - 1) & (nt > 0))
    def _():
        for_rows(lambda r: scatter_row(nt - 1, r, (nt - 1) % 2).start())
        for_rows(lambda r: gather_row(nt - 1, r, nt % 2).wait())
        for_rows(lambda r: scatter_row(nt - 1, r, (nt - 1) % 2).wait())

        @pl.when(nt >= 2)
        def _():
            for_rows(lambda r: scatter_row(nt - 2, r, nt % 2).wait())


def moe_experts(x, routing, wgu_bf, wd_bf, layer, tm):
    n, d = x.shape
    f = wd_bf.shape[-2]
    tile_expert, n_tiles, tok, dst, wslot = routing
    t_max = tile_expert.shape[0]
    grid_spec = pltpu.PrefetchScalarGridSpec(
        num_scalar_prefetch=4,
        grid=(t_max,),
        in_specs=[pl.BlockSpec(memory_space=pl.ANY),
                  pl.BlockSpec((tm, LANES), lambda t, te, nt, tk, ds: (t, 0)),
                  pl.BlockSpec((None, None, d, 2 * f), lambda t, te, nt, tk, ds: (layer, te[t], 0, 0)),
                  pl.BlockSpec((None, None, f, d), lambda t, te, nt, tk, ds: (layer, te[t], 0, 0))],
        out_specs=pl.BlockSpec(memory_space=pl.ANY),
        scratch_shapes=[pltpu.VMEM((2, tm, d), F32), pltpu.VMEM((2, tm, d), F32), pltpu.VMEM((tm, d), BF16),
                        pltpu.SemaphoreType.DMA((2,)), pltpu.SemaphoreType.DMA((2,))])
    return pl.pallas_call(
        _moe_kernel,
        grid_spec=grid_spec,
        out_shape=jax.ShapeDtypeStruct((t_max * tm, d), F32),
        compiler_params=_params(("arbitrary",)),
        name="moe_experts",
    )(tile_expert, n_tiles, tok, dst, x, wslot, wgu_bf, wd_bf)


def _pair_ln_kernel(y0_ref, y1_ref, x_ref, g_ref, b_ref, of_ref, ob_ref, *, alpha):
    out = _layer_norm(alpha * x_ref[...] + (y0_ref[...] + y1_ref[...]), g_ref[...], b_ref[...])
    of_ref[...] = out
    ob_ref[...] = out.astype(ob_ref.dtype)


def pair_sum_ln(y_pairs, x, g, b, layer, alpha, tm):
    n, d = x.shape
    nblk = n // tm
    return pl.pallas_call(
        functools.partial(_pair_ln_kernel, alpha=alpha),
        grid=(nblk,),
        in_specs=[pl.BlockSpec((tm, d), lambda m: (m, 0)),
                  pl.BlockSpec((tm, d), lambda m: (nblk + m, 0)),
                  pl.BlockSpec((tm, d), lambda m: (m, 0)),
                  pl.BlockSpec((None, 1, d), lambda m: (layer, 0, 0)),
                  pl.BlockSpec((None, 1, d), lambda m: (layer, 0, 0))],
        out_specs=[pl.BlockSpec((tm, d), lambda m: (m, 0)), pl.BlockSpec((tm, d), lambda m: (m, 0))],
        out_shape=[jax.ShapeDtypeStruct((n, d), F32), jax.ShapeDtypeStruct((n, d), BF16)],
        compiler_params=_params(("parallel",)),
        name="pair_sum_ln",
    )(y_pairs, y_pairs, x, g, b)


def _rope_tables(seq):
    t = jnp.arange(seq)
    row_id = (t // GRID_W).astype(F32)
    col_id = (t % GRID_W).astype(F32)
    nfreq = HEAD_DIM // 4
    inv = ROPE_THETA ** (-jnp.arange(nfreq, dtype=F32) / nfreq)
    ar = row_id[:, None] * inv
    ac = col_id[:, None] * inv
    cos = jnp.concatenate([jnp.cos(ar), jnp.cos(ar), jnp.cos(ac), jnp.cos(ac)], axis=-1)
    sin = jnp.concatenate([-jnp.sin(ar), jnp.sin(ar), -jnp.sin(ac), jnp.sin(ac)], axis=-1)
    return cos, sin


def _tile(n, pref):
    t = min(n, pref)
    while n % t:
        t //= 2
    return t


def kernel(x, w_in, attn_sink, qk_norm_q, qk_norm_k, ret_decay_fwd, ret_decay_bwd, ret_norm_g,
           diff_lambda_q1, diff_lambda_k1, diff_lambda_q2, diff_lambda_k2, diff_subln_g,
           w_branch, w_out, ln_mix_g, ln_mix_b, router_group, router_group_b, router_expert,
           router_expert_b, expert_w_gate, expert_w_up, expert_w_down, ln_ffn_g, ln_ffn_b):
    batch, seq, d = x.shape
    depth = w_in.shape[0]
    n = batch * seq
    alpha = (2.0 * depth) ** 0.25
    assert w_in.shape[2] == MIX_COLS + N_BRANCH * d and seq % BLOCK == 0 and seq % GRID_W == 0

    colscale = jnp.ones((w_in.shape[2],), F32)
    colscale = colscale.at[COL_AQ:COL_AQ + BRANCH_W].set(HEAD_DIM ** -0.5 * LOG2E)
    colscale = colscale.at[COL_DQ:COL_DQ + BRANCH_W].set(D_HALF ** -0.5 * LOG2E)
    w_in_bf = (w_in * colscale).astype(BF16)
    w_branch_bf = w_branch.astype(BF16)
    w_out_bf = w_out.astype(BF16)
    wgu_bf = jnp.concatenate([expert_w_gate.astype(BF16), expert_w_up.astype(BF16)], axis=-1)
    wd_bf = expert_w_down.astype(BF16)
    ln_mix_g3, ln_mix_b3 = ln_mix_g[:, None, :], ln_mix_b[:, None, :]
    ln_ffn_g3, ln_ffn_b3 = ln_ffn_g[:, None, :], ln_ffn_b[:, None, :]

    cos_t, sin_t = _rope_tables(seq)
    slopes_log2 = jnp.asarray([LOG2E * 2.0 ** (-(i + 1.0)) for i in range(N_HEADS)], F32)

    tm_big = _tile(n, 1024)
    xf = x.reshape(n, d)
    xb = xf.astype(BF16)
    for l in range(depth):
        h = in_proj(xb, w_in_bf, l, MIX_COLS, tm_big, 1024)
        o_all = attn_a(h, attn_sink[l].astype(F32) * LOG2E, batch, seq)
        gains = jnp.concatenate([jnp.tile(qk_norm_q[l], N_HEADS) * (HEAD_DIM ** -0.5 * LOG2E),
                                 jnp.tile(qk_norm_k[l], KV_HEADS)])[None, :]
        qt_b, k_b = qk_prep(h, gains, cos_t, sin_t, batch, seq, _tile(seq, 512))
        vt_b = h[:, COL_BV:COL_BV + KV_HEADS * HEAD_DIM].T
        o_all = attn_b(qt_b, k_b, vt_b, o_all, batch, seq, _tile(seq, 256), _tile(seq // 2, 512))
        tabs_f = retention_tables(ret_decay_fwd[l], HEAD_DIM ** -0.5, False)
        tabs_b = retention_tables(ret_decay_bwd[l], HEAD_DIM ** -0.5, True)
        o_all = retention(h, tabs_f, tabs_b, ret_norm_g[l][None, :], o_all, batch, seq)
        lam_init = 0.8 - 0.6 * math.exp(-0.3 * l)
        lam_params = jnp.stack([diff_lambda_q1[l], diff_lambda_k1[l], diff_lambda_q2[l], diff_lambda_k2[l]]).astype(F32)
        qt_d = h[:, COL_DQ:COL_DQ + BRANCH_W].T
        vt_d = h[:, COL_DV:COL_DV + BRANCH_W].T
        o_all = attn_d(qt_d, h, vt_d, o_all, slopes_log2, lam_params, diff_subln_g[l][:, None], lam_init, batch, seq,
                       _tile(seq, 512), _tile(seq // 2, 512))
        merged = gate_merge(xb, w_in_bf, o_all, w_branch_bf, l, tm_big, 256)
        xf, xb = out_proj_ln(merged, w_out_bf, xf, ln_mix_g3, ln_mix_b3, l, alpha, _tile(n, 512), _tile(d, 512))
        w_r = jnp.concatenate([router_group[l], router_expert[l]], axis=1)
        w_r = jnp.pad(w_r, ((0, 0), (0, LANES - w_r.shape[1])))
        w_r_hi = w_r.astype(BF16)
        w_r_lo = (w_r - w_r_hi.astype(F32)).astype(BF16)
        b_r = jnp.pad(jnp.concatenate([router_group_b[l], router_expert_b[l]]), (0, LANES - N_GROUPS - N_EXPERTS))[None, :]
        route = router(xf, w_r_hi, w_r_lo, b_r, _tile(n, 512))
        y_pairs = moe_experts(xf, moe_routing(route, n, MOE_TM), wgu_bf, wd_bf, l, MOE_TM)
        xf, xb = pair_sum_ln(y_pairs, xf, ln_ffn_g3, ln_ffn_b3, l, alpha, _tile(n, 256))
    return xf.reshape(batch, seq, d)
```

```python
import functools
import math

import jax
import jax.numpy as jnp
from jax import lax
from jax.experimental import pallas as pl
from jax.experimental.pallas import tpu as pltpu

F32 = jnp.float32
BF16 = jnp.bfloat16

HEAD_DIM = 128
BLOCK = 128
GRID_W = 64
N_HEADS = 8
KV_HEADS = 2
D_HALF = HEAD_DIM // 2
N_BRANCH = 4
BRANCH_W = N_HEADS * HEAD_DIM
ROPE_THETA = 10000.0
N_GROUPS = 4
EXPERTS_PER_GROUP = 4
N_EXPERTS = N_GROUPS * EXPERTS_PER_GROUP
LN_EPS = 1e-5
RMS_EPS = 1e-6
NEG_BIG = -1e30
LOG2E = 1.4426950408889634
LANES = 128
VMEM_LIMIT = 56 * 1024 * 1024

COL_AQ, COL_AK, COL_AV = 0, 1024, 1280
COL_BQ, COL_BK, COL_BV = 1536, 2560, 2816
COL_CQ, COL_CK, COL_CV, COL_CG = 3072, 4096, 5120, 6144
COL_DQ, COL_DK, COL_DV = 7168, 8192, 9216
MIX_COLS = 10240


def _params(sem):
    return pltpu.CompilerParams(dimension_semantics=sem, vmem_limit_bytes=VMEM_LIMIT)


def _nt_dot(a, b):
    return lax.dot_general(a, b, (((1,), (1,)), ((), ())), preferred_element_type=F32)


def _tn_dot(a, b):
    return lax.dot_general(a, b, (((0,), (0,)), ((), ())), preferred_element_type=F32)


def _dot(a, b):
    return jnp.dot(a, b, preferred_element_type=F32)


def _mm_kernel(x_ref, w_ref, o_ref):
    o_ref[...] = _dot(x_ref[...], w_ref[...]).astype(o_ref.dtype)


def in_proj(x_bf, w_in_bf, layer, n_cols, tm, tn):
    n, k = x_bf.shape
    return pl.pallas_call(
        _mm_kernel,
        grid=(n // tm, n_cols // tn),
        in_specs=[pl.BlockSpec((tm, k), lambda i, j: (i, 0)),
                  pl.BlockSpec((None, k, tn), lambda i, j: (layer, 0, j))],
        out_specs=pl.BlockSpec((tm, tn), lambda i, j: (i, j)),
        out_shape=jax.ShapeDtypeStruct((n, n_cols), BF16),
        compiler_params=_params(("parallel", "parallel")),
        name="in_proj",
    )(x_bf, w_in_bf)


def _attn_a_kernel(sink_ref, q_ref, kp_ref, kc_ref, kn_ref, vp_ref, vc_ref, vn_ref, o_ref, *, slopes, nb):
    i = pl.program_id(1)
    rep = N_HEADS // KV_HEADS
    rows = rep * BLOCK
    row = lax.broadcasted_iota(jnp.int32, (rows, 3 * BLOCK), 0)
    col = lax.broadcasted_iota(jnp.int32, (rows, 3 * BLOCK), 1)
    rel = (row % BLOCK) - (col - BLOCK)
    dist = jnp.abs(rel)
    valid = (dist <= BLOCK) & ((col >= BLOCK) | (i > 0)) & ((col < 2 * BLOCK) | (i < nb - 1))
    distf = dist.astype(F32)
    rowc = lax.broadcasted_iota(jnp.int32, (rows, 1), 0)
    for g in range(KV_HEADS):
        q = jnp.concatenate([q_ref[:, (g * rep + r) * HEAD_DIM:(g * rep + r + 1) * HEAD_DIM]
                             for r in range(rep)], axis=0)
        ks = slice(g * HEAD_DIM, (g + 1) * HEAD_DIM)
        k = jnp.concatenate([kp_ref[:, ks], kc_ref[:, ks], kn_ref[:, ks]], axis=0)
        v = jnp.concatenate([vp_ref[:, ks], vc_ref[:, ks], vn_ref[:, ks]], axis=0)
        slope = jnp.full((rows, 1), slopes[g * rep], F32)
        sink = jnp.full((rows, 1), sink_ref[g * rep], F32)
        for r in range(1, rep):
            slope = jnp.where(rowc >= r * BLOCK, slopes[g * rep + r], slope)
            sink = jnp.where(rowc >= r * BLOCK, sink_ref[g * rep + r], sink)
        s = _nt_dot(q, k) - slope * distf
        s = jnp.where(valid, s, NEG_BIG)
        m = jnp.maximum(jnp.max(s, axis=-1, keepdims=True), sink)
        p = jnp.exp2(s - m)
        denom = jnp.sum(p, axis=-1, keepdims=True) + jnp.exp2(sink - m)
        o = _dot(p.astype(BF16), v) / denom
        for r in range(rep):
            hh = g * rep + r
            o_ref[:, hh * HEAD_DIM:(hh + 1) * HEAD_DIM] = o[r * BLOCK:(r + 1) * BLOCK].astype(o_ref.dtype)


def attn_a(h, sink_log2, batch, seq):
    n = h.shape[0]
    nb = seq // BLOCK
    slopes = tuple(LOG2E * 2.0 ** (-(i + 1.0)) for i in range(N_HEADS))
    kvw = KV_HEADS * HEAD_DIM
    kcol, vcol = COL_AK // kvw, COL_AV // kvw

    def prev(b, i):
        return b * nb + jnp.maximum(i - 1, 0)

    def nxt(b, i):
        return b * nb + jnp.minimum(i + 1, nb - 1)

    return pl.pallas_call(
        functools.partial(_attn_a_kernel, slopes=slopes, nb=nb),
        grid=(batch, nb),
        in_specs=[pl.BlockSpec(memory_space=pltpu.SMEM),
                  pl.BlockSpec((BLOCK, BRANCH_W), lambda b, i: (b * nb + i, COL_AQ // BRANCH_W)),
                  pl.BlockSpec((BLOCK, kvw), lambda b, i: (prev(b, i), kcol)),
                  pl.BlockSpec((BLOCK, kvw), lambda b, i: (b * nb + i, kcol)),
                  pl.BlockSpec((BLOCK, kvw), lambda b, i: (nxt(b, i), kcol)),
                  pl.BlockSpec((BLOCK, kvw), lambda b, i: (prev(b, i), vcol)),
                  pl.BlockSpec((BLOCK, kvw), lambda b, i: (b * nb + i, vcol)),
                  pl.BlockSpec((BLOCK, kvw), lambda b, i: (nxt(b, i), vcol))],
        out_specs=pl.BlockSpec((None, BLOCK, BRANCH_W), lambda b, i: (0, b * nb + i, 0)),
        out_shape=jax.ShapeDtypeStruct((N_BRANCH, n, BRANCH_W), BF16),
        compiler_params=_params(("parallel", "parallel")),
        name="attn_a",
    )(sink_log2, h, h, h, h, h, h, h)


def _norm_rope(x, g, cos, sin, first_quarter):
    y = x * lax.rsqrt(jnp.mean(x * x, axis=-1, keepdims=True) + RMS_EPS) * g
    partner = jnp.where(first_quarter, pltpu.roll(y, HEAD_DIM - HEAD_DIM // 4, 1), pltpu.roll(y, HEAD_DIM // 4, 1))
    return y * cos + partner * sin


def _qk_prep_kernel(q0_ref, q1_ref, k_ref, g_ref, cos_ref, sin_ref, qt_ref, ko_ref):
    lane = lax.broadcasted_iota(jnp.int32, (k_ref.shape[0], HEAD_DIM), 1)
    first_quarter = (lane % (HEAD_DIM // 2)) < (HEAD_DIM // 4)
    cos = cos_ref[...]
    sin = sin_ref[...]
    half = N_HEADS // 2
    for hh in range(N_HEADS):
        src = q0_ref if hh < half else q1_ref
        sl = slice((hh % half) * HEAD_DIM, (hh % half + 1) * HEAD_DIM)
        gs = slice(hh * HEAD_DIM, (hh + 1) * HEAD_DIM)
        y = _norm_rope(src[:, sl].astype(F32), g_ref[:, gs], cos, sin, first_quarter)
        qt_ref[gs, :] = y.T.astype(qt_ref.dtype)
    for hh in range(KV_HEADS):
        sl = slice(hh * HEAD_DIM, (hh + 1) * HEAD_DIM)
        gs = slice((N_HEADS + hh) * HEAD_DIM, (N_HEADS + hh + 1) * HEAD_DIM)
        ko_ref[:, sl] = _norm_rope(k_ref[:, sl].astype(F32), g_ref[:, gs], cos, sin, first_quarter).astype(ko_ref.dtype)


def qk_prep(h, gains, cos_t, sin_t, batch, seq, tm):
    n = h.shape[0]
    qw = BRANCH_W // 2
    kw = KV_HEADS * HEAD_DIM
    nblk = seq // tm
    return pl.pallas_call(
        _qk_prep_kernel,
        grid=(batch, nblk),
        in_specs=[pl.BlockSpec((tm, qw), lambda b, i: (b * nblk + i, COL_BQ // qw)),
                  pl.BlockSpec((tm, qw), lambda b, i: (b * nblk + i, COL_BQ // qw + 1)),
                  pl.BlockSpec((tm, kw), lambda b, i: (b * nblk + i, COL_BK // kw)),
                  pl.BlockSpec((1, BRANCH_W + kw), lambda b, i: (0, 0)),
                  pl.BlockSpec((tm, HEAD_DIM), lambda b, i: (i, 0)),
                  pl.BlockSpec((tm, HEAD_DIM), lambda b, i: (i, 0))],
        out_specs=[pl.BlockSpec((BRANCH_W, tm), lambda b, i: (0, b * nblk + i)),
                   pl.BlockSpec((tm, kw), lambda b, i: (b * nblk + i, 0))],
        out_shape=[jax.ShapeDtypeStruct((BRANCH_W, n), BF16), jax.ShapeDtypeStruct((n, kw), BF16)],
        compiler_params=_params(("parallel", "parallel")),
        name="qk_prep",
    )(h, h, h, gains, cos_t, sin_t)


ONES_ROWS = 16


def _flash_pipeline(n_chunks, score_fn, vt_fn, bufs, m_ref, acc_ref, per_trip, exp_dtype):
    per_trip = per_trip if n_chunks % per_trip == 0 else 2
    assert n_chunks % per_trip == 0
    scores, accumulate = _flash_stages(m_ref, acc_ref, exp_dtype)
    scores(score_fn(0), bufs[0])

    def body(i, carry):
        c = per_trip * i
        for j in range(per_trip):
            scores(score_fn(jnp.minimum(c + j + 1, n_chunks - 1)), bufs[(j + 1) % 2])
            accumulate(vt_fn(c + j), bufs[j % 2])
        return carry

    lax.fori_loop(0, n_chunks // per_trip, body, 0)


def _flash_init(m_ref, acc_ref):
    m_ref[...] = jnp.full(m_ref.shape, NEG_BIG, F32)
    acc_ref[...] = jnp.zeros(acc_ref.shape, F32)


def _flash_stages(m_ref, acc_ref, exp_dtype):
    def scores(s_and_offset, buf):
        s, offset = s_and_offset
        s_ref, x_ref, c_ref = buf
        s_ref[...] = s
        x_ref[...] = jnp.max(s, axis=0, keepdims=True) + offset
        c_ref[...] = jnp.full(c_ref.shape, offset, F32)

    def accumulate(vt, buf):
        s_ref, x_ref, c_ref = buf
        m_old = m_ref[...]
        m_new = jnp.maximum(m_old, x_ref[...])
        a = jnp.exp2(m_old - m_new)
        p = jnp.exp2((s_ref[...] - (m_new - c_ref[...])).astype(exp_dtype)).astype(BF16)
        vt1 = jnp.concatenate([vt, jnp.ones((ONES_ROWS, vt.shape[1]), vt.dtype)], axis=0)
        acc_ref[...] = a * acc_ref[...] + _dot(vt1, p)
        m_ref[...] = m_new

    return scores, accumulate


def _attn_b_kernel(qt_ref, k_ref, vt_ref, stack_ref, o_ref, m_ref, acc_ref, *bufs, tk):
    rep = qt_ref.shape[0] // HEAD_DIM
    tq = qt_ref.shape[1]
    seq = k_ref.shape[0]
    qt = jnp.concatenate([qt_ref[r * HEAD_DIM:(r + 1) * HEAD_DIM, :] for r in range(rep)], axis=1)

    def score_fn(c):
        return _dot(k_ref[pl.ds(pl.multiple_of(c * tk, tk), tk), :], qt), 0.0

    def vt_fn(c):
        return vt_ref[:, pl.ds(pl.multiple_of(c * tk, tk), tk)]

    _flash_init(m_ref, acc_ref)
    _flash_pipeline(seq // tk, score_fn, vt_fn, (bufs[:3], bufs[3:]), m_ref, acc_ref, per_trip=2, exp_dtype=BF16)
    o = acc_ref[:HEAD_DIM, :] / acc_ref[HEAD_DIM:HEAD_DIM + 1, :]
    for r in range(rep):
        o_ref[:, r * HEAD_DIM:(r + 1) * HEAD_DIM] = o[:, r * tq:(r + 1) * tq].T.astype(o_ref.dtype)


def _flash_scratch(tk, width):
    row = pltpu.VMEM((1, width), F32)
    return [row, pltpu.VMEM((HEAD_DIM + ONES_ROWS, width), F32),
            pltpu.VMEM((tk, width), F32), row, row,
            pltpu.VMEM((tk, width), F32), row, row]


def attn_b(qt, kb, vt, o_stack, batch, seq, tq, tk):
    n = kb.shape[0]
    rep = N_HEADS // KV_HEADS
    nq = seq // tq
    width = rep * tq
    return pl.pallas_call(
        functools.partial(_attn_b_kernel, tk=tk),
        grid=(batch, KV_HEADS, nq),
        in_specs=[pl.BlockSpec((rep * HEAD_DIM, tq), lambda b, g, i: (g, b * nq + i)),
                  pl.BlockSpec((seq, HEAD_DIM), lambda b, g, i: (b, g)),
                  pl.BlockSpec((HEAD_DIM, seq), lambda b, g, i: (g, b)),
                  pl.BlockSpec(memory_space=pl.ANY)],
        out_specs=pl.BlockSpec((None, tq, rep * HEAD_DIM), lambda b, g, i: (1, b * nq + i, g)),
        out_shape=jax.ShapeDtypeStruct(o_stack.shape, o_stack.dtype),
        input_output_aliases={3: 0},
        scratch_shapes=_flash_scratch(tk, width),
        compiler_params=_params(("parallel", "parallel", "parallel")),
        name="attn_b",
    )(qt, kb, vt, o_stack)


def _retention_kernel(*refs, final):
    if final:
        (q_ref, k_ref, v_ref, inner_ref, qdec_ref, kdec_ref, cdec_ref, prev_ref, gate_ref, ng_ref, stack_ref,
         o_ref, state_ref) = refs
    else:
        q_ref, k_ref, v_ref, inner_ref, qdec_ref, kdec_ref, cdec_ref, o_ref, state_ref = refs

    @pl.when(pl.program_id(1) == 0)
    def _():
        state_ref[...] = jnp.zeros(state_ref.shape, F32)

    n_sub = q_ref.shape[0] // BLOCK
    for sub in (reversed(range(n_sub)) if final else range(n_sub)):
        rows = slice(sub * BLOCK, (sub + 1) * BLOCK)
        for hh in range(N_HEADS):
            sl = slice(hh * HEAD_DIM, (hh + 1) * HEAD_DIM)
            q = q_ref[rows, sl]
            k = k_ref[rows, sl]
            v = v_ref[rows, sl]
            state = state_ref[hh]
            inner = _nt_dot(q, k) * inner_ref[hh]
            o = _dot(inner.astype(BF16), v) + _dot(q, state.astype(BF16)) * qdec_ref[hh]
            kd = (k.astype(F32) * kdec_ref[hh]).astype(BF16)
            state_ref[hh] = state * cdec_ref[hh] + _tn_dot(kd, v)
            if final:
                r = o + prev_ref[rows, sl]
                mu = jnp.mean(r, axis=-1, keepdims=True)
                d = r - mu
                var = jnp.mean(d * d, axis=-1, keepdims=True)
                rn = d * lax.rsqrt(var + LN_EPS) * ng_ref[:, sl]
                gate = gate_ref[rows, sl].astype(F32)
                silu = gate / (1.0 + jnp.exp(-gate))
                o_ref[rows, sl] = (silu * rn).astype(o_ref.dtype)
            else:
                o_ref[rows, sl] = o


RET_CHUNKS = 4


def retention(h, tabs_f, tabs_b, norm_g, o_stack, batch, seq):
    n = h.shape[0]
    rb = BLOCK * RET_CHUNKS if seq % (BLOCK * RET_CHUNKS) == 0 else BLOCK
    nb = seq // rb
    w = BRANCH_W
    tab_specs = [pl.BlockSpec((N_HEADS, BLOCK, BLOCK), lambda b, c: (0, 0, 0)),
                 pl.BlockSpec((N_HEADS, BLOCK, HEAD_DIM), lambda b, c: (0, 0, 0)),
                 pl.BlockSpec((N_HEADS, BLOCK, HEAD_DIM), lambda b, c: (0, 0, 0)),
                 pl.BlockSpec((N_HEADS, 1, HEAD_DIM), lambda b, c: (0, 0, 0))]

    def specs(rowmap):
        return [pl.BlockSpec((rb, w), lambda b, c: (rowmap(b, c), COL_CQ // w)),
                pl.BlockSpec((rb, w), lambda b, c: (rowmap(b, c), COL_CK // w)),
                pl.BlockSpec((rb, w), lambda b, c: (rowmap(b, c), COL_CV // w))]

    fmap = lambda b, c: b * nb + c
    bmap = lambda b, c: b * nb + (nb - 1 - c)
    scratch = [pltpu.VMEM((N_HEADS, HEAD_DIM, HEAD_DIM), F32)]
    o_f = pl.pallas_call(
        functools.partial(_retention_kernel, final=False),
        grid=(batch, nb),
        in_specs=specs(fmap) + tab_specs,
        out_specs=pl.BlockSpec((rb, w), lambda b, c: (fmap(b, c), 0)),
        out_shape=jax.ShapeDtypeStruct((n, w), F32),
        scratch_shapes=scratch,
        compiler_params=_params(("parallel", "arbitrary")),
        name="retention_fwd",
    )(h, h, h, *tabs_f)
    return pl.pallas_call(
        functools.partial(_retention_kernel, final=True),
        grid=(batch, nb),
        in_specs=specs(bmap) + tab_specs + [
            pl.BlockSpec((rb, w), lambda b, c: (bmap(b, c), 0)),
            pl.BlockSpec((rb, w), lambda b, c: (bmap(b, c), COL_CG // w)),
            pl.BlockSpec((1, w), lambda b, c: (0, 0)),
            pl.BlockSpec(memory_space=pl.ANY)],
        out_specs=pl.BlockSpec((None, rb, w), lambda b, c: (2, bmap(b, c), 0)),
        out_shape=jax.ShapeDtypeStruct(o_stack.shape, o_stack.dtype),
        input_output_aliases={10: 0},
        scratch_shapes=scratch,
        compiler_params=_params(("parallel", "arbitrary")),
        name="retention_bwd",
    )(h, h, h, *tabs_b, o_f, h, norm_g, o_stack)


def retention_tables(dec, scale, backward):
    lg = jnp.log1p(-jnp.exp(dec.astype(F32)))
    pos = jnp.arange(BLOCK, dtype=F32)
    diff = pos[:, None] - pos[None, :]
    if backward:
        diff = -diff
        mask = diff > 0
        qexp = BLOCK - pos
        kexp = pos
    else:
        mask = diff >= 0
        qexp = pos + 1.0
        kexp = BLOCK - 1.0 - pos
    inner = jnp.where(mask[None], jnp.exp(lg[:, None, None] * jnp.where(mask, diff, 0.0)[None]), 0.0) * scale
    qdec = jnp.broadcast_to(jnp.exp(lg[:, None] * qexp)[:, :, None], (N_HEADS, BLOCK, HEAD_DIM))
    kdec = jnp.broadcast_to((jnp.exp(lg[:, None] * kexp) * scale)[:, :, None], (N_HEADS, BLOCK, HEAD_DIM))
    cdec = jnp.broadcast_to(jnp.exp(lg * BLOCK)[:, None, None], (N_HEADS, 1, HEAD_DIM))
    return inner, qdec, kdec, cdec


def _attn_d_kernel(slope_ref, lam_ref, g_ref, q_ref, k_ref, vt_ref, stack_ref, o_ref, m_ref, acc_ref, *bufs,
                   tk, lam_init):
    tq = q_ref.shape[1]
    seq = k_ref.shape[0]
    slope = slope_ref[pl.program_id(1)]
    q0 = pl.program_id(2) * tq
    feat = lax.broadcasted_iota(jnp.int32, (HEAD_DIM, tq), 0)
    q = q_ref[...]
    zero = jnp.zeros_like(q)
    qt = jnp.concatenate([jnp.where(feat < D_HALF, q, zero), jnp.where(feat >= D_HALF, q, zero)], axis=1)
    rel = slope * (lax.broadcasted_iota(jnp.int32, (tk, tq), 0)
                   - lax.broadcasted_iota(jnp.int32, (tk, tq), 1)).astype(F32)

    def score_fn(c):
        off = pl.multiple_of(c * tk, tk)
        bias = jnp.abs(rel + slope * (off - q0).astype(F32))
        return _dot(k_ref[pl.ds(off, tk), :], qt) - jnp.concatenate([bias, bias], axis=1), 0.0

    def vt_fn(c):
        return vt_ref[:, pl.ds(pl.multiple_of(c * tk, tk), tk)]

    _flash_init(m_ref, acc_ref)
    _flash_pipeline(seq // tk, score_fn, vt_fn, (bufs[:3], bufs[3:]), m_ref, acc_ref, per_trip=2, exp_dtype=BF16)
    lp = lam_ref[...]
    lam = (jnp.exp(jnp.sum(lp[0:1] * lp[1:2], axis=-1, keepdims=True))
           - jnp.exp(jnp.sum(lp[2:3] * lp[3:4], axis=-1, keepdims=True)) + lam_init)
    on = acc_ref[:HEAD_DIM, :] / acc_ref[HEAD_DIM:HEAD_DIM + 1, :]
    o = on[:, :tq] - lam * on[:, tq:]
    y = o * lax.rsqrt(jnp.mean(o * o, axis=0, keepdims=True) + RMS_EPS) * g_ref[...]
    o_ref[...] = (y * (1.0 - lam_init)).T.astype(o_ref.dtype)


def attn_d(qt, h, vt, o_stack, slopes_log2, lam_params, subln_g, lam_init, batch, seq, tq, tk):
    n = h.shape[0]
    nq = seq // tq
    return pl.pallas_call(
        functools.partial(_attn_d_kernel, tk=tk, lam_init=lam_init),
        grid=(batch, N_HEADS, nq),
        in_specs=[pl.BlockSpec(memory_space=pltpu.SMEM),
                  pl.BlockSpec((4, D_HALF), lambda b, hh, i: (0, 0)),
                  pl.BlockSpec((HEAD_DIM, 1), lambda b, hh, i: (0, 0)),
                  pl.BlockSpec((HEAD_DIM, tq), lambda b, hh, i: (hh, b * nq + i)),
                  pl.BlockSpec((seq, HEAD_DIM), lambda b, hh, i: (b, COL_DK // HEAD_DIM + hh)),
                  pl.BlockSpec((HEAD_DIM, seq), lambda b, hh, i: (hh, b)),
                  pl.BlockSpec(memory_space=pl.ANY)],
        out_specs=pl.BlockSpec((None, tq, HEAD_DIM), lambda b, hh, i: (3, b * nq + i, hh)),
        out_shape=jax.ShapeDtypeStruct(o_stack.shape, o_stack.dtype),
        input_output_aliases={6: 0},
        scratch_shapes=_flash_scratch(tk, 2 * tq),
        compiler_params=_params(("parallel", "parallel", "parallel")),
        name="attn_d",
    )(slopes_log2, lam_params, subln_g, qt, h, vt, o_stack)


def _gate_merge_kernel(x_ref, *refs):
    wg_refs, o_refs, wb_refs = refs[:N_BRANCH], refs[N_BRANCH:2 * N_BRANCH], refs[2 * N_BRANCH:3 * N_BRANCH]
    out_ref = refs[3 * N_BRANCH]
    x = x_ref[...]
    acc = None
    for i in range(N_BRANCH):
        gate = _dot(x, wg_refs[i][...])
        contrib = _dot(o_refs[i][...], wb_refs[i][...]) / (1.0 + jnp.exp(-gate))
        acc = contrib if acc is None else acc + contrib
    out_ref[...] = acc.astype(out_ref.dtype)


def gate_merge(x_bf, w_in_bf, o_stack, w_branch_bf, layer, tm, tn):
    n, d = x_bf.shape
    gcol0 = MIX_COLS // tn
    per = d // tn
    once = dict(pipeline_mode=pl.Buffered(1))
    wg_specs = [pl.BlockSpec((None, d, tn), functools.partial(lambda m, j, i: (layer, 0, gcol0 + i * per + j), i=i))
                for i in range(N_BRANCH)]
    o_specs = [pl.BlockSpec((None, tm, BRANCH_W), functools.partial(lambda m, j, i: (i, m, 0), i=i), **once)
               for i in range(N_BRANCH)]
    wb_specs = [pl.BlockSpec((None, None, BRANCH_W, tn), functools.partial(lambda m, j, i: (layer, i, 0, j), i=i))
                for i in range(N_BRANCH)]
    return pl.pallas_call(
        _gate_merge_kernel,
        grid=(n // tm, d // tn),
        in_specs=[pl.BlockSpec((tm, d), lambda m, j: (m, 0))] + wg_specs + o_specs + wb_specs,
        out_specs=pl.BlockSpec((tm, tn), lambda m, j: (m, j)),
        out_shape=jax.ShapeDtypeStruct((n, d), BF16),
        compiler_params=_params(("parallel", "parallel")),
        name="gate_merge",
    )(x_bf, *([w_in_bf] * N_BRANCH), *([o_stack] * N_BRANCH), *([w_branch_bf] * N_BRANCH))


def _layer_norm(y, g, b):
    mu = jnp.mean(y, axis=-1, keepdims=True)
    d = y - mu
    var = jnp.mean(d * d, axis=-1, keepdims=True)
    return d * lax.rsqrt(var + LN_EPS) * g + b


def _out_ln_kernel(m_ref, w_ref, x_ref, g_ref, b_ref, of_ref, ob_ref, *, alpha):
    k = pl.program_id(1)

    @pl.when(k == 0)
    def _():
        of_ref[...] = jnp.zeros(of_ref.shape, F32)

    of_ref[...] += _dot(m_ref[...], w_ref[...])

    @pl.when(k == pl.num_programs(1) - 1)
    def _():
        rows = min(of_ref.shape[0], LN_ROWS)

        def body(r, c):
            sl = pl.ds(pl.multiple_of(r * rows, rows), rows)
            out = _layer_norm(alpha * x_ref[sl, :] + of_ref[sl, :], g_ref[...], b_ref[...])
            of_ref[sl, :] = out
            ob_ref[sl, :] = out.astype(ob_ref.dtype)
            return c

        lax.fori_loop(0, of_ref.shape[0] // rows, body, 0)


LN_ROWS = 128


def out_proj_ln(merged, w_out_bf, x, g, b, layer, alpha, tm, tk):
    n, d = x.shape
    return pl.pallas_call(
        functools.partial(_out_ln_kernel, alpha=alpha),
        grid=(n // tm, d // tk),
        in_specs=[pl.BlockSpec((tm, tk), lambda m, k: (m, k)),
                  pl.BlockSpec((None, tk, d), lambda m, k: (layer, k, 0)),
                  pl.BlockSpec((tm, d), lambda m, k: (m, 0), pipeline_mode=pl.Buffered(1)),
                  pl.BlockSpec((None, 1, d), lambda m, k: (layer, 0, 0)),
                  pl.BlockSpec((None, 1, d), lambda m, k: (layer, 0, 0))],
        out_specs=[pl.BlockSpec((tm, d), lambda m, k: (m, 0)), pl.BlockSpec((tm, d), lambda m, k: (m, 0))],
        out_shape=[jax.ShapeDtypeStruct((n, d), F32), jax.ShapeDtypeStruct((n, d), BF16)],
        compiler_params=_params(("parallel", "arbitrary")),
        name="out_proj_ln",
    )(merged, w_out_bf, x, g, b)


def _router_kernel(x_ref, whi_ref, wlo_ref, b_ref, o_ref):
    x = x_ref[...]
    x_hi = x.astype(BF16)
    x_lo = (x - x_hi.astype(F32)).astype(BF16)
    whi = whi_ref[...]
    logits = _dot(x_hi, whi) + _dot(x_lo, whi) + _dot(x_hi, wlo_ref[...]) + b_ref[...]
    lane = lax.broadcasted_iota(jnp.int32, logits.shape, 1)
    big = jnp.int32(LANES)
    gmask = lane < N_GROUPS
    gl = jnp.where(gmask, logits, NEG_BIG)
    gmax = jnp.max(gl, axis=-1, keepdims=True)
    g_top = jnp.min(jnp.where(gmask & (gl == gmax), lane, big), axis=-1, keepdims=True)
    g_w = 1.0 / jnp.sum(jnp.where(gmask, jnp.exp(gl - gmax), 0.0), axis=-1, keepdims=True)
    lo = N_GROUPS + g_top * EXPERTS_PER_GROUP
    emask = (lane >= lo) & (lane < lo + EXPERTS_PER_GROUP)
    e1 = jnp.where(emask, logits, NEG_BIG)
    v1 = jnp.max(e1, axis=-1, keepdims=True)
    i1 = jnp.min(jnp.where(emask & (e1 == v1), lane, big), axis=-1, keepdims=True)
    emask2 = emask & (lane != i1)
    e2 = jnp.where(emask2, logits, NEG_BIG)
    v2 = jnp.max(e2, axis=-1, keepdims=True)
    i2 = jnp.min(jnp.where(emask2 & (e2 == v2), lane, big), axis=-1, keepdims=True)
    t = jnp.exp(v2 - v1)
    w1 = g_w / (1.0 + t)
    w2 = g_w * t / (1.0 + t)
    o_ref[...] = (jnp.where(lane == 0, (i1 - N_GROUPS).astype(F32), 0.0)
                  + jnp.where(lane == 1, (i2 - N_GROUPS).astype(F32), 0.0)
                  + jnp.where(lane == 2, w1, 0.0) + jnp.where(lane == 3, w2, 0.0))


def router(x, w_hi, w_lo, bias, tm):
    n, d = x.shape
    return pl.pallas_call(
        _router_kernel,
        grid=(n // tm,),
        in_specs=[pl.BlockSpec((tm, d), lambda m: (m, 0)),
                  pl.BlockSpec((d, LANES), lambda m: (0, 0)),
                  pl.BlockSpec((d, LANES), lambda m: (0, 0)),
                  pl.BlockSpec((1, LANES), lambda m: (0, 0))],
        out_specs=pl.BlockSpec((tm, LANES), lambda m: (m, 0)),
        out_shape=jax.ShapeDtypeStruct((n, LANES), F32),
        compiler_params=_params(("parallel",)),
        name="router",
    )(x, w_hi, w_lo, bias)


MOE_TM = 256


def moe_routing(route, n_tok, tm):
    n_pairs = 2 * n_tok
    e = route[:, 0:2].astype(jnp.int32).reshape(n_pairs)
    w = route[:, 2:4].reshape(n_pairs)
    counts = jnp.sum((e[:, None] == jnp.arange(N_EXPERTS)[None, :]).astype(jnp.int32), axis=0)
    tiles_per = (counts + tm - 1) // tm
    tile_end = jnp.cumsum(tiles_per)
    n_tiles = tile_end[-1]
    t_max = n_pairs // tm + N_EXPERTS
    tile_expert = jnp.minimum(jnp.searchsorted(tile_end, jnp.arange(t_max), side="right"), N_EXPERTS - 1).astype(jnp.int32)
    pad_exp = jnp.repeat(jnp.arange(N_EXPERTS, dtype=jnp.int32), tm)
    pad_rank = jnp.tile(jnp.arange(tm, dtype=jnp.int32), N_EXPERTS)
    pad_key = jnp.where(pad_rank < (tiles_per * tm - counts)[pad_exp], 2 * pad_exp + 1, 2 * N_EXPERTS)
    keys = jnp.concatenate([2 * e, pad_key])
    ident = jnp.concatenate([jnp.arange(n_pairs, dtype=jnp.int32),
                             n_pairs + jnp.arange(N_EXPERTS * tm, dtype=jnp.int32)])
    _, ident, wslot = lax.sort((keys, ident, jnp.concatenate([w, jnp.zeros((N_EXPERTS * tm,), w.dtype)])),
                               num_keys=1, is_stable=True)
    valid = ident < n_pairs
    tok = jnp.where(valid, ident // 2, 0)
    dst = jnp.where(valid, (ident % 2) * n_tok + ident // 2, ident)
    return (tile_expert, n_tiles.astype(jnp.int32)[None], tok.astype(jnp.int32), dst.astype(jnp.int32),
            jnp.broadcast_to(wslot[:, None], (t_max * tm, LANES)))


def _moe_kernel(te_ref, nt_ref, tok_ref, dst_ref, x_hbm, w_ref, wgu_ref, wd_ref, out_hbm,
                xbuf, ybuf, gsem, ssem):
    t = pl.program_id(0)
    nt = nt_ref[0]
    tm = xbuf.shape[1]
    f = wd_ref.shape[0]

    def gather_row(tile, r, slot):
        tok = tok_ref[tile * tm + r]
        return pltpu.make_async_copy(x_hbm.at[pl.ds(tok, 1), :], xbuf.at[slot, pl.ds(r, 1), :], gsem.at[slot])

    def scatter_row(tile, r, slot):
        d = dst_ref[tile * tm + r]
        return pltpu.make_async_copy(ybuf.at[slot, pl.ds(r, 1), :], out_hbm.at[pl.ds(d, 1), :], ssem.at[slot])

    def for_rows(fn):
        def body(r, c):
            fn(r)
            return c
        lax.fori_loop(0, tm, body, 0, unroll=8)

    @pl.when((t == 0) & (nt > 0))
    def _():
        for_rows(lambda r: gather_row(0, r, 0).start())

    @pl.when(t + 1 < nt)
    def _():
        for_rows(lambda r: gather_row(t + 1, r, (t + 1) % 2).start())

    @pl.when(t < nt)
    def _():
        slot = t % 2
        for_rows(lambda r: gather_row(t, r, slot).wait())
        xb = xbuf[slot].astype(BF16)
        gu = _dot(xb, wgu_ref[...])
        gate = gu[:, :f]
        w = w_ref[...]
        hmid = (gate / (1.0 + jnp.exp(-gate))) * gu[:, f:] * jnp.concatenate([w] * (f // LANES), axis=1)
        y = _dot(hmid.astype(BF16), wd_ref[...])

        @pl.when(t >= 2)
        def _():
            for_rows(lambda r: scatter_row(t - 2, r, slot).wait())

        ybuf[slot] = y
        for_rows(lambda r: scatter_row(t, r, slot).start())

    @pl.when(t == pl.num_programs(0) - 1)
    def _():
        for back in (1, 2):
            @pl.when(nt >= back)
            def _():
                for_rows(lambda r: scatter_row(nt - back, r, (nt - back) % 2).wait())


def moe_experts(x, routing, wgu_bf, wd_bf, layer, tm):
    n, d = x.shape
    f = wd_bf.shape[-2]
    tile_expert, n_tiles, tok, dst, wslot = routing
    t_max = tile_expert.shape[0]
    grid_spec = pltpu.PrefetchScalarGridSpec(
        num_scalar_prefetch=4,
        grid=(t_max,),
        in_specs=[pl.BlockSpec(memory_space=pl.ANY),
                  pl.BlockSpec((tm, LANES), lambda t, te, nt, tk, ds: (t, 0)),
                  pl.BlockSpec((None, None, d, 2 * f), lambda t, te, nt, tk, ds: (layer, te[t], 0, 0)),
                  pl.BlockSpec((None, None, f, d), lambda t, te, nt, tk, ds: (layer, te[t], 0, 0))],
        out_specs=pl.BlockSpec(memory_space=pl.ANY),
        scratch_shapes=[pltpu.VMEM((2, tm, d), F32), pltpu.VMEM((2, tm, d), F32),
                        pltpu.SemaphoreType.DMA((2,)), pltpu.SemaphoreType.DMA((2,))])
    return pl.pallas_call(
        _moe_kernel,
        grid_spec=grid_spec,
        out_shape=jax.ShapeDtypeStruct((t_max * tm, d), F32),
        compiler_params=_params(("arbitrary",)),
        name="moe_experts",
    )(tile_expert, n_tiles, tok, dst, x, wslot, wgu_bf, wd_bf)


def _pair_ln_kernel(y0_ref, y1_ref, x_ref, g_ref, b_ref, of_ref, ob_ref, *, alpha):
    out = _layer_norm(alpha * x_ref[...] + (y0_ref[...] + y1_ref[...]), g_ref[...], b_ref[...])
    of_ref[...] = out
    ob_ref[...] = out.astype(ob_ref.dtype)


def pair_sum_ln(y_pairs, x, g, b, layer, alpha, tm):
    n, d = x.shape
    nblk = n // tm
    return pl.pallas_call(
        functools.partial(_pair_ln_kernel, alpha=alpha),
        grid=(nblk,),
        in_specs=[pl.BlockSpec((tm, d), lambda m: (m, 0)),
                  pl.BlockSpec((tm, d), lambda m: (nblk + m, 0)),
                  pl.BlockSpec((tm, d), lambda m: (m, 0)),
                  pl.BlockSpec((None, 1, d), lambda m: (layer, 0, 0)),
                  pl.BlockSpec((None, 1, d), lambda m: (layer, 0, 0))],
        out_specs=[pl.BlockSpec((tm, d), lambda m: (m, 0)), pl.BlockSpec((tm, d), lambda m: (m, 0))],
        out_shape=[jax.ShapeDtypeStruct((n, d), F32), jax.ShapeDtypeStruct((n, d), BF16)],
        compiler_params=_params(("parallel",)),
        name="pair_sum_ln",
    )(y_pairs, y_pairs, x, g, b)


def _rope_tables(seq):
    t = jnp.arange(seq)
    row_id = (t // GRID_W).astype(F32)
    col_id = (t % GRID_W).astype(F32)
    nfreq = HEAD_DIM // 4
    inv = ROPE_THETA ** (-jnp.arange(nfreq, dtype=F32) / nfreq)
    ar = row_id[:, None] * inv
    ac = col_id[:, None] * inv
    cos = jnp.concatenate([jnp.cos(ar), jnp.cos(ar), jnp.cos(ac), jnp.cos(ac)], axis=-1)
    sin = jnp.concatenate([-jnp.sin(ar), jnp.sin(ar), -jnp.sin(ac), jnp.sin(ac)], axis=-1)
    return cos, sin


def _tile(n, pref):
    t = min(n, pref)
    while n % t:
        t //= 2
    return t


def kernel(x, w_in, attn_sink, qk_norm_q, qk_norm_k, ret_decay_fwd, ret_decay_bwd, ret_norm_g,
           diff_lambda_q1, diff_lambda_k1, diff_lambda_q2, diff_lambda_k2, diff_subln_g,
           w_branch, w_out, ln_mix_g, ln_mix_b, router_group, router_group_b, router_expert,
           router_expert_b, expert_w_gate, expert_w_up, expert_w_down, ln_ffn_g, ln_ffn_b):
    batch, seq, d = x.shape
    depth = w_in.shape[0]
    n = batch * seq
    alpha = (2.0 * depth) ** 0.25
    assert w_in.shape[2] == MIX_COLS + N_BRANCH * d and seq % BLOCK == 0 and seq % GRID_W == 0

    colscale = jnp.ones((w_in.shape[2],), F32)
    colscale = colscale.at[COL_AQ:COL_AQ + BRANCH_W].set(HEAD_DIM ** -0.5 * LOG2E)
    colscale = colscale.at[COL_DQ:COL_DQ + BRANCH_W].set(D_HALF ** -0.5 * LOG2E)
    w_in_bf = (w_in * colscale).astype(BF16)
    w_branch_bf = w_branch.astype(BF16)
    w_out_bf = w_out.astype(BF16)
    wgu_bf = jnp.concatenate([expert_w_gate.astype(BF16), expert_w_up.astype(BF16)], axis=-1)
    wd_bf = expert_w_down.astype(BF16)
    ln_mix_g3, ln_mix_b3 = ln_mix_g[:, None, :], ln_mix_b[:, None, :]
    ln_ffn_g3, ln_ffn_b3 = ln_ffn_g[:, None, :], ln_ffn_b[:, None, :]

    cos_t, sin_t = _rope_tables(seq)
    slopes_log2 = jnp.asarray([LOG2E * 2.0 ** (-(i + 1.0)) for i in range(N_HEADS)], F32)

    tm_big = _tile(n, 1024)
    xf = x.reshape(n, d)
    xb = xf.astype(BF16)
    for l in range(depth):
        h = in_proj(xb, w_in_bf, l, MIX_COLS, tm_big, 1024)
        o_all = attn_a(h, attn_sink[l].astype(F32) * LOG2E, batch, seq)
        gains = jnp.concatenate([jnp.tile(qk_norm_q[l], N_HEADS) * (HEAD_DIM ** -0.5 * LOG2E),
                                 jnp.tile(qk_norm_k[l], KV_HEADS)])[None, :]
        qt_b, k_b = qk_prep(h, gains, cos_t, sin_t, batch, seq, _tile(seq, 512))
        vt_b = h[:, COL_BV:COL_BV + KV_HEADS * HEAD_DIM].T
        o_all = attn_b(qt_b, k_b, vt_b, o_all, batch, seq, _tile(seq, 256), _tile(seq // 2, 512))
        tabs_f = retention_tables(ret_decay_fwd[l], HEAD_DIM ** -0.5, False)
        tabs_b = retention_tables(ret_decay_bwd[l], HEAD_DIM ** -0.5, True)
        o_all = retention(h, tabs_f, tabs_b, ret_norm_g[l][None, :], o_all, batch, seq)
        lam_init = 0.8 - 0.6 * math.exp(-0.3 * l)
        lam_params = jnp.stack([diff_lambda_q1[l], diff_lambda_k1[l], diff_lambda_q2[l], diff_lambda_k2[l]]).astype(F32)
        qt_d = h[:, COL_DQ:COL_DQ + BRANCH_W].T
        vt_d = h[:, COL_DV:COL_DV + BRANCH_W].T
        o_all = attn_d(qt_d, h, vt_d, o_all, slopes_log2, lam_params, diff_subln_g[l][:, None], lam_init, batch, seq,
                       _tile(seq, 512), _tile(seq // 2, 512))
        merged = gate_merge(xb, w_in_bf, o_all, w_branch_bf, l, tm_big, 256)
        xf, xb = out_proj_ln(merged, w_out_bf, xf, ln_mix_g3, ln_mix_b3, l, alpha, _tile(n, 512), _tile(d, 512))
        w_r = jnp.concatenate([router_group[l], router_expert[l]], axis=1)
        w_r = jnp.pad(w_r, ((0, 0), (0, LANES - w_r.shape[1])))
        w_r_hi = w_r.astype(BF16)
        w_r_lo = (w_r - w_r_hi.astype(F32)).astype(BF16)
        b_r = jnp.pad(jnp.concatenate([router_group_b[l], router_expert_b[l]]), (0, LANES - N_GROUPS - N_EXPERTS))[None, :]
        route = router(xf, w_r_hi, w_r_lo, b_r, _tile(n, 512))
        y_pairs = moe_experts(xf, moe_routing(route, n, MOE_TM), wgu_bf, wd_bf, l, MOE_TM)
        xf, xb = pair_sum_ln(y_pairs, xf, ln_ffn_g3, ln_ffn_b3, l, alpha, _tile(n, 256))
    return xf.reshape(batch, seq, d)
```

```python
import functools
import math

import jax
import jax.numpy as jnp
from jax import lax
from jax.experimental import pallas as pl
from jax.experimental.pallas import tpu as pltpu

F32 = jnp.float32
BF16 = jnp.bfloat16

HEAD_DIM = 128
BLOCK = 128
GRID_W = 64
N_HEADS = 8
KV_HEADS = 2
D_HALF = HEAD_DIM // 2
N_BRANCH = 4
BRANCH_W = N_HEADS * HEAD_DIM
ROPE_THETA = 10000.0
N_GROUPS = 4
EXPERTS_PER_GROUP = 4
N_EXPERTS = N_GROUPS * EXPERTS_PER_GROUP
LN_EPS = 1e-5
RMS_EPS = 1e-6
NEG_BIG = -1e30
LOG2E = 1.4426950408889634
LANES = 128
VMEM_LIMIT = 56 * 1024 * 1024

COL_AQ, COL_AK, COL_AV = 0, 1024, 1280
COL_BQ, COL_BK, COL_BV = 1536, 2560, 2816
COL_CQ, COL_CK, COL_CV, COL_CG = 3072, 4096, 5120, 6144
COL_DQ, COL_DK, COL_DV = 7168, 8192, 9216
MIX_COLS = 10240


def _params(sem):
    return pltpu.CompilerParams(dimension_semantics=sem, vmem_limit_bytes=VMEM_LIMIT)


def _nt_dot(a, b):
    return lax.dot_general(a, b, (((1,), (1,)), ((), ())), preferred_element_type=F32)


def _tn_dot(a, b):
    return lax.dot_general(a, b, (((0,), (0,)), ((), ())), preferred_element_type=F32)


def _dot(a, b):
    return jnp.dot(a, b, preferred_element_type=F32)


def _mm_kernel(x_ref, w_ref, o_ref):
    o_ref[...] = _dot(x_ref[...], w_ref[...]).astype(o_ref.dtype)


def in_proj(x_bf, w_in_bf, layer, n_cols, tm, tn):
    n, k = x_bf.shape
    return pl.pallas_call(
        _mm_kernel,
        grid=(n // tm, n_cols // tn),
        in_specs=[pl.BlockSpec((tm, k), lambda i, j: (i, 0)),
                  pl.BlockSpec((None, k, tn), lambda i, j: (layer, 0, j))],
        out_specs=pl.BlockSpec((tm, tn), lambda i, j: (i, j)),
        out_shape=jax.ShapeDtypeStruct((n, n_cols), BF16),
        compiler_params=_params(("parallel", "parallel")),
        name="in_proj",
    )(x_bf, w_in_bf)


def _attn_a_kernel(sink_ref, q_ref, kp_ref, kc_ref, kn_ref, vp_ref, vc_ref, vn_ref, o_ref, *, slopes, nb):
    i = pl.program_id(1)
    rep = N_HEADS // KV_HEADS
    rows = rep * BLOCK
    row = lax.broadcasted_iota(jnp.int32, (rows, 3 * BLOCK), 0)
    col = lax.broadcasted_iota(jnp.int32, (rows, 3 * BLOCK), 1)
    rel = (row % BLOCK) - (col - BLOCK)
    dist = jnp.abs(rel)
    valid = (dist <= BLOCK) & ((col >= BLOCK) | (i > 0)) & ((col < 2 * BLOCK) | (i < nb - 1))
    distf = dist.astype(F32)
    rowc = lax.broadcasted_iota(jnp.int32, (rows, 1), 0)
    for g in range(KV_HEADS):
        q = jnp.concatenate([q_ref[:, (g * rep + r) * HEAD_DIM:(g * rep + r + 1) * HEAD_DIM]
                             for r in range(rep)], axis=0)
        ks = slice(g * HEAD_DIM, (g + 1) * HEAD_DIM)
        k = jnp.concatenate([kp_ref[:, ks], kc_ref[:, ks], kn_ref[:, ks]], axis=0)
        v = jnp.concatenate([vp_ref[:, ks], vc_ref[:, ks], vn_ref[:, ks]], axis=0)
        slope = jnp.full((rows, 1), slopes[g * rep], F32)
        sink = jnp.full((rows, 1), sink_ref[g * rep], F32)
        for r in range(1, rep):
            slope = jnp.where(rowc >= r * BLOCK, slopes[g * rep + r], slope)
            sink = jnp.where(rowc >= r * BLOCK, sink_ref[g * rep + r], sink)
        s = _nt_dot(q, k) - slope * distf
        s = jnp.where(valid, s, NEG_BIG)
        m = jnp.maximum(jnp.max(s, axis=-1, keepdims=True), sink)
        p = jnp.exp2(s - m)
        denom = jnp.sum(p, axis=-1, keepdims=True) + jnp.exp2(sink - m)
        o = _dot(p.astype(BF16), v) / denom
        for r in range(rep):
            hh = g * rep + r
            o_ref[:, hh * HEAD_DIM:(hh + 1) * HEAD_DIM] = o[r * BLOCK:(r + 1) * BLOCK].astype(o_ref.dtype)


def attn_a(h, sink_log2, batch, seq):
    n = h.shape[0]
    nb = seq // BLOCK
    slopes = tuple(LOG2E * 2.0 ** (-(i + 1.0)) for i in range(N_HEADS))
    kvw = KV_HEADS * HEAD_DIM
    kcol, vcol = COL_AK // kvw, COL_AV // kvw

    def prev(b, i):
        return b * nb + jnp.maximum(i - 1, 0)

    def nxt(b, i):
        return b * nb + jnp.minimum(i + 1, nb - 1)

    return pl.pallas_call(
        functools.partial(_attn_a_kernel, slopes=slopes, nb=nb),
        grid=(batch, nb),
        in_specs=[pl.BlockSpec(memory_space=pltpu.SMEM),
                  pl.BlockSpec((BLOCK, BRANCH_W), lambda b, i: (b * nb + i, COL_AQ // BRANCH_W)),
                  pl.BlockSpec((BLOCK, kvw), lambda b, i: (prev(b, i), kcol)),
                  pl.BlockSpec((BLOCK, kvw), lambda b, i: (b * nb + i, kcol)),
                  pl.BlockSpec((BLOCK, kvw), lambda b, i: (nxt(b, i), kcol)),
                  pl.BlockSpec((BLOCK, kvw), lambda b, i: (prev(b, i), vcol)),
                  pl.BlockSpec((BLOCK, kvw), lambda b, i: (b * nb + i, vcol)),
                  pl.BlockSpec((BLOCK, kvw), lambda b, i: (nxt(b, i), vcol))],
        out_specs=pl.BlockSpec((None, BLOCK, BRANCH_W), lambda b, i: (0, b * nb + i, 0)),
        out_shape=jax.ShapeDtypeStruct((N_BRANCH, n, BRANCH_W), BF16),
        compiler_params=_params(("parallel", "parallel")),
        name="attn_a",
    )(sink_log2, h, h, h, h, h, h, h)


def _norm_rope(x, g, cos, sin, first_quarter):
    y = x * lax.rsqrt(jnp.mean(x * x, axis=-1, keepdims=True) + RMS_EPS) * g
    partner = jnp.where(first_quarter, pltpu.roll(y, HEAD_DIM - HEAD_DIM // 4, 1), pltpu.roll(y, HEAD_DIM // 4, 1))
    return y * cos + partner * sin


def _qk_prep_kernel(q0_ref, q1_ref, k_ref, g_ref, cos_ref, sin_ref, qt_ref, ko_ref):
    lane = lax.broadcasted_iota(jnp.int32, (k_ref.shape[0], HEAD_DIM), 1)
    first_quarter = (lane % (HEAD_DIM // 2)) < (HEAD_DIM // 4)
    cos = cos_ref[...]
    sin = sin_ref[...]
    half = N_HEADS // 2
    for hh in range(N_HEADS):
        src = q0_ref if hh < half else q1_ref
        sl = slice((hh % half) * HEAD_DIM, (hh % half + 1) * HEAD_DIM)
        gs = slice(hh * HEAD_DIM, (hh + 1) * HEAD_DIM)
        y = _norm_rope(src[:, sl].astype(F32), g_ref[:, gs], cos, sin, first_quarter)
        qt_ref[gs, :] = y.T.astype(qt_ref.dtype)
    for hh in range(KV_HEADS):
        sl = slice(hh * HEAD_DIM, (hh + 1) * HEAD_DIM)
        gs = slice((N_HEADS + hh) * HEAD_DIM, (N_HEADS + hh + 1) * HEAD_DIM)
        ko_ref[:, sl] = _norm_rope(k_ref[:, sl].astype(F32), g_ref[:, gs], cos, sin, first_quarter).astype(ko_ref.dtype)


def qk_prep(h, gains, cos_t, sin_t, batch, seq, tm):
    n = h.shape[0]
    qw = BRANCH_W // 2
    kw = KV_HEADS * HEAD_DIM
    nblk = seq // tm
    return pl.pallas_call(
        _qk_prep_kernel,
        grid=(batch, nblk),
        in_specs=[pl.BlockSpec((tm, qw), lambda b, i: (b * nblk + i, COL_BQ // qw)),
                  pl.BlockSpec((tm, qw), lambda b, i: (b * nblk + i, COL_BQ // qw + 1)),
                  pl.BlockSpec((tm, kw), lambda b, i: (b * nblk + i, COL_BK // kw)),
                  pl.BlockSpec((1, BRANCH_W + kw), lambda b, i: (0, 0)),
                  pl.BlockSpec((tm, HEAD_DIM), lambda b, i: (i, 0)),
                  pl.BlockSpec((tm, HEAD_DIM), lambda b, i: (i, 0))],
        out_specs=[pl.BlockSpec((BRANCH_W, tm), lambda b, i: (0, b * nblk + i)),
                   pl.BlockSpec((tm, kw), lambda b, i: (b * nblk + i, 0))],
        out_shape=[jax.ShapeDtypeStruct((BRANCH_W, n), BF16), jax.ShapeDtypeStruct((n, kw), BF16)],
        compiler_params=_params(("parallel", "parallel")),
        name="qk_prep",
    )(h, h, h, gains, cos_t, sin_t)


ONES_ROWS = 16


FLASH_CHUNKS_PER_TRIP = 4


def _flash_pipeline(n_chunks, score_fn, vt_fn, bufs, m_ref, acc_ref):
    per_trip = FLASH_CHUNKS_PER_TRIP if n_chunks % FLASH_CHUNKS_PER_TRIP == 0 else 2
    assert n_chunks % per_trip == 0
    scores, accumulate = _flash_stages(m_ref, acc_ref)
    scores(score_fn(0), bufs[0])

    def body(i, carry):
        c = per_trip * i
        for j in range(per_trip):
            scores(score_fn(jnp.minimum(c + j + 1, n_chunks - 1)), bufs[(j + 1) % 2])
            accumulate(vt_fn(c + j), bufs[j % 2])
        return carry

    lax.fori_loop(0, n_chunks // per_trip, body, 0)


def _flash_init(m_ref, acc_ref):
    m_ref[...] = jnp.full(m_ref.shape, NEG_BIG, F32)
    acc_ref[...] = jnp.zeros(acc_ref.shape, F32)


def _flash_stages(m_ref, acc_ref):
    def scores(s_and_offset, buf):
        s, offset = s_and_offset
        s_ref, x_ref, c_ref = buf
        s_ref[...] = s
        x_ref[...] = jnp.max(s, axis=0, keepdims=True) + offset
        c_ref[...] = jnp.full(c_ref.shape, offset, F32)

    def accumulate(vt, buf):
        s_ref, x_ref, c_ref = buf
        m_old = m_ref[...]
        m_new = jnp.maximum(m_old, x_ref[...])
        a = jnp.exp2(m_old - m_new)
        p = jnp.exp2(s_ref[...] - (m_new - c_ref[...])).astype(BF16)
        vt1 = jnp.concatenate([vt, jnp.ones((ONES_ROWS, vt.shape[1]), vt.dtype)], axis=0)
        acc_ref[...] = a * acc_ref[...] + _dot(vt1, p)
        m_ref[...] = m_new

    return scores, accumulate


def _attn_b_kernel(qt_ref, k_ref, vt_ref, stack_ref, o_ref, m_ref, acc_ref, *bufs, tk):
    rep = qt_ref.shape[0] // HEAD_DIM
    tq = qt_ref.shape[1]
    seq = k_ref.shape[0]
    qt = jnp.concatenate([qt_ref[r * HEAD_DIM:(r + 1) * HEAD_DIM, :] for r in range(rep)], axis=1)

    def score_fn(c):
        return _dot(k_ref[pl.ds(pl.multiple_of(c * tk, tk), tk), :], qt), 0.0

    def vt_fn(c):
        return vt_ref[:, pl.ds(pl.multiple_of(c * tk, tk), tk)]

    _flash_init(m_ref, acc_ref)
    _flash_pipeline(seq // tk, score_fn, vt_fn, (bufs[:3], bufs[3:]), m_ref, acc_ref)
    o = acc_ref[:HEAD_DIM, :] / acc_ref[HEAD_DIM:HEAD_DIM + 1, :]
    for r in range(rep):
        o_ref[:, r * HEAD_DIM:(r + 1) * HEAD_DIM] = o[:, r * tq:(r + 1) * tq].T.astype(o_ref.dtype)


def _flash_scratch(tk, width):
    row = pltpu.VMEM((1, width), F32)
    return [row, pltpu.VMEM((HEAD_DIM + ONES_ROWS, width), F32),
            pltpu.VMEM((tk, width), F32), row, row,
            pltpu.VMEM((tk, width), F32), row, row]


def attn_b(qt, kb, vt, o_stack, batch, seq, tq, tk):
    n = kb.shape[0]
    rep = N_HEADS // KV_HEADS
    nq = seq // tq
    width = rep * tq
    return pl.pallas_call(
        functools.partial(_attn_b_kernel, tk=tk),
        grid=(batch, KV_HEADS, nq),
        in_specs=[pl.BlockSpec((rep * HEAD_DIM, tq), lambda b, g, i: (g, b * nq + i)),
                  pl.BlockSpec((seq, HEAD_DIM), lambda b, g, i: (b, g)),
                  pl.BlockSpec((HEAD_DIM, seq), lambda b, g, i: (g, b)),
                  pl.BlockSpec(memory_space=pl.ANY)],
        out_specs=pl.BlockSpec((None, tq, rep * HEAD_DIM), lambda b, g, i: (1, b * nq + i, g)),
        out_shape=jax.ShapeDtypeStruct(o_stack.shape, o_stack.dtype),
        input_output_aliases={3: 0},
        scratch_shapes=_flash_scratch(tk, width),
        compiler_params=_params(("parallel", "parallel", "parallel")),
        name="attn_b",
    )(qt, kb, vt, o_stack)


def _retention_kernel(*refs, final):
    if final:
        (q_ref, k_ref, v_ref, inner_ref, qdec_ref, kdec_ref, cdec_ref, prev_ref, gate_ref, ng_ref, stack_ref,
         o_ref, state_ref) = refs
    else:
        q_ref, k_ref, v_ref, inner_ref, qdec_ref, kdec_ref, cdec_ref, o_ref, state_ref = refs

    @pl.when(pl.program_id(1) == 0)
    def _():
        state_ref[...] = jnp.zeros(state_ref.shape, F32)

    n_sub = q_ref.shape[0] // BLOCK
    for sub in (reversed(range(n_sub)) if final else range(n_sub)):
        rows = slice(sub * BLOCK, (sub + 1) * BLOCK)
        for hh in range(N_HEADS):
            sl = slice(hh * HEAD_DIM, (hh + 1) * HEAD_DIM)
            q = q_ref[rows, sl]
            k = k_ref[rows, sl]
            v = v_ref[rows, sl]
            state = state_ref[hh]
            inner = _nt_dot(q, k) * inner_ref[hh]
            o = _dot(inner.astype(BF16), v) + _dot(q, state.astype(BF16)) * qdec_ref[hh]
            kd = (k.astype(F32) * kdec_ref[hh]).astype(BF16)
            state_ref[hh] = state * cdec_ref[hh] + _tn_dot(kd, v)
            if final:
                r = o + prev_ref[rows, sl]
                mu = jnp.mean(r, axis=-1, keepdims=True)
                d = r - mu
                var = jnp.mean(d * d, axis=-1, keepdims=True)
                rn = d * lax.rsqrt(var + LN_EPS) * ng_ref[:, sl]
                gate = gate_ref[rows, sl].astype(F32)
                silu = gate / (1.0 + jnp.exp(-gate))
                o_ref[rows, sl] = (silu * rn).astype(o_ref.dtype)
            else:
                o_ref[rows, sl] = o


RET_CHUNKS = 4


def retention(h, tabs_f, tabs_b, norm_g, o_stack, batch, seq):
    n = h.shape[0]
    rb = BLOCK * RET_CHUNKS if seq % (BLOCK * RET_CHUNKS) == 0 else BLOCK
    nb = seq // rb
    w = BRANCH_W
    tab_specs = [pl.BlockSpec((N_HEADS, BLOCK, BLOCK), lambda b, c: (0, 0, 0)),
                 pl.BlockSpec((N_HEADS, BLOCK, HEAD_DIM), lambda b, c: (0, 0, 0)),
                 pl.BlockSpec((N_HEADS, BLOCK, HEAD_DIM), lambda b, c: (0, 0, 0)),
                 pl.BlockSpec((N_HEADS, 1, HEAD_DIM), lambda b, c: (0, 0, 0))]

    def specs(rowmap):
        return [pl.BlockSpec((rb, w), lambda b, c: (rowmap(b, c), COL_CQ // w)),
                pl.BlockSpec((rb, w), lambda b, c: (rowmap(b, c), COL_CK // w)),
                pl.BlockSpec((rb, w), lambda b, c: (rowmap(b, c), COL_CV // w))]

    fmap = lambda b, c: b * nb + c
    bmap = lambda b, c: b * nb + (nb - 1 - c)
    scratch = [pltpu.VMEM((N_HEADS, HEAD_DIM, HEAD_DIM), F32)]
    o_f = pl.pallas_call(
        functools.partial(_retention_kernel, final=False),
        grid=(batch, nb),
        in_specs=specs(fmap) + tab_specs,
        out_specs=pl.BlockSpec((rb, w), lambda b, c: (fmap(b, c), 0)),
        out_shape=jax.ShapeDtypeStruct((n, w), F32),
        scratch_shapes=scratch,
        compiler_params=_params(("parallel", "arbitrary")),
        name="retention_fwd",
    )(h, h, h, *tabs_f)
    return pl.pallas_call(
        functools.partial(_retention_kernel, final=True),
        grid=(batch, nb),
        in_specs=specs(bmap) + tab_specs + [
            pl.BlockSpec((rb, w), lambda b, c: (bmap(b, c), 0)),
            pl.BlockSpec((rb, w), lambda b, c: (bmap(b, c), COL_CG // w)),
            pl.BlockSpec((1, w), lambda b, c: (0, 0)),
            pl.BlockSpec(memory_space=pl.ANY)],
        out_specs=pl.BlockSpec((None, rb, w), lambda b, c: (2, bmap(b, c), 0)),
        out_shape=jax.ShapeDtypeStruct(o_stack.shape, o_stack.dtype),
        input_output_aliases={10: 0},
        scratch_shapes=scratch,
        compiler_params=_params(("parallel", "arbitrary")),
        name="retention_bwd",
    )(h, h, h, *tabs_b, o_f, h, norm_g, o_stack)


def retention_tables(dec, scale, backward):
    lg = jnp.log1p(-jnp.exp(dec.astype(F32)))
    pos = jnp.arange(BLOCK, dtype=F32)
    diff = pos[:, None] - pos[None, :]
    if backward:
        diff = -diff
        mask = diff > 0
        qexp = BLOCK - pos
        kexp = pos
    else:
        mask = diff >= 0
        qexp = pos + 1.0
        kexp = BLOCK - 1.0 - pos
    inner = jnp.where(mask[None], jnp.exp(lg[:, None, None] * jnp.where(mask, diff, 0.0)[None]), 0.0) * scale
    qdec = jnp.broadcast_to(jnp.exp(lg[:, None] * qexp)[:, :, None], (N_HEADS, BLOCK, HEAD_DIM))
    kdec = jnp.broadcast_to((jnp.exp(lg[:, None] * kexp) * scale)[:, :, None], (N_HEADS, BLOCK, HEAD_DIM))
    cdec = jnp.broadcast_to(jnp.exp(lg * BLOCK)[:, None, None], (N_HEADS, 1, HEAD_DIM))
    return inner, qdec, kdec, cdec


def _attn_d_kernel(slope_ref, lam_ref, g_ref, q_ref, k_ref, vt_ref, stack_ref, o_ref, m_ref, acc_ref, *bufs,
                   tk, lam_init):
    tq = q_ref.shape[1]
    seq = k_ref.shape[0]
    slope = slope_ref[pl.program_id(1)]
    q0 = pl.program_id(2) * tq
    feat = lax.broadcasted_iota(jnp.int32, (HEAD_DIM, tq), 0)
    q = q_ref[...]
    zero = jnp.zeros_like(q)
    qt = jnp.concatenate([jnp.where(feat < D_HALF, q, zero), jnp.where(feat >= D_HALF, q, zero)], axis=1)
    rel = slope * (lax.broadcasted_iota(jnp.int32, (tk, tq), 0)
                   - lax.broadcasted_iota(jnp.int32, (tk, tq), 1)).astype(F32)

    def score_fn(c):
        off = pl.multiple_of(c * tk, tk)
        bias = jnp.abs(rel + slope * (off - q0).astype(F32))
        return _dot(k_ref[pl.ds(off, tk), :], qt) - jnp.concatenate([bias, bias], axis=1), 0.0

    def vt_fn(c):
        return vt_ref[:, pl.ds(pl.multiple_of(c * tk, tk), tk)]

    _flash_init(m_ref, acc_ref)
    _flash_pipeline(seq // tk, score_fn, vt_fn, (bufs[:3], bufs[3:]), m_ref, acc_ref)
    lp = lam_ref[...]
    lam = (jnp.exp(jnp.sum(lp[0:1] * lp[1:2], axis=-1, keepdims=True))
           - jnp.exp(jnp.sum(lp[2:3] * lp[3:4], axis=-1, keepdims=True)) + lam_init)
    on = acc_ref[:HEAD_DIM, :] / acc_ref[HEAD_DIM:HEAD_DIM + 1, :]
    o = on[:, :tq] - lam * on[:, tq:]
    y = o * lax.rsqrt(jnp.mean(o * o, axis=0, keepdims=True) + RMS_EPS) * g_ref[...]
    o_ref[...] = (y * (1.0 - lam_init)).T.astype(o_ref.dtype)


def attn_d(qt, h, vt, o_stack, slopes_log2, lam_params, subln_g, lam_init, batch, seq, tq, tk):
    n = h.shape[0]
    nq = seq // tq
    return pl.pallas_call(
        functools.partial(_attn_d_kernel, tk=tk, lam_init=lam_init),
        grid=(batch, N_HEADS, nq),
        in_specs=[pl.BlockSpec(memory_space=pltpu.SMEM),
                  pl.BlockSpec((4, D_HALF), lambda b, hh, i: (0, 0)),
                  pl.BlockSpec((HEAD_DIM, 1), lambda b, hh, i: (0, 0)),
                  pl.BlockSpec((HEAD_DIM, tq), lambda b, hh, i: (hh, b * nq + i)),
                  pl.BlockSpec((seq, HEAD_DIM), lambda b, hh, i: (b, COL_DK // HEAD_DIM + hh)),
                  pl.BlockSpec((HEAD_DIM, seq), lambda b, hh, i: (hh, b)),
                  pl.BlockSpec(memory_space=pl.ANY)],
        out_specs=pl.BlockSpec((None, tq, HEAD_DIM), lambda b, hh, i: (3, b * nq + i, hh)),
        out_shape=jax.ShapeDtypeStruct(o_stack.shape, o_stack.dtype),
        input_output_aliases={6: 0},
        scratch_shapes=_flash_scratch(tk, 2 * tq),
        compiler_params=_params(("parallel", "parallel", "parallel")),
        name="attn_d",
    )(slopes_log2, lam_params, subln_g, qt, h, vt, o_stack)


def _gate_merge_kernel(x_ref, *refs):
    wg_refs, o_refs, wb_refs = refs[:N_BRANCH], refs[N_BRANCH:2 * N_BRANCH], refs[2 * N_BRANCH:3 * N_BRANCH]
    out_ref = refs[3 * N_BRANCH]
    x = x_ref[...]
    acc = None
    for i in range(N_BRANCH):
        gate = _dot(x, wg_refs[i][...])
        contrib = _dot(o_refs[i][...], wb_refs[i][...]) / (1.0 + jnp.exp(-gate))
        acc = contrib if acc is None else acc + contrib
    out_ref[...] = acc.astype(out_ref.dtype)


def gate_merge(x_bf, w_in_bf, o_stack, w_branch_bf, layer, tm, tn):
    n, d = x_bf.shape
    gcol0 = MIX_COLS // tn
    per = d // tn
    once = dict(pipeline_mode=pl.Buffered(1))
    wg_specs = [pl.BlockSpec((None, d, tn), functools.partial(lambda m, j, i: (layer, 0, gcol0 + i * per + j), i=i))
                for i in range(N_BRANCH)]
    o_specs = [pl.BlockSpec((None, tm, BRANCH_W), functools.partial(lambda m, j, i: (i, m, 0), i=i), **once)
               for i in range(N_BRANCH)]
    wb_specs = [pl.BlockSpec((None, None, BRANCH_W, tn), functools.partial(lambda m, j, i: (layer, i, 0, j), i=i))
                for i in range(N_BRANCH)]
    return pl.pallas_call(
        _gate_merge_kernel,
        grid=(n // tm, d // tn),
        in_specs=[pl.BlockSpec((tm, d), lambda m, j: (m, 0))] + wg_specs + o_specs + wb_specs,
        out_specs=pl.BlockSpec((tm, tn), lambda m, j: (m, j)),
        out_shape=jax.ShapeDtypeStruct((n, d), BF16),
        compiler_params=_params(("parallel", "parallel")),
        name="gate_merge",
    )(x_bf, *([w_in_bf] * N_BRANCH), *([o_stack] * N_BRANCH), *([w_branch_bf] * N_BRANCH))


def _layer_norm(y, g, b):
    mu = jnp.mean(y, axis=-1, keepdims=True)
    d = y - mu
    var = jnp.mean(d * d, axis=-1, keepdims=True)
    return d * lax.rsqrt(var + LN_EPS) * g + b


def _out_ln_kernel(m_ref, w_ref, x_ref, g_ref, b_ref, of_ref, ob_ref, *, alpha):
    k = pl.program_id(1)

    @pl.when(k == 0)
    def _():
        of_ref[...] = jnp.zeros(of_ref.shape, F32)

    of_ref[...] += _dot(m_ref[...], w_ref[...])

    @pl.when(k == pl.num_programs(1) - 1)
    def _():
        rows = min(of_ref.shape[0], LN_ROWS)

        def body(r, c):
            sl = pl.ds(pl.multiple_of(r * rows, rows), rows)
            out = _layer_norm(alpha * x_ref[sl, :] + of_ref[sl, :], g_ref[...], b_ref[...])
            of_ref[sl, :] = out
            ob_ref[sl, :] = out.astype(ob_ref.dtype)
            return c

        lax.fori_loop(0, of_ref.shape[0] // rows, body, 0)


LN_ROWS = 128


def out_proj_ln(merged, w_out_bf, x, g, b, layer, alpha, tm, tk):
    n, d = x.shape
    return pl.pallas_call(
        functools.partial(_out_ln_kernel, alpha=alpha),
        grid=(n // tm, d // tk),
        in_specs=[pl.BlockSpec((tm, tk), lambda m, k: (m, k)),
                  pl.BlockSpec((None, tk, d), lambda m, k: (layer, k, 0)),
                  pl.BlockSpec((tm, d), lambda m, k: (m, 0), pipeline_mode=pl.Buffered(1)),
                  pl.BlockSpec((None, 1, d), lambda m, k: (layer, 0, 0)),
                  pl.BlockSpec((None, 1, d), lambda m, k: (layer, 0, 0))],
        out_specs=[pl.BlockSpec((tm, d), lambda m, k: (m, 0)), pl.BlockSpec((tm, d), lambda m, k: (m, 0))],
        out_shape=[jax.ShapeDtypeStruct((n, d), F32), jax.ShapeDtypeStruct((n, d), BF16)],
        compiler_params=_params(("parallel", "arbitrary")),
        name="out_proj_ln",
    )(merged, w_out_bf, x, g, b)


def _router_kernel(x_ref, whi_ref, wlo_ref, b_ref, o_ref):
    x = x_ref[...]
    x_hi = x.astype(BF16)
    x_lo = (x - x_hi.astype(F32)).astype(BF16)
    whi = whi_ref[...]
    logits = _dot(x_hi, whi) + _dot(x_lo, whi) + _dot(x_hi, wlo_ref[...]) + b_ref[...]
    lane = lax.broadcasted_iota(jnp.int32, logits.shape, 1)
    big = jnp.int32(LANES)
    gmask = lane < N_GROUPS
    gl = jnp.where(gmask, logits, NEG_BIG)
    gmax = jnp.max(gl, axis=-1, keepdims=True)
    g_top = jnp.min(jnp.where(gmask & (gl == gmax), lane, big), axis=-1, keepdims=True)
    g_w = 1.0 / jnp.sum(jnp.where(gmask, jnp.exp(gl - gmax), 0.0), axis=-1, keepdims=True)
    lo = N_GROUPS + g_top * EXPERTS_PER_GROUP
    emask = (lane >= lo) & (lane < lo + EXPERTS_PER_GROUP)
    e1 = jnp.where(emask, logits, NEG_BIG)
    v1 = jnp.max(e1, axis=-1, keepdims=True)
    i1 = jnp.min(jnp.where(emask & (e1 == v1), lane, big), axis=-1, keepdims=True)
    emask2 = emask & (lane != i1)
    e2 = jnp.where(emask2, logits, NEG_BIG)
    v2 = jnp.max(e2, axis=-1, keepdims=True)
    i2 = jnp.min(jnp.where(emask2 & (e2 == v2), lane, big), axis=-1, keepdims=True)
    t = jnp.exp(v2 - v1)
    w1 = g_w / (1.0 + t)
    w2 = g_w * t / (1.0 + t)
    o_ref[...] = (jnp.where(lane == 0, (i1 - N_GROUPS).astype(F32), 0.0)
                  + jnp.where(lane == 1, (i2 - N_GROUPS).astype(F32), 0.0)
                  + jnp.where(lane == 2, w1, 0.0) + jnp.where(lane == 3, w2, 0.0))


def router(x, w_hi, w_lo, bias, tm):
    n, d = x.shape
    return pl.pallas_call(
        _router_kernel,
        grid=(n // tm,),
        in_specs=[pl.BlockSpec((tm, d), lambda m: (m, 0)),
                  pl.BlockSpec((d, LANES), lambda m: (0, 0)),
                  pl.BlockSpec((d, LANES), lambda m: (0, 0)),
                  pl.BlockSpec((1, LANES), lambda m: (0, 0))],
        out_specs=pl.BlockSpec((tm, LANES), lambda m: (m, 0)),
        out_shape=jax.ShapeDtypeStruct((n, LANES), F32),
        compiler_params=_params(("parallel",)),
        name="router",
    )(x, w_hi, w_lo, bias)


MOE_TM = 256


def moe_routing(route, n_tok, tm):
    n_pairs = 2 * n_tok
    e = route[:, 0:2].astype(jnp.int32).reshape(n_pairs)
    w = route[:, 2:4].reshape(n_pairs)
    counts = jnp.sum((e[:, None] == jnp.arange(N_EXPERTS)[None, :]).astype(jnp.int32), axis=0)
    tiles_per = (counts + tm - 1) // tm
    tile_end = jnp.cumsum(tiles_per)
    n_tiles = tile_end[-1]
    t_max = n_pairs // tm + N_EXPERTS
    tile_expert = jnp.minimum(jnp.searchsorted(tile_end, jnp.arange(t_max), side="right"), N_EXPERTS - 1).astype(jnp.int32)
    pad_exp = jnp.repeat(jnp.arange(N_EXPERTS, dtype=jnp.int32), tm)
    pad_rank = jnp.tile(jnp.arange(tm, dtype=jnp.int32), N_EXPERTS)
    pad_key = jnp.where(pad_rank < (tiles_per * tm - counts)[pad_exp], 2 * pad_exp + 1, 2 * N_EXPERTS)
    keys = jnp.concatenate([2 * e, pad_key])
    ident = jnp.concatenate([jnp.arange(n_pairs, dtype=jnp.int32),
                             n_pairs + jnp.arange(N_EXPERTS * tm, dtype=jnp.int32)])
    _, ident, wslot = lax.sort((keys, ident, jnp.concatenate([w, jnp.zeros((N_EXPERTS * tm,), w.dtype)])),
                               num_keys=1, is_stable=True)
    valid = ident < n_pairs
    tok = jnp.where(valid, ident // 2, 0)
    dst = jnp.where(valid, (ident % 2) * n_tok + ident // 2, ident)
    return (tile_expert, n_tiles.astype(jnp.int32)[None], tok.astype(jnp.int32), dst.astype(jnp.int32),
            jnp.broadcast_to(wslot[:, None], (t_max * tm, LANES)))


def _moe_kernel(te_ref, nt_ref, tok_ref, dst_ref, x_hbm, w_ref, wgu_ref, wd_ref, out_hbm,
                xbuf, ybuf, gsem, ssem):
    t = pl.program_id(0)
    nt = nt_ref[0]
    tm = xbuf.shape[1]
    f = wd_ref.shape[0]

    def gather_row(tile, r, slot):
        tok = tok_ref[tile * tm + r]
        return pltpu.make_async_copy(x_hbm.at[pl.ds(tok, 1), :], xbuf.at[slot, pl.ds(r, 1), :], gsem.at[slot])

    def scatter_row(tile, r, slot):
        d = dst_ref[tile * tm + r]
        return pltpu.make_async_copy(ybuf.at[slot, pl.ds(r, 1), :], out_hbm.at[pl.ds(d, 1), :], ssem.at[slot])

    def for_rows(fn):
        def body(r, c):
            fn(r)
            return c
        lax.fori_loop(0, tm, body, 0, unroll=8)

    @pl.when((t == 0) & (nt > 0))
    def _():
        for_rows(lambda r: gather_row(0, r, 0).start())

    @pl.when(t + 1 < nt)
    def _():
        for_rows(lambda r: gather_row(t + 1, r, (t + 1) % 2).start())

    @pl.when(t < nt)
    def _():
        slot = t % 2
        for_rows(lambda r: gather_row(t, r, slot).wait())
        xb = xbuf[slot].astype(BF16)
        gu = _dot(xb, wgu_ref[...])
        gate = gu[:, :f]
        w = w_ref[...]
        hmid = (gate / (1.0 + jnp.exp(-gate))) * gu[:, f:] * jnp.concatenate([w] * (f // LANES), axis=1)
        y = _dot(hmid.astype(BF16), wd_ref[...])

        @pl.when(t >= 2)
        def _():
            for_rows(lambda r: scatter_row(t - 2, r, slot).wait())

        ybuf[slot] = y
        for_rows(lambda r: scatter_row(t, r, slot).start())

    @pl.when(t == pl.num_programs(0) - 1)
    def _():
        for back in (1, 2):
            @pl.when(nt >= back)
            def _():
                for_rows(lambda r: scatter_row(nt - back, r, (nt - back) % 2).wait())


def moe_experts(x, routing, wgu_bf, wd_bf, layer, tm):
    n, d = x.shape
    f = wd_bf.shape[-2]
    tile_expert, n_tiles, tok, dst, wslot = routing
    t_max = tile_expert.shape[0]
    grid_spec = pltpu.PrefetchScalarGridSpec(
        num_scalar_prefetch=4,
        grid=(t_max,),
        in_specs=[pl.BlockSpec(memory_space=pl.ANY),
                  pl.BlockSpec((tm, LANES), lambda t, te, nt, tk, ds: (t, 0)),
                  pl.BlockSpec((None, None, d, 2 * f), lambda t, te, nt, tk, ds: (layer, te[t], 0, 0)),
                  pl.BlockSpec((None, None, f, d), lambda t, te, nt, tk, ds: (layer, te[t], 0, 0))],
        out_specs=pl.BlockSpec(memory_space=pl.ANY),
        scratch_shapes=[pltpu.VMEM((2, tm, d), F32), pltpu.VMEM((2, tm, d), F32),
                        pltpu.SemaphoreType.DMA((2,)), pltpu.SemaphoreType.DMA((2,))])
    return pl.pallas_call(
        _moe_kernel,
        grid_spec=grid_spec,
        out_shape=jax.ShapeDtypeStruct((t_max * tm, d), F32),
        compiler_params=_params(("arbitrary",)),
        name="moe_experts",
    )(tile_expert, n_tiles, tok, dst, x, wslot, wgu_bf, wd_bf)


def _pair_ln_kernel(y0_ref, y1_ref, x_ref, g_ref, b_ref, of_ref, ob_ref, *, alpha):
    out = _layer_norm(alpha * x_ref[...] + (y0_ref[...] + y1_ref[...]), g_ref[...], b_ref[...])
    of_ref[...] = out
    ob_ref[...] = out.astype(ob_ref.dtype)


def pair_sum_ln(y_pairs, x, g, b, layer, alpha, tm):
    n, d = x.shape
    nblk = n // tm
    return pl.pallas_call(
        functools.partial(_pair_ln_kernel, alpha=alpha),
        grid=(nblk,),
        in_specs=[pl.BlockSpec((tm, d), lambda m: (m, 0)),
                  pl.BlockSpec((tm, d), lambda m: (nblk + m, 0)),
                  pl.BlockSpec((tm, d), lambda m: (m, 0)),
                  pl.BlockSpec((None, 1, d), lambda m: (layer, 0, 0)),
                  pl.BlockSpec((None, 1, d), lambda m: (layer, 0, 0))],
        out_specs=[pl.BlockSpec((tm, d), lambda m: (m, 0)), pl.BlockSpec((tm, d), lambda m: (m, 0))],
        out_shape=[jax.ShapeDtypeStruct((n, d), F32), jax.ShapeDtypeStruct((n, d), BF16)],
        compiler_params=_params(("parallel",)),
        name="pair_sum_ln",
    )(y_pairs, y_pairs, x, g, b)


def _rope_tables(seq):
    t = jnp.arange(seq)
    row_id = (t // GRID_W).astype(F32)
    col_id = (t % GRID_W).astype(F32)
    nfreq = HEAD_DIM // 4
    inv = ROPE_THETA ** (-jnp.arange(nfreq, dtype=F32) / nfreq)
    ar = row_id[:, None] * inv
    ac = col_id[:, None] * inv
    cos = jnp.concatenate([jnp.cos(ar), jnp.cos(ar), jnp.cos(ac), jnp.cos(ac)], axis=-1)
    sin = jnp.concatenate([-jnp.sin(ar), jnp.sin(ar), -jnp.sin(ac), jnp.sin(ac)], axis=-1)
    return cos, sin


def _tile(n, pref):
    t = min(n, pref)
    while n % t:
        t //= 2
    return t


def kernel(x, w_in, attn_sink, qk_norm_q, qk_norm_k, ret_decay_fwd, ret_decay_bwd, ret_norm_g,
           diff_lambda_q1, diff_lambda_k1, diff_lambda_q2, diff_lambda_k2, diff_subln_g,
           w_branch, w_out, ln_mix_g, ln_mix_b, router_group, router_group_b, router_expert,
           router_expert_b, expert_w_gate, expert_w_up, expert_w_down, ln_ffn_g, ln_ffn_b):
    batch, seq, d = x.shape
    depth = w_in.shape[0]
    n = batch * seq
    alpha = (2.0 * depth) ** 0.25
    assert w_in.shape[2] == MIX_COLS + N_BRANCH * d and seq % BLOCK == 0 and seq % GRID_W == 0

    colscale = jnp.ones((w_in.shape[2],), F32)
    colscale = colscale.at[COL_AQ:COL_AQ + BRANCH_W].set(HEAD_DIM ** -0.5 * LOG2E)
    colscale = colscale.at[COL_DQ:COL_DQ + BRANCH_W].set(D_HALF ** -0.5 * LOG2E)
    w_in_bf = (w_in * colscale).astype(BF16)
    w_branch_bf = w_branch.astype(BF16)
    w_out_bf = w_out.astype(BF16)
    wgu_bf = jnp.concatenate([expert_w_gate.astype(BF16), expert_w_up.astype(BF16)], axis=-1)
    wd_bf = expert_w_down.astype(BF16)
    ln_mix_g3, ln_mix_b3 = ln_mix_g[:, None, :], ln_mix_b[:, None, :]
    ln_ffn_g3, ln_ffn_b3 = ln_ffn_g[:, None, :], ln_ffn_b[:, None, :]

    cos_t, sin_t = _rope_tables(seq)
    slopes_log2 = jnp.asarray([LOG2E * 2.0 ** (-(i + 1.0)) for i in range(N_HEADS)], F32)

    tm_big = _tile(n, 1024)
    xf = x.reshape(n, d)
    xb = xf.astype(BF16)
    for l in range(depth):
        h = in_proj(xb, w_in_bf, l, MIX_COLS, tm_big, 1024)
        o_all = attn_a(h, attn_sink[l].astype(F32) * LOG2E, batch, seq)
        gains = jnp.concatenate([jnp.tile(qk_norm_q[l], N_HEADS) * (HEAD_DIM ** -0.5 * LOG2E),
                                 jnp.tile(qk_norm_k[l], KV_HEADS)])[None, :]
        qt_b, k_b = qk_prep(h, gains, cos_t, sin_t, batch, seq, _tile(seq, 512))
        vt_b = h[:, COL_BV:COL_BV + KV_HEADS * HEAD_DIM].T
        o_all = attn_b(qt_b, k_b, vt_b, o_all, batch, seq, _tile(seq, 256), _tile(seq // 2, 512))
        tabs_f = retention_tables(ret_decay_fwd[l], HEAD_DIM ** -0.5, False)
        tabs_b = retention_tables(ret_decay_bwd[l], HEAD_DIM ** -0.5, True)
        o_all = retention(h, tabs_f, tabs_b, ret_norm_g[l][None, :], o_all, batch, seq)
        lam_init = 0.8 - 0.6 * math.exp(-0.3 * l)
        lam_params = jnp.stack([diff_lambda_q1[l], diff_lambda_k1[l], diff_lambda_q2[l], diff_lambda_k2[l]]).astype(F32)
        qt_d = h[:, COL_DQ:COL_DQ + BRANCH_W].T
        vt_d = h[:, COL_DV:COL_DV + BRANCH_W].T
        o_all = attn_d(qt_d, h, vt_d, o_all, slopes_log2, lam_params, diff_subln_g[l][:, None], lam_init, batch, seq,
                       _tile(seq, 512), _tile(seq // 2, 512))
        merged = gate_merge(xb, w_in_bf, o_all, w_branch_bf, l, tm_big, 256)
        xf, xb = out_proj_ln(merged, w_out_bf, xf, ln_mix_g3, ln_mix_b3, l, alpha, _tile(n, 512), _tile(d, 512))
        w_r = jnp.concatenate([router_group[l], router_expert[l]], axis=1)
        w_r = jnp.pad(w_r, ((0, 0), (0, LANES - w_r.shape[1])))
        w_r_hi = w_r.astype(BF16)
        w_r_lo = (w_r - w_r_hi.astype(F32)).astype(BF16)
        b_r = jnp.pad(jnp.concatenate([router_group_b[l], router_expert_b[l]]), (0, LANES - N_GROUPS - N_EXPERTS))[None, :]
        route = router(xf, w_r_hi, w_r_lo, b_r, _tile(n, 512))
        y_pairs = moe_experts(xf, moe_routing(route, n, MOE_TM), wgu_bf, wd_bf, l, MOE_TM)
        xf, xb = pair_sum_ln(y_pairs, xf, ln_ffn_g3, ln_ffn_b3, l, alpha, _tile(n, 256))
    return xf.reshape(batch, seq, d)
```

```python
import functools
import math

import jax
import jax.numpy as jnp
from jax import lax
from jax.experimental import pallas as pl
from jax.experimental.pallas import tpu as pltpu

F32 = jnp.float32
BF16 = jnp.bfloat16

HEAD_DIM = 128
BLOCK = 128
GRID_W = 64
N_HEADS = 8
KV_HEADS = 2
D_HALF = HEAD_DIM // 2
N_BRANCH = 4
BRANCH_W = N_HEADS * HEAD_DIM
ROPE_THETA = 10000.0
N_GROUPS = 4
EXPERTS_PER_GROUP = 4
N_EXPERTS = N_GROUPS * EXPERTS_PER_GROUP
LN_EPS = 1e-5
RMS_EPS = 1e-6
NEG_BIG = -1e30
LOG2E = 1.4426950408889634
LANES = 128
VMEM_LIMIT = 56 * 1024 * 1024

COL_AQ, COL_AK, COL_AV = 0, 1024, 1280
COL_BQ, COL_BK, COL_BV = 1536, 2560, 2816
COL_CQ, COL_CK, COL_CV, COL_CG = 3072, 4096, 5120, 6144
COL_DQ, COL_DK, COL_DV = 7168, 8192, 9216
MIX_COLS = 10240


def _params(sem):
    return pltpu.CompilerParams(dimension_semantics=sem, vmem_limit_bytes=VMEM_LIMIT)


def _nt_dot(a, b):
    return lax.dot_general(a, b, (((1,), (1,)), ((), ())), preferred_element_type=F32)


def _tn_dot(a, b):
    return lax.dot_general(a, b, (((0,), (0,)), ((), ())), preferred_element_type=F32)


def _dot(a, b):
    return jnp.dot(a, b, preferred_element_type=F32)


def _mm_kernel(x_ref, w_ref, o_ref):
    o_ref[...] = _dot(x_ref[...], w_ref[...]).astype(o_ref.dtype)


def in_proj(x_bf, w_in_bf, layer, n_cols, tm, tn):
    n, k = x_bf.shape
    return pl.pallas_call(
        _mm_kernel,
        grid=(n // tm, n_cols // tn),
        in_specs=[pl.BlockSpec((tm, k), lambda i, j: (i, 0)),
                  pl.BlockSpec((None, k, tn), lambda i, j: (layer, 0, j))],
        out_specs=pl.BlockSpec((tm, tn), lambda i, j: (i, j)),
        out_shape=jax.ShapeDtypeStruct((n, n_cols), BF16),
        compiler_params=_params(("parallel", "parallel")),
        name="in_proj",
    )(x_bf, w_in_bf)


def _attn_a_kernel(sink_ref, q_ref, kp_ref, kc_ref, kn_ref, vp_ref, vc_ref, vn_ref, o_ref, *, slopes, nb):
    i = pl.program_id(1)
    rep = N_HEADS // KV_HEADS
    rows = rep * BLOCK
    row = lax.broadcasted_iota(jnp.int32, (rows, 3 * BLOCK), 0)
    col = lax.broadcasted_iota(jnp.int32, (rows, 3 * BLOCK), 1)
    rel = (row % BLOCK) - (col - BLOCK)
    dist = jnp.abs(rel)
    valid = (dist <= BLOCK) & ((col >= BLOCK) | (i > 0)) & ((col < 2 * BLOCK) | (i < nb - 1))
    distf = dist.astype(F32)
    rowc = lax.broadcasted_iota(jnp.int32, (rows, 1), 0)
    for g in range(KV_HEADS):
        q = jnp.concatenate([q_ref[:, (g * rep + r) * HEAD_DIM:(g * rep + r + 1) * HEAD_DIM]
                             for r in range(rep)], axis=0)
        ks = slice(g * HEAD_DIM, (g + 1) * HEAD_DIM)
        k = jnp.concatenate([kp_ref[:, ks], kc_ref[:, ks], kn_ref[:, ks]], axis=0)
        v = jnp.concatenate([vp_ref[:, ks], vc_ref[:, ks], vn_ref[:, ks]], axis=0)
        slope = jnp.full((rows, 1), slopes[g * rep], F32)
        sink = jnp.full((rows, 1), sink_ref[g * rep], F32)
        for r in range(1, rep):
            slope = jnp.where(rowc >= r * BLOCK, slopes[g * rep + r], slope)
            sink = jnp.where(rowc >= r * BLOCK, sink_ref[g * rep + r], sink)
        s = _nt_dot(q, k) - slope * distf
        s = jnp.where(valid, s, NEG_BIG)
        m = jnp.maximum(jnp.max(s, axis=-1, keepdims=True), sink)
        p = jnp.exp2(s - m)
        denom = jnp.sum(p, axis=-1, keepdims=True) + jnp.exp2(sink - m)
        o = _dot(p.astype(BF16), v) / denom
        for r in range(rep):
            hh = g * rep + r
            o_ref[:, hh * HEAD_DIM:(hh + 1) * HEAD_DIM] = o[r * BLOCK:(r + 1) * BLOCK].astype(o_ref.dtype)


def attn_a(h, sink_log2, batch, seq):
    n = h.shape[0]
    nb = seq // BLOCK
    slopes = tuple(LOG2E * 2.0 ** (-(i + 1.0)) for i in range(N_HEADS))
    kvw = KV_HEADS * HEAD_DIM
    kcol, vcol = COL_AK // kvw, COL_AV // kvw

    def prev(b, i):
        return b * nb + jnp.maximum(i - 1, 0)

    def nxt(b, i):
        return b * nb + jnp.minimum(i + 1, nb - 1)

    return pl.pallas_call(
        functools.partial(_attn_a_kernel, slopes=slopes, nb=nb),
        grid=(batch, nb),
        in_specs=[pl.BlockSpec(memory_space=pltpu.SMEM),
                  pl.BlockSpec((BLOCK, BRANCH_W), lambda b, i: (b * nb + i, COL_AQ // BRANCH_W)),
                  pl.BlockSpec((BLOCK, kvw), lambda b, i: (prev(b, i), kcol)),
                  pl.BlockSpec((BLOCK, kvw), lambda b, i: (b * nb + i, kcol)),
                  pl.BlockSpec((BLOCK, kvw), lambda b, i: (nxt(b, i), kcol)),
                  pl.BlockSpec((BLOCK, kvw), lambda b, i: (prev(b, i), vcol)),
                  pl.BlockSpec((BLOCK, kvw), lambda b, i: (b * nb + i, vcol)),
                  pl.BlockSpec((BLOCK, kvw), lambda b, i: (nxt(b, i), vcol))],
        out_specs=pl.BlockSpec((None, BLOCK, BRANCH_W), lambda b, i: (0, b * nb + i, 0)),
        out_shape=jax.ShapeDtypeStruct((N_BRANCH, n, BRANCH_W), BF16),
        compiler_params=_params(("parallel", "parallel")),
        name="attn_a",
    )(sink_log2, h, h, h, h, h, h, h)


def _norm_rope(x, g, cos, sin, first_quarter):
    y = x * lax.rsqrt(jnp.mean(x * x, axis=-1, keepdims=True) + RMS_EPS) * g
    partner = jnp.where(first_quarter, pltpu.roll(y, HEAD_DIM - HEAD_DIM // 4, 1), pltpu.roll(y, HEAD_DIM // 4, 1))
    return y * cos + partner * sin


def _qk_prep_kernel(q0_ref, q1_ref, k_ref, g_ref, cos_ref, sin_ref, qt_ref, ko_ref):
    lane = lax.broadcasted_iota(jnp.int32, (k_ref.shape[0], HEAD_DIM), 1)
    first_quarter = (lane % (HEAD_DIM // 2)) < (HEAD_DIM // 4)
    cos = cos_ref[...]
    sin = sin_ref[...]
    half = N_HEADS // 2
    for hh in range(N_HEADS):
        src = q0_ref if hh < half else q1_ref
        sl = slice((hh % half) * HEAD_DIM, (hh % half + 1) * HEAD_DIM)
        gs = slice(hh * HEAD_DIM, (hh + 1) * HEAD_DIM)
        y = _norm_rope(src[:, sl].astype(F32), g_ref[:, gs], cos, sin, first_quarter)
        qt_ref[gs, :] = y.T.astype(qt_ref.dtype)
    for hh in range(KV_HEADS):
        sl = slice(hh * HEAD_DIM, (hh + 1) * HEAD_DIM)
        gs = slice((N_HEADS + hh) * HEAD_DIM, (N_HEADS + hh + 1) * HEAD_DIM)
        ko_ref[:, sl] = _norm_rope(k_ref[:, sl].astype(F32), g_ref[:, gs], cos, sin, first_quarter).astype(ko_ref.dtype)


def qk_prep(h, gains, cos_t, sin_t, batch, seq, tm):
    n = h.shape[0]
    qw = BRANCH_W // 2
    kw = KV_HEADS * HEAD_DIM
    nblk = seq // tm
    return pl.pallas_call(
        _qk_prep_kernel,
        grid=(batch, nblk),
        in_specs=[pl.BlockSpec((tm, qw), lambda b, i: (b * nblk + i, COL_BQ // qw)),
                  pl.BlockSpec((tm, qw), lambda b, i: (b * nblk + i, COL_BQ // qw + 1)),
                  pl.BlockSpec((tm, kw), lambda b, i: (b * nblk + i, COL_BK // kw)),
                  pl.BlockSpec((1, BRANCH_W + kw), lambda b, i: (0, 0)),
                  pl.BlockSpec((tm, HEAD_DIM), lambda b, i: (i, 0)),
                  pl.BlockSpec((tm, HEAD_DIM), lambda b, i: (i, 0))],
        out_specs=[pl.BlockSpec((BRANCH_W, tm), lambda b, i: (0, b * nblk + i)),
                   pl.BlockSpec((tm, kw), lambda b, i: (b * nblk + i, 0))],
        out_shape=[jax.ShapeDtypeStruct((BRANCH_W, n), BF16), jax.ShapeDtypeStruct((n, kw), BF16)],
        compiler_params=_params(("parallel", "parallel")),
        name="qk_prep",
    )(h, h, h, gains, cos_t, sin_t)


ONES_ROWS = 16


FLASH_CHUNKS_PER_TRIP = 4


def _flash_pipeline(n_chunks, score_fn, vt_fn, bufs, m_ref, acc_ref):
    per_trip = FLASH_CHUNKS_PER_TRIP if n_chunks % FLASH_CHUNKS_PER_TRIP == 0 else 2
    assert n_chunks % per_trip == 0
    scores, accumulate = _flash_stages(m_ref, acc_ref)
    scores(score_fn(0), bufs[0])

    def body(i, carry):
        c = per_trip * i
        for j in range(per_trip):
            scores(score_fn(jnp.minimum(c + j + 1, n_chunks - 1)), bufs[(j + 1) % 2])
            accumulate(vt_fn(c + j), bufs[j % 2])
        return carry

    lax.fori_loop(0, n_chunks // per_trip, body, 0)


def _flash_init(m_ref, acc_ref):
    m_ref[...] = jnp.full(m_ref.shape, NEG_BIG, F32)
    acc_ref[...] = jnp.zeros(acc_ref.shape, F32)


def _flash_stages(m_ref, acc_ref):
    def scores(s, buf):
        s_ref, x_ref = buf
        s_ref[...] = s
        x_ref[...] = jnp.max(s, axis=0, keepdims=True)

    def accumulate(vt, buf):
        s_ref, x_ref = buf
        m_old = m_ref[...]
        m_new = jnp.maximum(m_old, x_ref[...])
        a = jnp.exp2(m_old - m_new)
        p = jnp.exp2(s_ref[...] - m_new).astype(BF16)
        vt1 = jnp.concatenate([vt, jnp.ones((ONES_ROWS, vt.shape[1]), vt.dtype)], axis=0)
        acc_ref[...] = a * acc_ref[...] + _dot(vt1, p)
        m_ref[...] = m_new

    return scores, accumulate


def _attn_b_kernel(qt_ref, k_ref, vt_ref, stack_ref, o_ref, m_ref, acc_ref, *bufs, tk):
    rep = qt_ref.shape[0] // HEAD_DIM
    tq = qt_ref.shape[1]
    seq = k_ref.shape[0]
    qt = jnp.concatenate([qt_ref[r * HEAD_DIM:(r + 1) * HEAD_DIM, :] for r in range(rep)], axis=1)

    def score_fn(c):
        return _dot(k_ref[pl.ds(pl.multiple_of(c * tk, tk), tk), :], qt)

    def vt_fn(c):
        return vt_ref[:, pl.ds(pl.multiple_of(c * tk, tk), tk)]

    _flash_init(m_ref, acc_ref)
    _flash_pipeline(seq // tk, score_fn, vt_fn, (bufs[:2], bufs[2:]), m_ref, acc_ref)
    o = acc_ref[:HEAD_DIM, :] / acc_ref[HEAD_DIM:HEAD_DIM + 1, :]
    for r in range(rep):
        o_ref[:, r * HEAD_DIM:(r + 1) * HEAD_DIM] = o[:, r * tq:(r + 1) * tq].T.astype(o_ref.dtype)


def _flash_scratch(tk, width):
    row = pltpu.VMEM((1, width), F32)
    return [row, pltpu.VMEM((HEAD_DIM + ONES_ROWS, width), F32),
            pltpu.VMEM((tk, width), F32), row,
            pltpu.VMEM((tk, width), F32), row]


def attn_b(qt, kb, vt, o_stack, batch, seq, tq, tk):
    n = kb.shape[0]
    rep = N_HEADS // KV_HEADS
    nq = seq // tq
    width = rep * tq
    return pl.pallas_call(
        functools.partial(_attn_b_kernel, tk=tk),
        grid=(batch, KV_HEADS, nq),
        in_specs=[pl.BlockSpec((rep * HEAD_DIM, tq), lambda b, g, i: (g, b * nq + i)),
                  pl.BlockSpec((seq, HEAD_DIM), lambda b, g, i: (b, g)),
                  pl.BlockSpec((HEAD_DIM, seq), lambda b, g, i: (g, b)),
                  pl.BlockSpec(memory_space=pl.ANY)],
        out_specs=pl.BlockSpec((None, tq, rep * HEAD_DIM), lambda b, g, i: (1, b * nq + i, g)),
        out_shape=jax.ShapeDtypeStruct(o_stack.shape, o_stack.dtype),
        input_output_aliases={3: 0},
        scratch_shapes=_flash_scratch(tk, width),
        compiler_params=_params(("parallel", "parallel", "parallel")),
        name="attn_b",
    )(qt, kb, vt, o_stack)


def _retention_kernel(*refs, final):
    if final:
        (q_ref, k_ref, v_ref, inner_ref, qdec_ref, kdec_ref, cdec_ref, prev_ref, gate_ref, ng_ref, stack_ref,
         o_ref, state_ref) = refs
    else:
        q_ref, k_ref, v_ref, inner_ref, qdec_ref, kdec_ref, cdec_ref, o_ref, state_ref = refs

    @pl.when(pl.program_id(1) == 0)
    def _():
        state_ref[...] = jnp.zeros(state_ref.shape, F32)

    n_sub = q_ref.shape[0] // BLOCK
    for sub in (reversed(range(n_sub)) if final else range(n_sub)):
        rows = slice(sub * BLOCK, (sub + 1) * BLOCK)
        for hh in range(N_HEADS):
            sl = slice(hh * HEAD_DIM, (hh + 1) * HEAD_DIM)
            q = q_ref[rows, sl]
            k = k_ref[rows, sl]
            v = v_ref[rows, sl]
            state = state_ref[hh]
            inner = _nt_dot(q, k) * inner_ref[hh]
            o = _dot(inner.astype(BF16), v) + _dot(q, state.astype(BF16)) * qdec_ref[hh]
            kd = (k.astype(F32) * kdec_ref[hh]).astype(BF16)
            state_ref[hh] = state * cdec_ref[hh] + _tn_dot(kd, v)
            if final:
                r = o + prev_ref[rows, sl]
                mu = jnp.mean(r, axis=-1, keepdims=True)
                d = r - mu
                var = jnp.mean(d * d, axis=-1, keepdims=True)
                rn = d * lax.rsqrt(var + LN_EPS) * ng_ref[:, sl]
                gate = gate_ref[rows, sl].astype(F32)
                silu = gate / (1.0 + jnp.exp(-gate))
                o_ref[rows, sl] = (silu * rn).astype(o_ref.dtype)
            else:
                o_ref[rows, sl] = o


RET_CHUNKS = 8


def retention(h, tabs_f, tabs_b, norm_g, o_stack, batch, seq):
    n = h.shape[0]
    rb = BLOCK * RET_CHUNKS if seq % (BLOCK * RET_CHUNKS) == 0 else BLOCK
    nb = seq // rb
    w = BRANCH_W
    tab_specs = [pl.BlockSpec((N_HEADS, BLOCK, BLOCK), lambda b, c: (0, 0, 0)),
                 pl.BlockSpec((N_HEADS, BLOCK, HEAD_DIM), lambda b, c: (0, 0, 0)),
                 pl.BlockSpec((N_HEADS, BLOCK, HEAD_DIM), lambda b, c: (0, 0, 0)),
                 pl.BlockSpec((N_HEADS, 1, HEAD_DIM), lambda b, c: (0, 0, 0))]

    def specs(rowmap):
        return [pl.BlockSpec((rb, w), lambda b, c: (rowmap(b, c), COL_CQ // w)),
                pl.BlockSpec((rb, w), lambda b, c: (rowmap(b, c), COL_CK // w)),
                pl.BlockSpec((rb, w), lambda b, c: (rowmap(b, c), COL_CV // w))]

    fmap = lambda b, c: b * nb + c
    bmap = lambda b, c: b * nb + (nb - 1 - c)
    scratch = [pltpu.VMEM((N_HEADS, HEAD_DIM, HEAD_DIM), F32)]
    o_f = pl.pallas_call(
        functools.partial(_retention_kernel, final=False),
        grid=(batch, nb),
        in_specs=specs(fmap) + tab_specs,
        out_specs=pl.BlockSpec((rb, w), lambda b, c: (fmap(b, c), 0)),
        out_shape=jax.ShapeDtypeStruct((n, w), F32),
        scratch_shapes=scratch,
        compiler_params=_params(("parallel", "arbitrary")),
        name="retention_fwd",
    )(h, h, h, *tabs_f)
    return pl.pallas_call(
        functools.partial(_retention_kernel, final=True),
        grid=(batch, nb),
        in_specs=specs(bmap) + tab_specs + [
            pl.BlockSpec((rb, w), lambda b, c: (bmap(b, c), 0)),
            pl.BlockSpec((rb, w), lambda b, c: (bmap(b, c), COL_CG // w)),
            pl.BlockSpec((1, w), lambda b, c: (0, 0)),
            pl.BlockSpec(memory_space=pl.ANY)],
        out_specs=pl.BlockSpec((None, rb, w), lambda b, c: (2, bmap(b, c), 0)),
        out_shape=jax.ShapeDtypeStruct(o_stack.shape, o_stack.dtype),
        input_output_aliases={10: 0},
        scratch_shapes=scratch,
        compiler_params=_params(("parallel", "arbitrary")),
        name="retention_bwd",
    )(h, h, h, *tabs_b, o_f, h, norm_g, o_stack)


def retention_tables(dec, scale, backward):
    lg = jnp.log1p(-jnp.exp(dec.astype(F32)))
    pos = jnp.arange(BLOCK, dtype=F32)
    diff = pos[:, None] - pos[None, :]
    if backward:
        diff = -diff
        mask = diff > 0
        qexp = BLOCK - pos
        kexp = pos
    else:
        mask = diff >= 0
        qexp = pos + 1.0
        kexp = BLOCK - 1.0 - pos
    inner = jnp.where(mask[None], jnp.exp(lg[:, None, None] * jnp.where(mask, diff, 0.0)[None]), 0.0) * scale
    qdec = jnp.broadcast_to(jnp.exp(lg[:, None] * qexp)[:, :, None], (N_HEADS, BLOCK, HEAD_DIM))
    kdec = jnp.broadcast_to((jnp.exp(lg[:, None] * kexp) * scale)[:, :, None], (N_HEADS, BLOCK, HEAD_DIM))
    cdec = jnp.broadcast_to(jnp.exp(lg * BLOCK)[:, None, None], (N_HEADS, 1, HEAD_DIM))
    return inner, qdec, kdec, cdec


def _attn_d_kernel(slope_ref, lam_ref, g_ref, q_ref, k_ref, vt_ref, stack_ref, o_ref, m_ref, acc_ref, *bufs,
                   tk, lam_init):
    tq = q_ref.shape[1]
    seq = k_ref.shape[0]
    slope = slope_ref[pl.program_id(1)]
    q0 = pl.program_id(2) * tq
    feat = lax.broadcasted_iota(jnp.int32, (HEAD_DIM, tq), 0)
    q = q_ref[...]
    zero = jnp.zeros_like(q)
    qt = jnp.concatenate([jnp.where(feat < D_HALF, q, zero), jnp.where(feat >= D_HALF, q, zero)], axis=1)
    rel = slope * (lax.broadcasted_iota(jnp.int32, (tk, tq), 0)
                   - lax.broadcasted_iota(jnp.int32, (tk, tq), 1)).astype(F32)

    def score_fn(c):
        off = pl.multiple_of(c * tk, tk)
        bias = jnp.abs(rel + slope * (off - q0).astype(F32))
        return _dot(k_ref[pl.ds(off, tk), :], qt) - jnp.concatenate([bias, bias], axis=1)

    def vt_fn(c):
        return vt_ref[:, pl.ds(pl.multiple_of(c * tk, tk), tk)]

    _flash_init(m_ref, acc_ref)
    _flash_pipeline(seq // tk, score_fn, vt_fn, (bufs[:2], bufs[2:]), m_ref, acc_ref)
    lp = lam_ref[...]
    lam = (jnp.exp(jnp.sum(lp[0:1] * lp[1:2], axis=-1, keepdims=True))
           - jnp.exp(jnp.sum(lp[2:3] * lp[3:4], axis=-1, keepdims=True)) + lam_init)
    on = acc_ref[:HEAD_DIM, :] / acc_ref[HEAD_DIM:HEAD_DIM + 1, :]
    o = on[:, :tq] - lam * on[:, tq:]
    y = o * lax.rsqrt(jnp.mean(o * o, axis=0, keepdims=True) + RMS_EPS) * g_ref[...]
    o_ref[...] = (y * (1.0 - lam_init)).T.astype(o_ref.dtype)


def attn_d(qt, h, vt, o_stack, slopes_log2, lam_params, subln_g, lam_init, batch, seq, tq, tk):
    n = h.shape[0]
    nq = seq // tq
    return pl.pallas_call(
        functools.partial(_attn_d_kernel, tk=tk, lam_init=lam_init),
        grid=(batch, N_HEADS, nq),
        in_specs=[pl.BlockSpec(memory_space=pltpu.SMEM),
                  pl.BlockSpec((4, D_HALF), lambda b, hh, i: (0, 0)),
                  pl.BlockSpec((HEAD_DIM, 1), lambda b, hh, i: (0, 0)),
                  pl.BlockSpec((HEAD_DIM, tq), lambda b, hh, i: (hh, b * nq + i)),
                  pl.BlockSpec((seq, HEAD_DIM), lambda b, hh, i: (b, COL_DK // HEAD_DIM + hh)),
                  pl.BlockSpec((HEAD_DIM, seq), lambda b, hh, i: (hh, b)),
                  pl.BlockSpec(memory_space=pl.ANY)],
        out_specs=pl.BlockSpec((None, tq, HEAD_DIM), lambda b, hh, i: (3, b * nq + i, hh)),
        out_shape=jax.ShapeDtypeStruct(o_stack.shape, o_stack.dtype),
        input_output_aliases={6: 0},
        scratch_shapes=_flash_scratch(tk, 2 * tq),
        compiler_params=_params(("parallel", "parallel", "parallel")),
        name="attn_d",
    )(slopes_log2, lam_params, subln_g, qt, h, vt, o_stack)


def _gate_merge_kernel(x_ref, *refs):
    wg_refs, o_refs, wb_refs = refs[:N_BRANCH], refs[N_BRANCH:2 * N_BRANCH], refs[2 * N_BRANCH:3 * N_BRANCH]
    out_ref = refs[3 * N_BRANCH]
    x = x_ref[...]
    acc = None
    for i in range(N_BRANCH):
        gate = _dot(x, wg_refs[i][...])
        contrib = _dot(o_refs[i][...], wb_refs[i][...]) / (1.0 + jnp.exp(-gate))
        acc = contrib if acc is None else acc + contrib
    out_ref[...] = acc.astype(out_ref.dtype)


def gate_merge(x_bf, w_in_bf, o_stack, w_branch_bf, layer, tm, tn):
    n, d = x_bf.shape
    gcol0 = MIX_COLS // tn
    per = d // tn
    once = dict(pipeline_mode=pl.Buffered(1))
    wg_specs = [pl.BlockSpec((None, d, tn), functools.partial(lambda m, j, i: (layer, 0, gcol0 + i * per + j), i=i))
                for i in range(N_BRANCH)]
    o_specs = [pl.BlockSpec((None, tm, BRANCH_W), functools.partial(lambda m, j, i: (i, m, 0), i=i), **once)
               for i in range(N_BRANCH)]
    wb_specs = [pl.BlockSpec((None, None, BRANCH_W, tn), functools.partial(lambda m, j, i: (layer, i, 0, j), i=i))
                for i in range(N_BRANCH)]
    return pl.pallas_call(
        _gate_merge_kernel,
        grid=(n // tm, d // tn),
        in_specs=[pl.BlockSpec((tm, d), lambda m, j: (m, 0))] + wg_specs + o_specs + wb_specs,
        out_specs=pl.BlockSpec((tm, tn), lambda m, j: (m, j)),
        out_shape=jax.ShapeDtypeStruct((n, d), BF16),
        compiler_params=_params(("parallel", "parallel")),
        name="gate_merge",
    )(x_bf, *([w_in_bf] * N_BRANCH), *([o_stack] * N_BRANCH), *([w_branch_bf] * N_BRANCH))


def _layer_norm(y, g, b):
    mu = jnp.mean(y, axis=-1, keepdims=True)
    d = y - mu
    var = jnp.mean(d * d, axis=-1, keepdims=True)
    return d * lax.rsqrt(var + LN_EPS) * g + b


def _out_ln_kernel(m_ref, w_ref, x_ref, g_ref, b_ref, of_ref, ob_ref, *, alpha):
    k = pl.program_id(1)

    @pl.when(k == 0)
    def _():
        of_ref[...] = jnp.zeros(of_ref.shape, F32)

    of_ref[...] += _dot(m_ref[...], w_ref[...])

    @pl.when(k == pl.num_programs(1) - 1)
    def _():
        rows = min(of_ref.shape[0], LN_ROWS)

        def body(r, c):
            sl = pl.ds(pl.multiple_of(r * rows, rows), rows)
            out = _layer_norm(alpha * x_ref[sl, :] + of_ref[sl, :], g_ref[...], b_ref[...])
            of_ref[sl, :] = out
            ob_ref[sl, :] = out.astype(ob_ref.dtype)
            return c

        lax.fori_loop(0, of_ref.shape[0] // rows, body, 0)


LN_ROWS = 128


def out_proj_ln(merged, w_out_bf, x, g, b, layer, alpha, tm, tk):
    n, d = x.shape
    return pl.pallas_call(
        functools.partial(_out_ln_kernel, alpha=alpha),
        grid=(n // tm, d // tk),
        in_specs=[pl.BlockSpec((tm, tk), lambda m, k: (m, k)),
                  pl.BlockSpec((None, tk, d), lambda m, k: (layer, k, 0)),
                  pl.BlockSpec((tm, d), lambda m, k: (m, 0), pipeline_mode=pl.Buffered(1)),
                  pl.BlockSpec((None, 1, d), lambda m, k: (layer, 0, 0)),
                  pl.BlockSpec((None, 1, d), lambda m, k: (layer, 0, 0))],
        out_specs=[pl.BlockSpec((tm, d), lambda m, k: (m, 0)), pl.BlockSpec((tm, d), lambda m, k: (m, 0))],
        out_shape=[jax.ShapeDtypeStruct((n, d), F32), jax.ShapeDtypeStruct((n, d), BF16)],
        compiler_params=_params(("parallel", "arbitrary")),
        name="out_proj_ln",
    )(merged, w_out_bf, x, g, b)


def _router_kernel(x_ref, whi_ref, wlo_ref, b_ref, o_ref):
    x = x_ref[...]
    x_hi = x.astype(BF16)
    x_lo = (x - x_hi.astype(F32)).astype(BF16)
    whi = whi_ref[...]
    logits = _dot(x_hi, whi) + _dot(x_lo, whi) + _dot(x_hi, wlo_ref[...]) + b_ref[...]
    lane = lax.broadcasted_iota(jnp.int32, logits.shape, 1)
    big = jnp.int32(LANES)
    gmask = lane < N_GROUPS
    gl = jnp.where(gmask, logits, NEG_BIG)
    gmax = jnp.max(gl, axis=-1, keepdims=True)
    g_top = jnp.min(jnp.where(gmask & (gl == gmax), lane, big), axis=-1, keepdims=True)
    g_w = 1.0 / jnp.sum(jnp.where(gmask, jnp.exp(gl - gmax), 0.0), axis=-1, keepdims=True)
    lo = N_GROUPS + g_top * EXPERTS_PER_GROUP
    emask = (lane >= lo) & (lane < lo + EXPERTS_PER_GROUP)
    e1 = jnp.where(emask, logits, NEG_BIG)
    v1 = jnp.max(e1, axis=-1, keepdims=True)
    i1 = jnp.min(jnp.where(emask & (e1 == v1), lane, big), axis=-1, keepdims=True)
    emask2 = emask & (lane != i1)
    e2 = jnp.where(emask2, logits, NEG_BIG)
    v2 = jnp.max(e2, axis=-1, keepdims=True)
    i2 = jnp.min(jnp.where(emask2 & (e2 == v2), lane, big), axis=-1, keepdims=True)
    t = jnp.exp(v2 - v1)
    w1 = g_w / (1.0 + t)
    w2 = g_w * t / (1.0 + t)
    o_ref[...] = (jnp.where(lane == 0, (i1 - N_GROUPS).astype(F32), 0.0)
                  + jnp.where(lane == 1, (i2 - N_GROUPS).astype(F32), 0.0)
                  + jnp.where(lane == 2, w1, 0.0) + jnp.where(lane == 3, w2, 0.0))


def router(x, w_hi, w_lo, bias, tm):
    n, d = x.shape
    return pl.pallas_call(
        _router_kernel,
        grid=(n // tm,),
        in_specs=[pl.BlockSpec((tm, d), lambda m: (m, 0)),
                  pl.BlockSpec((d, LANES), lambda m: (0, 0)),
                  pl.BlockSpec((d, LANES), lambda m: (0, 0)),
                  pl.BlockSpec((1, LANES), lambda m: (0, 0))],
        out_specs=pl.BlockSpec((tm, LANES), lambda m: (m, 0)),
        out_shape=jax.ShapeDtypeStruct((n, LANES), F32),
        compiler_params=_params(("parallel",)),
        name="router",
    )(x, w_hi, w_lo, bias)


MOE_TM = 256


def moe_routing(route, n_tok, tm):
    n_pairs = 2 * n_tok
    e = route[:, 0:2].astype(jnp.int32).reshape(n_pairs)
    w = route[:, 2:4].reshape(n_pairs)
    counts = jnp.sum((e[:, None] == jnp.arange(N_EXPERTS)[None, :]).astype(jnp.int32), axis=0)
    tiles_per = (counts + tm - 1) // tm
    tile_end = jnp.cumsum(tiles_per)
    n_tiles = tile_end[-1]
    t_max = n_pairs // tm + N_EXPERTS
    tile_expert = jnp.minimum(jnp.searchsorted(tile_end, jnp.arange(t_max), side="right"), N_EXPERTS - 1).astype(jnp.int32)
    pad_exp = jnp.repeat(jnp.arange(N_EXPERTS, dtype=jnp.int32), tm)
    pad_rank = jnp.tile(jnp.arange(tm, dtype=jnp.int32), N_EXPERTS)
    pad_key = jnp.where(pad_rank < (tiles_per * tm - counts)[pad_exp], 2 * pad_exp + 1, 2 * N_EXPERTS)
    keys = jnp.concatenate([2 * e, pad_key])
    ident = jnp.concatenate([jnp.arange(n_pairs, dtype=jnp.int32),
                             n_pairs + jnp.arange(N_EXPERTS * tm, dtype=jnp.int32)])
    _, ident, wslot = lax.sort((keys, ident, jnp.concatenate([w, jnp.zeros((N_EXPERTS * tm,), w.dtype)])),
                               num_keys=1, is_stable=True)
    valid = ident < n_pairs
    tok = jnp.where(valid, ident // 2, 0)
    dst = jnp.where(valid, (ident % 2) * n_tok + ident // 2, ident)
    return (tile_expert, n_tiles.astype(jnp.int32)[None], tok.astype(jnp.int32), dst.astype(jnp.int32),
            jnp.broadcast_to(wslot[:, None], (t_max * tm, LANES)))


def _moe_kernel(te_ref, nt_ref, tok_ref, dst_ref, x_hbm, w_ref, wgu_ref, wd_ref, out_hbm,
                xbuf, ybuf, gsem, ssem):
    t = pl.program_id(0)
    nt = nt_ref[0]
    tm = xbuf.shape[1]
    f = wd_ref.shape[0]

    def gather_row(tile, r, slot):
        tok = tok_ref[tile * tm + r]
        return pltpu.make_async_copy(x_hbm.at[pl.ds(tok, 1), :], xbuf.at[slot, pl.ds(r, 1), :], gsem.at[slot])

    def scatter_row(tile, r, slot):
        d = dst_ref[tile * tm + r]
        return pltpu.make_async_copy(ybuf.at[slot, pl.ds(r, 1), :], out_hbm.at[pl.ds(d, 1), :], ssem.at[slot])

    def for_rows(fn):
        def body(r, c):
            fn(r)
            return c
        lax.fori_loop(0, tm, body, 0, unroll=8)

    @pl.when((t == 0) & (nt > 0))
    def _():
        for_rows(lambda r: gather_row(0, r, 0).start())

    @pl.when(t + 1 < nt)
    def _():
        for_rows(lambda r: gather_row(t + 1, r, (t + 1) % 2).start())

    @pl.when(t < nt)
    def _():
        slot = t % 2
        for_rows(lambda r: gather_row(t, r, slot).wait())
        xb = xbuf[slot].astype(BF16)
        gu = _dot(xb, wgu_ref[...])
        gate = gu[:, :f]
        w = w_ref[...]
        hmid = (gate / (1.0 + jnp.exp(-gate))) * gu[:, f:] * jnp.concatenate([w] * (f // LANES), axis=1)
        y = _dot(hmid.astype(BF16), wd_ref[...])

        @pl.when(t >= 2)
        def _():
            for_rows(lambda r: scatter_row(t - 2, r, slot).wait())

        ybuf[slot] = y
        for_rows(lambda r: scatter_row(t, r, slot).start())

    @pl.when(t == pl.num_programs(0) - 1)
    def _():
        for back in (1, 2):
            @pl.when(nt >= back)
            def _():
                for_rows(lambda r: scatter_row(nt - back, r, (nt - back) % 2).wait())


def moe_experts(x, routing, wgu_bf, wd_bf, layer, tm):
    n, d = x.shape
    f = wd_bf.shape[-2]
    tile_expert, n_tiles, tok, dst, wslot = routing
    t_max = tile_expert.shape[0]
    grid_spec = pltpu.PrefetchScalarGridSpec(
        num_scalar_prefetch=4,
        grid=(t_max,),
        in_specs=[pl.BlockSpec(memory_space=pl.ANY),
                  pl.BlockSpec((tm, LANES), lambda t, te, nt, tk, ds: (t, 0)),
                  pl.BlockSpec((None, None, d, 2 * f), lambda t, te, nt, tk, ds: (layer, te[t], 0, 0)),
                  pl.BlockSpec((None, None, f, d), lambda t, te, nt, tk, ds: (layer, te[t], 0, 0))],
        out_specs=pl.BlockSpec(memory_space=pl.ANY),
        scratch_shapes=[pltpu.VMEM((2, tm, d), F32), pltpu.VMEM((2, tm, d), F32),
                        pltpu.SemaphoreType.DMA((2,)), pltpu.SemaphoreType.DMA((2,))])
    return pl.pallas_call(
        _moe_kernel,
        grid_spec=grid_spec,
        out_shape=jax.ShapeDtypeStruct((t_max * tm, d), F32),
        compiler_params=_params(("arbitrary",)),
        name="moe_experts",
    )(tile_expert, n_tiles, tok, dst, x, wslot, wgu_bf, wd_bf)


def _pair_ln_kernel(y0_ref, y1_ref, x_ref, g_ref, b_ref, of_ref, ob_ref, *, alpha):
    out = _layer_norm(alpha * x_ref[...] + (y0_ref[...] + y1_ref[...]), g_ref[...], b_ref[...])
    of_ref[...] = out
    ob_ref[...] = out.astype(ob_ref.dtype)


def pair_sum_ln(y_pairs, x, g, b, layer, alpha, tm):
    n, d = x.shape
    nblk = n // tm
    return pl.pallas_call(
        functools.partial(_pair_ln_kernel, alpha=alpha),
        grid=(nblk,),
        in_specs=[pl.BlockSpec((tm, d), lambda m: (m, 0)),
                  pl.BlockSpec((tm, d), lambda m: (nblk + m, 0)),
                  pl.BlockSpec((tm, d), lambda m: (m, 0)),
                  pl.BlockSpec((None, 1, d), lambda m: (layer, 0, 0)),
                  pl.BlockSpec((None, 1, d), lambda m: (layer, 0, 0))],
        out_specs=[pl.BlockSpec((tm, d), lambda m: (m, 0)), pl.BlockSpec((tm, d), lambda m: (m, 0))],
        out_shape=[jax.ShapeDtypeStruct((n, d), F32), jax.ShapeDtypeStruct((n, d), BF16)],
        compiler_params=_params(("parallel",)),
        name="pair_sum_ln",
    )(y_pairs, y_pairs, x, g, b)


def _rope_tables(seq):
    t = jnp.arange(seq)
    row_id = (t // GRID_W).astype(F32)
    col_id = (t % GRID_W).astype(F32)
    nfreq = HEAD_DIM // 4
    inv = ROPE_THETA ** (-jnp.arange(nfreq, dtype=F32) / nfreq)
    ar = row_id[:, None] * inv
    ac = col_id[:, None] * inv
    cos = jnp.concatenate([jnp.cos(ar), jnp.cos(ar), jnp.cos(ac), jnp.cos(ac)], axis=-1)
    sin = jnp.concatenate([-jnp.sin(ar), jnp.sin(ar), -jnp.sin(ac), jnp.sin(ac)], axis=-1)
    return cos, sin


def _tile(n, pref):
    t = min(n, pref)
    while n % t:
        t //= 2
    return t


def kernel(x, w_in, attn_sink, qk_norm_q, qk_norm_k, ret_decay_fwd, ret_decay_bwd, ret_norm_g,
           diff_lambda_q1, diff_lambda_k1, diff_lambda_q2, diff_lambda_k2, diff_subln_g,
           w_branch, w_out, ln_mix_g, ln_mix_b, router_group, router_group_b, router_expert,
           router_expert_b, expert_w_gate, expert_w_up, expert_w_down, ln_ffn_g, ln_ffn_b):
    batch, seq, d = x.shape
    depth = w_in.shape[0]
    n = batch * seq
    alpha = (2.0 * depth) ** 0.25
    assert w_in.shape[2] == MIX_COLS + N_BRANCH * d and seq % BLOCK == 0 and seq % GRID_W == 0

    colscale = jnp.ones((w_in.shape[2],), F32)
    colscale = colscale.at[COL_AQ:COL_AQ + BRANCH_W].set(HEAD_DIM ** -0.5 * LOG2E)
    colscale = colscale.at[COL_DQ:COL_DQ + BRANCH_W].set(D_HALF ** -0.5 * LOG2E)
    w_in_bf = (w_in * colscale).astype(BF16)
    w_branch_bf = w_branch.astype(BF16)
    w_out_bf = w_out.astype(BF16)
    wgu_bf = jnp.concatenate([expert_w_gate.astype(BF16), expert_w_up.astype(BF16)], axis=-1)
    wd_bf = expert_w_down.astype(BF16)
    ln_mix_g3, ln_mix_b3 = ln_mix_g[:, None, :], ln_mix_b[:, None, :]
    ln_ffn_g3, ln_ffn_b3 = ln_ffn_g[:, None, :], ln_ffn_b[:, None, :]

    cos_t, sin_t = _rope_tables(seq)
    slopes_log2 = jnp.asarray([LOG2E * 2.0 ** (-(i + 1.0)) for i in range(N_HEADS)], F32)

    tm_big = _tile(n, 1024)
    xf = x.reshape(n, d)
    xb = xf.astype(BF16)
    for l in range(depth):
        h = in_proj(xb, w_in_bf, l, MIX_COLS, tm_big, 1024)
        o_all = attn_a(h, attn_sink[l].astype(F32) * LOG2E, batch, seq)
        gains = jnp.concatenate([jnp.tile(qk_norm_q[l], N_HEADS) * (HEAD_DIM ** -0.5 * LOG2E),
                                 jnp.tile(qk_norm_k[l], KV_HEADS)])[None, :]
        qt_b, k_b = qk_prep(h, gains, cos_t, sin_t, batch, seq, _tile(seq, 512))
        vt_b = h[:, COL_BV:COL_BV + KV_HEADS * HEAD_DIM].T
        o_all = attn_b(qt_b, k_b, vt_b, o_all, batch, seq, _tile(seq, 256), _tile(seq // 2, 512))
        tabs_f = retention_tables(ret_decay_fwd[l], HEAD_DIM ** -0.5, False)
        tabs_b = retention_tables(ret_decay_bwd[l], HEAD_DIM ** -0.5, True)
        o_all = retention(h, tabs_f, tabs_b, ret_norm_g[l][None, :], o_all, batch, seq)
        lam_init = 0.8 - 0.6 * math.exp(-0.3 * l)
        lam_params = jnp.stack([diff_lambda_q1[l], diff_lambda_k1[l], diff_lambda_q2[l], diff_lambda_k2[l]]).astype(F32)
        qt_d = h[:, COL_DQ:COL_DQ + BRANCH_W].T
        vt_d = h[:, COL_DV:COL_DV + BRANCH_W].T
        o_all = attn_d(qt_d, h, vt_d, o_all, slopes_log2, lam_params, diff_subln_g[l][:, None], lam_init, batch, seq,
                       _tile(seq, 512), _tile(seq // 2, 512))
        merged = gate_merge(xb, w_in_bf, o_all, w_branch_bf, l, tm_big, 256)
        xf, xb = out_proj_ln(merged, w_out_bf, xf, ln_mix_g3, ln_mix_b3, l, alpha, _tile(n, 512), _tile(d, 512))
        w_r = jnp.concatenate([router_group[l], router_expert[l]], axis=1)
        w_r = jnp.pad(w_r, ((0, 0), (0, LANES - w_r.shape[1])))
        w_r_hi = w_r.astype(BF16)
        w_r_lo = (w_r - w_r_hi.astype(F32)).astype(BF16)
        b_r = jnp.pad(jnp.concatenate([router_group_b[l], router_expert_b[l]]), (0, LANES - N_GROUPS - N_EXPERTS))[None, :]
        route = router(xf, w_r_hi, w_r_lo, b_r, _tile(n, 512))
        y_pairs = moe_experts(xf, moe_routing(route, n, MOE_TM), wgu_bf, wd_bf, l, MOE_TM)
        xf, xb = pair_sum_ln(y_pairs, xf, ln_ffn_g3, ln_ffn_b3, l, alpha, _tile(n, 256))
    return xf.reshape(batch, seq, d)
```

```python
import functools
import math

import jax
import jax.numpy as jnp
from jax import lax
from jax.experimental import pallas as pl
from jax.experimental.pallas import tpu as pltpu

F32 = jnp.float32
BF16 = jnp.bfloat16

HEAD_DIM = 128
BLOCK = 128
GRID_W = 64
N_HEADS = 8
KV_HEADS = 2
D_HALF = HEAD_DIM // 2
N_BRANCH = 4
BRANCH_W = N_HEADS * HEAD_DIM
ROPE_THETA = 10000.0
N_GROUPS = 4
EXPERTS_PER_GROUP = 4
N_EXPERTS = N_GROUPS * EXPERTS_PER_GROUP
LN_EPS = 1e-5
RMS_EPS = 1e-6
NEG_BIG = -1e30
LOG2E = 1.4426950408889634
LANES = 128
VMEM_LIMIT = 56 * 1024 * 1024

COL_AQ, COL_AK, COL_AV = 0, 1024, 1280
COL_BQ, COL_BK, COL_BV = 1536, 2560, 2816
COL_CQ, COL_CK, COL_CV, COL_CG = 3072, 4096, 5120, 6144
COL_DQ, COL_DK, COL_DV = 7168, 8192, 9216
MIX_COLS = 10240


def _params(sem):
    return pltpu.CompilerParams(dimension_semantics=sem, vmem_limit_bytes=VMEM_LIMIT)


def _nt_dot(a, b):
    return lax.dot_general(a, b, (((1,), (1,)), ((), ())), preferred_element_type=F32)


def _tn_dot(a, b):
    return lax.dot_general(a, b, (((0,), (0,)), ((), ())), preferred_element_type=F32)


def _dot(a, b):
    return jnp.dot(a, b, preferred_element_type=F32)


def _mm_kernel(x_ref, w_ref, o_ref):
    o_ref[...] = _dot(x_ref[...], w_ref[...]).astype(o_ref.dtype)


def in_proj(x_bf, w_in_bf, layer, n_cols, tm, tn):
    n, k = x_bf.shape
    return pl.pallas_call(
        _mm_kernel,
        grid=(n // tm, n_cols // tn),
        in_specs=[pl.BlockSpec((tm, k), lambda i, j: (i, 0)),
                  pl.BlockSpec((None, k, tn), lambda i, j: (layer, 0, j))],
        out_specs=pl.BlockSpec((tm, tn), lambda i, j: (i, j)),
        out_shape=jax.ShapeDtypeStruct((n, n_cols), BF16),
        compiler_params=_params(("parallel", "parallel")),
        name="in_proj",
    )(x_bf, w_in_bf)


def _attn_a_kernel(sink_ref, q_ref, kp_ref, kc_ref, kn_ref, vp_ref, vc_ref, vn_ref, o_ref, *, slopes, nb):
    i = pl.program_id(1)
    rep = N_HEADS // KV_HEADS
    rows = rep * BLOCK
    row = lax.broadcasted_iota(jnp.int32, (rows, 3 * BLOCK), 0)
    col = lax.broadcasted_iota(jnp.int32, (rows, 3 * BLOCK), 1)
    rel = (row % BLOCK) - (col - BLOCK)
    dist = jnp.abs(rel)
    valid = (dist <= BLOCK) & ((col >= BLOCK) | (i > 0)) & ((col < 2 * BLOCK) | (i < nb - 1))
    distf = dist.astype(F32)
    rowc = lax.broadcasted_iota(jnp.int32, (rows, 1), 0)
    for g in range(KV_HEADS):
        q = jnp.concatenate([q_ref[:, (g * rep + r) * HEAD_DIM:(g * rep + r + 1) * HEAD_DIM]
                             for r in range(rep)], axis=0)
        ks = slice(g * HEAD_DIM, (g + 1) * HEAD_DIM)
        k = jnp.concatenate([kp_ref[:, ks], kc_ref[:, ks], kn_ref[:, ks]], axis=0)
        v = jnp.concatenate([vp_ref[:, ks], vc_ref[:, ks], vn_ref[:, ks]], axis=0)
        slope = jnp.full((rows, 1), slopes[g * rep], F32)
        sink = jnp.full((rows, 1), sink_ref[g * rep], F32)
        for r in range(1, rep):
            slope = jnp.where(rowc >= r * BLOCK, slopes[g * rep + r], slope)
            sink = jnp.where(rowc >= r * BLOCK, sink_ref[g * rep + r], sink)
        s = _nt_dot(q, k) - slope * distf
        s = jnp.where(valid, s, NEG_BIG)
        m = jnp.maximum(jnp.max(s, axis=-1, keepdims=True), sink)
        p = jnp.exp2(s - m)
        denom = jnp.sum(p, axis=-1, keepdims=True) + jnp.exp2(sink - m)
        o = _dot(p.astype(BF16), v) / denom
        for r in range(rep):
            hh = g * rep + r
            o_ref[:, hh * HEAD_DIM:(hh + 1) * HEAD_DIM] = o[r * BLOCK:(r + 1) * BLOCK].astype(o_ref.dtype)


def attn_a(h, sink_log2, batch, seq):
    n = h.shape[0]
    nb = seq // BLOCK
    slopes = tuple(LOG2E * 2.0 ** (-(i + 1.0)) for i in range(N_HEADS))
    kvw = KV_HEADS * HEAD_DIM
    kcol, vcol = COL_AK // kvw, COL_AV // kvw

    def prev(b, i):
        return b * nb + jnp.maximum(i - 1, 0)

    def nxt(b, i):
        return b * nb + jnp.minimum(i + 1, nb - 1)

    return pl.pallas_call(
        functools.partial(_attn_a_kernel, slopes=slopes, nb=nb),
        grid=(batch, nb),
        in_specs=[pl.BlockSpec(memory_space=pltpu.SMEM),
                  pl.BlockSpec((BLOCK, BRANCH_W), lambda b, i: (b * nb + i, COL_AQ // BRANCH_W)),
                  pl.BlockSpec((BLOCK, kvw), lambda b, i: (prev(b, i), kcol)),
                  pl.BlockSpec((BLOCK, kvw), lambda b, i: (b * nb + i, kcol)),
                  pl.BlockSpec((BLOCK, kvw), lambda b, i: (nxt(b, i), kcol)),
                  pl.BlockSpec((BLOCK, kvw), lambda b, i: (prev(b, i), vcol)),
                  pl.BlockSpec((BLOCK, kvw), lambda b, i: (b * nb + i, vcol)),
                  pl.BlockSpec((BLOCK, kvw), lambda b, i: (nxt(b, i), vcol))],
        out_specs=pl.BlockSpec((None, BLOCK, BRANCH_W), lambda b, i: (0, b * nb + i, 0)),
        out_shape=jax.ShapeDtypeStruct((N_BRANCH, n, BRANCH_W), BF16),
        compiler_params=_params(("parallel", "parallel")),
        name="attn_a",
    )(sink_log2, h, h, h, h, h, h, h)


def _norm_rope(x, g, cos, sin, first_quarter):
    y = x * lax.rsqrt(jnp.mean(x * x, axis=-1, keepdims=True) + RMS_EPS) * g
    partner = jnp.where(first_quarter, pltpu.roll(y, HEAD_DIM - HEAD_DIM // 4, 1), pltpu.roll(y, HEAD_DIM // 4, 1))
    return y * cos + partner * sin


def _qk_prep_kernel(q0_ref, q1_ref, k_ref, g_ref, cos_ref, sin_ref, qt_ref, ko_ref):
    lane = lax.broadcasted_iota(jnp.int32, (k_ref.shape[0], HEAD_DIM), 1)
    first_quarter = (lane % (HEAD_DIM // 2)) < (HEAD_DIM // 4)
    cos = cos_ref[...]
    sin = sin_ref[...]
    half = N_HEADS // 2
    for hh in range(N_HEADS):
        src = q0_ref if hh < half else q1_ref
        sl = slice((hh % half) * HEAD_DIM, (hh % half + 1) * HEAD_DIM)
        gs = slice(hh * HEAD_DIM, (hh + 1) * HEAD_DIM)
        y = _norm_rope(src[:, sl].astype(F32), g_ref[:, gs], cos, sin, first_quarter)
        qt_ref[gs, :] = y.T.astype(qt_ref.dtype)
    for hh in range(KV_HEADS):
        sl = slice(hh * HEAD_DIM, (hh + 1) * HEAD_DIM)
        gs = slice((N_HEADS + hh) * HEAD_DIM, (N_HEADS + hh + 1) * HEAD_DIM)
        ko_ref[:, sl] = _norm_rope(k_ref[:, sl].astype(F32), g_ref[:, gs], cos, sin, first_quarter).astype(ko_ref.dtype)


def qk_prep(h, gains, cos_t, sin_t, batch, seq, tm):
    n = h.shape[0]
    qw = BRANCH_W // 2
    kw = KV_HEADS * HEAD_DIM
    nblk = seq // tm
    return pl.pallas_call(
        _qk_prep_kernel,
        grid=(batch, nblk),
        in_specs=[pl.BlockSpec((tm, qw), lambda b, i: (b * nblk + i, COL_BQ // qw)),
                  pl.BlockSpec((tm, qw), lambda b, i: (b * nblk + i, COL_BQ // qw + 1)),
                  pl.BlockSpec((tm, kw), lambda b, i: (b * nblk + i, COL_BK // kw)),
                  pl.BlockSpec((1, BRANCH_W + kw), lambda b, i: (0, 0)),
                  pl.BlockSpec((tm, HEAD_DIM), lambda b, i: (i, 0)),
                  pl.BlockSpec((tm, HEAD_DIM), lambda b, i: (i, 0))],
        out_specs=[pl.BlockSpec((BRANCH_W, tm), lambda b, i: (0, b * nblk + i)),
                   pl.BlockSpec((tm, kw), lambda b, i: (b * nblk + i, 0))],
        out_shape=[jax.ShapeDtypeStruct((BRANCH_W, n), BF16), jax.ShapeDtypeStruct((n, kw), BF16)],
        compiler_params=_params(("parallel", "parallel")),
        name="qk_prep",
    )(h, h, h, gains, cos_t, sin_t)


ONES_ROWS = 16


FLASH_CHUNKS_PER_TRIP = 4


def _flash_pipeline(n_chunks, score_fn, vt_fn, bufs, m_ref, acc_ref):
    per_trip = FLASH_CHUNKS_PER_TRIP if n_chunks % FLASH_CHUNKS_PER_TRIP == 0 else 2
    assert n_chunks % per_trip == 0
    scores, accumulate = _flash_stages(m_ref, acc_ref)
    scores(score_fn(0), bufs[0])

    def body(i, carry):
        c = per_trip * i
        for j in range(per_trip):
            scores(score_fn(jnp.minimum(c + j + 1, n_chunks - 1)), bufs[(j + 1) % 2])
            accumulate(vt_fn(c + j), bufs[j % 2])
        return carry

    lax.fori_loop(0, n_chunks // per_trip, body, 0)


def _flash_init(m_ref, acc_ref):
    m_ref[...] = jnp.full(m_ref.shape, NEG_BIG, F32)
    acc_ref[...] = jnp.zeros(acc_ref.shape, F32)


def _flash_stages(m_ref, acc_ref):
    def scores(s, buf):
        s_ref, x_ref = buf
        s_ref[...] = s
        x_ref[...] = jnp.max(s, axis=0, keepdims=True)

    def accumulate(vt, buf):
        s_ref, x_ref = buf
        m_old = m_ref[...]
        m_new = jnp.maximum(m_old, x_ref[...])
        a = jnp.exp2(m_old - m_new)
        p = jnp.exp2(s_ref[...] - m_new).astype(BF16)
        vt1 = jnp.concatenate([vt, jnp.ones((ONES_ROWS, vt.shape[1]), vt.dtype)], axis=0)
        acc_ref[...] = a * acc_ref[...] + _dot(vt1, p)
        m_ref[...] = m_new

    return scores, accumulate


def _attn_b_kernel(qt_ref, k_ref, vt_ref, stack_ref, o_ref, m_ref, acc_ref, *bufs, tk):
    rep = qt_ref.shape[0] // HEAD_DIM
    tq = qt_ref.shape[1]
    seq = k_ref.shape[0]
    qt = jnp.concatenate([qt_ref[r * HEAD_DIM:(r + 1) * HEAD_DIM, :] for r in range(rep)], axis=1)

    def score_fn(c):
        return _dot(k_ref[pl.ds(pl.multiple_of(c * tk, tk), tk), :], qt)

    def vt_fn(c):
        return vt_ref[:, pl.ds(pl.multiple_of(c * tk, tk), tk)]

    _flash_init(m_ref, acc_ref)
    _flash_pipeline(seq // tk, score_fn, vt_fn, (bufs[:2], bufs[2:]), m_ref, acc_ref)
    o = acc_ref[:HEAD_DIM, :] / acc_ref[HEAD_DIM:HEAD_DIM + 1, :]
    for r in range(rep):
        o_ref[:, r * HEAD_DIM:(r + 1) * HEAD_DIM] = o[:, r * tq:(r + 1) * tq].T.astype(o_ref.dtype)


def _flash_scratch(tk, width):
    row = pltpu.VMEM((1, width), F32)
    return [row, pltpu.VMEM((HEAD_DIM + ONES_ROWS, width), F32),
            pltpu.VMEM((tk, width), F32), row,
            pltpu.VMEM((tk, width), F32), row]


def attn_b(qt, kb, vt, o_stack, batch, seq, tq, tk):
    n = kb.shape[0]
    rep = N_HEADS // KV_HEADS
    nq = seq // tq
    width = rep * tq
    return pl.pallas_call(
        functools.partial(_attn_b_kernel, tk=tk),
        grid=(batch, KV_HEADS, nq),
        in_specs=[pl.BlockSpec((rep * HEAD_DIM, tq), lambda b, g, i: (g, b * nq + i)),
                  pl.BlockSpec((seq, HEAD_DIM), lambda b, g, i: (b, g)),
                  pl.BlockSpec((HEAD_DIM, seq), lambda b, g, i: (g, b)),
                  pl.BlockSpec(memory_space=pl.ANY)],
        out_specs=pl.BlockSpec((None, tq, rep * HEAD_DIM), lambda b, g, i: (1, b * nq + i, g)),
        out_shape=jax.ShapeDtypeStruct(o_stack.shape, o_stack.dtype),
        input_output_aliases={3: 0},
        scratch_shapes=_flash_scratch(tk, width),
        compiler_params=_params(("parallel", "parallel", "parallel")),
        name="attn_b",
    )(qt, kb, vt, o_stack)


def _retention_kernel(*refs, final):
    if final:
        (q_ref, k_ref, v_ref, inner_ref, qdec_ref, kdec_ref, cdec_ref, prev_ref, gate_ref, ng_ref, stack_ref,
         o_ref, state_ref) = refs
    else:
        q_ref, k_ref, v_ref, inner_ref, qdec_ref, kdec_ref, cdec_ref, o_ref, state_ref = refs

    @pl.when(pl.program_id(1) == 0)
    def _():
        state_ref[...] = jnp.zeros(state_ref.shape, F32)

    n_sub = q_ref.shape[0] // BLOCK
    for sub in (reversed(range(n_sub)) if final else range(n_sub)):
        rows = slice(sub * BLOCK, (sub + 1) * BLOCK)
        for hh in range(N_HEADS):
            sl = slice(hh * HEAD_DIM, (hh + 1) * HEAD_DIM)
            q = q_ref[rows, sl]
            k = k_ref[rows, sl]
            v = v_ref[rows, sl]
            state = state_ref[hh]
            inner = _nt_dot(q, k) * inner_ref[hh]
            o = _dot(inner.astype(BF16), v) + _dot(q, state.astype(BF16)) * qdec_ref[hh]
            kd = (k.astype(F32) * kdec_ref[hh]).astype(BF16)
            state_ref[hh] = state * cdec_ref[hh] + _tn_dot(kd, v)
            if final:
                r = o + prev_ref[rows, sl]
                mu = jnp.mean(r, axis=-1, keepdims=True)
                d = r - mu
                var = jnp.mean(d * d, axis=-1, keepdims=True)
                rn = d * lax.rsqrt(var + LN_EPS) * ng_ref[:, sl]
                gate = gate_ref[rows, sl].astype(F32)
                silu = gate / (1.0 + jnp.exp(-gate))
                o_ref[rows, sl] = (silu * rn).astype(o_ref.dtype)
            else:
                o_ref[rows, sl] = o


RET_CHUNKS = 8


def retention(h, tabs_f, tabs_b, norm_g, o_stack, batch, seq):
    n = h.shape[0]
    rb = BLOCK * RET_CHUNKS if seq % (BLOCK * RET_CHUNKS) == 0 else BLOCK
    nb = seq // rb
    w = BRANCH_W
    tab_specs = [pl.BlockSpec((N_HEADS, BLOCK, BLOCK), lambda b, c: (0, 0, 0)),
                 pl.BlockSpec((N_HEADS, BLOCK, HEAD_DIM), lambda b, c: (0, 0, 0)),
                 pl.BlockSpec((N_HEADS, BLOCK, HEAD_DIM), lambda b, c: (0, 0, 0)),
                 pl.BlockSpec((N_HEADS, 1, HEAD_DIM), lambda b, c: (0, 0, 0))]

    def specs(rowmap):
        return [pl.BlockSpec((rb, w), lambda b, c: (rowmap(b, c), COL_CQ // w)),
                pl.BlockSpec((rb, w), lambda b, c: (rowmap(b, c), COL_CK // w)),
                pl.BlockSpec((rb, w), lambda b, c: (rowmap(b, c), COL_CV // w))]

    fmap = lambda b, c: b * nb + c
    bmap = lambda b, c: b * nb + (nb - 1 - c)
    scratch = [pltpu.VMEM((N_HEADS, HEAD_DIM, HEAD_DIM), F32)]
    o_f = pl.pallas_call(
        functools.partial(_retention_kernel, final=False),
        grid=(batch, nb),
        in_specs=specs(fmap) + tab_specs,
        out_specs=pl.BlockSpec((rb, w), lambda b, c: (fmap(b, c), 0)),
        out_shape=jax.ShapeDtypeStruct((n, w), F32),
        scratch_shapes=scratch,
        compiler_params=_params(("parallel", "arbitrary")),
        name="retention_fwd",
    )(h, h, h, *tabs_f)
    return pl.pallas_call(
        functools.partial(_retention_kernel, final=True),
        grid=(batch, nb),
        in_specs=specs(bmap) + tab_specs + [
            pl.BlockSpec((rb, w), lambda b, c: (bmap(b, c), 0)),
            pl.BlockSpec((rb, w), lambda b, c: (bmap(b, c), COL_CG // w)),
            pl.BlockSpec((1, w), lambda b, c: (0, 0)),
            pl.BlockSpec(memory_space=pl.ANY)],
        out_specs=pl.BlockSpec((None, rb, w), lambda b, c: (2, bmap(b, c), 0)),
        out_shape=jax.ShapeDtypeStruct(o_stack.shape, o_stack.dtype),
        input_output_aliases={10: 0},
        scratch_shapes=scratch,
        compiler_params=_params(("parallel", "arbitrary")),
        name="retention_bwd",
    )(h, h, h, *tabs_b, o_f, h, norm_g, o_stack)


def retention_tables(dec, scale, backward):
    lg = jnp.log1p(-jnp.exp(dec.astype(F32)))
    pos = jnp.arange(BLOCK, dtype=F32)
    diff = pos[:, None] - pos[None, :]
    if backward:
        diff = -diff
        mask = diff > 0
        qexp = BLOCK - pos
        kexp = pos
    else:
        mask = diff >= 0
        qexp = pos + 1.0
        kexp = BLOCK - 1.0 - pos
    inner = jnp.where(mask[None], jnp.exp(lg[:, None, None] * jnp.where(mask, diff, 0.0)[None]), 0.0) * scale
    qdec = jnp.broadcast_to(jnp.exp(lg[:, None] * qexp)[:, :, None], (N_HEADS, BLOCK, HEAD_DIM))
    kdec = jnp.broadcast_to((jnp.exp(lg[:, None] * kexp) * scale)[:, :, None], (N_HEADS, BLOCK, HEAD_DIM))
    cdec = jnp.broadcast_to(jnp.exp(lg * BLOCK)[:, None, None], (N_HEADS, 1, HEAD_DIM))
    return inner, qdec, kdec, cdec


def _attn_d_kernel(slope_ref, lam_ref, g_ref, q_ref, k_ref, vt_ref, stack_ref, o_ref, m_ref, acc_ref, *bufs,
                   tk, lam_init):
    tq = q_ref.shape[1]
    seq = k_ref.shape[0]
    slope = slope_ref[pl.program_id(1)]
    q0 = pl.program_id(2) * tq
    feat = lax.broadcasted_iota(jnp.int32, (HEAD_DIM, tq), 0)
    q = q_ref[...]
    zero = jnp.zeros_like(q)
    qt = jnp.concatenate([jnp.where(feat < D_HALF, q, zero), jnp.where(feat >= D_HALF, q, zero)], axis=1)
    rel = (lax.broadcasted_iota(jnp.int32, (tk, tq), 0) - lax.broadcasted_iota(jnp.int32, (tk, tq), 1)).astype(F32)

    def score_fn(c):
        off = pl.multiple_of(c * tk, tk)
        bias = slope * jnp.abs(rel + (off - q0).astype(F32))
        return _dot(k_ref[pl.ds(off, tk), :], qt) - jnp.concatenate([bias, bias], axis=1)

    def vt_fn(c):
        return vt_ref[:, pl.ds(pl.multiple_of(c * tk, tk), tk)]

    _flash_init(m_ref, acc_ref)
    _flash_pipeline(seq // tk, score_fn, vt_fn, (bufs[:2], bufs[2:]), m_ref, acc_ref)
    lp = lam_ref[...]
    lam = (jnp.exp(jnp.sum(lp[0:1] * lp[1:2], axis=-1, keepdims=True))
           - jnp.exp(jnp.sum(lp[2:3] * lp[3:4], axis=-1, keepdims=True)) + lam_init)
    on = acc_ref[:HEAD_DIM, :] / acc_ref[HEAD_DIM:HEAD_DIM + 1, :]
    o = on[:, :tq] - lam * on[:, tq:]
    y = o * lax.rsqrt(jnp.mean(o * o, axis=0, keepdims=True) + RMS_EPS) * g_ref[...]
    o_ref[...] = (y * (1.0 - lam_init)).T.astype(o_ref.dtype)


def attn_d(qt, h, vt, o_stack, slopes_log2, lam_params, subln_g, lam_init, batch, seq, tq, tk):
    n = h.shape[0]
    nq = seq // tq
    return pl.pallas_call(
        functools.partial(_attn_d_kernel, tk=tk, lam_init=lam_init),
        grid=(batch, N_HEADS, nq),
        in_specs=[pl.BlockSpec(memory_space=pltpu.SMEM),
                  pl.BlockSpec((4, D_HALF), lambda b, hh, i: (0, 0)),
                  pl.BlockSpec((HEAD_DIM, 1), lambda b, hh, i: (0, 0)),
                  pl.BlockSpec((HEAD_DIM, tq), lambda b, hh, i: (hh, b * nq + i)),
                  pl.BlockSpec((seq, HEAD_DIM), lambda b, hh, i: (b, COL_DK // HEAD_DIM + hh)),
                  pl.BlockSpec((HEAD_DIM, seq), lambda b, hh, i: (hh, b)),
                  pl.BlockSpec(memory_space=pl.ANY)],
        out_specs=pl.BlockSpec((None, tq, HEAD_DIM), lambda b, hh, i: (3, b * nq + i, hh)),
        out_shape=jax.ShapeDtypeStruct(o_stack.shape, o_stack.dtype),
        input_output_aliases={6: 0},
        scratch_shapes=_flash_scratch(tk, 2 * tq),
        compiler_params=_params(("parallel", "parallel", "parallel")),
        name="attn_d",
    )(slopes_log2, lam_params, subln_g, qt, h, vt, o_stack)


def _gate_merge_kernel(x_ref, *refs):
    wg_refs, o_refs, wb_refs = refs[:N_BRANCH], refs[N_BRANCH:2 * N_BRANCH], refs[2 * N_BRANCH:3 * N_BRANCH]
    out_ref = refs[3 * N_BRANCH]
    x = x_ref[...]
    acc = None
    for i in range(N_BRANCH):
        gate = _dot(x, wg_refs[i][...])
        contrib = _dot(o_refs[i][...], wb_refs[i][...]) / (1.0 + jnp.exp(-gate))
        acc = contrib if acc is None else acc + contrib
    out_ref[...] = acc.astype(out_ref.dtype)


def gate_merge(x_bf, w_in_bf, o_stack, w_branch_bf, layer, tm, tn):
    n, d = x_bf.shape
    gcol0 = MIX_COLS // tn
    per = d // tn
    once = dict(pipeline_mode=pl.Buffered(1))
    wg_specs = [pl.BlockSpec((None, d, tn), functools.partial(lambda m, j, i: (layer, 0, gcol0 + i * per + j), i=i))
                for i in range(N_BRANCH)]
    o_specs = [pl.BlockSpec((None, tm, BRANCH_W), functools.partial(lambda m, j, i: (i, m, 0), i=i), **once)
               for i in range(N_BRANCH)]
    wb_specs = [pl.BlockSpec((None, None, BRANCH_W, tn), functools.partial(lambda m, j, i: (layer, i, 0, j), i=i))
                for i in range(N_BRANCH)]
    return pl.pallas_call(
        _gate_merge_kernel,
        grid=(n // tm, d // tn),
        in_specs=[pl.BlockSpec((tm, d), lambda m, j: (m, 0))] + wg_specs + o_specs + wb_specs,
        out_specs=pl.BlockSpec((tm, tn), lambda m, j: (m, j)),
        out_shape=jax.ShapeDtypeStruct((n, d), BF16),
        compiler_params=_params(("parallel", "parallel")),
        name="gate_merge",
    )(x_bf, *([w_in_bf] * N_BRANCH), *([o_stack] * N_BRANCH), *([w_branch_bf] * N_BRANCH))


def _layer_norm(y, g, b):
    mu = jnp.mean(y, axis=-1, keepdims=True)
    d = y - mu
    var = jnp.mean(d * d, axis=-1, keepdims=True)
    return d * lax.rsqrt(var + LN_EPS) * g + b


def _out_ln_kernel(m_ref, w_ref, x_ref, g_ref, b_ref, of_ref, ob_ref, *, alpha):
    k = pl.program_id(1)

    @pl.when(k == 0)
    def _():
        of_ref[...] = jnp.zeros(of_ref.shape, F32)

    of_ref[...] += _dot(m_ref[...], w_ref[...])

    @pl.when(k == pl.num_programs(1) - 1)
    def _():
        rows = min(of_ref.shape[0], LN_ROWS)

        def body(r, c):
            sl = pl.ds(pl.multiple_of(r * rows, rows), rows)
            out = _layer_norm(alpha * x_ref[sl, :] + of_ref[sl, :], g_ref[...], b_ref[...])
            of_ref[sl, :] = out
            ob_ref[sl, :] = out.astype(ob_ref.dtype)
            return c

        lax.fori_loop(0, of_ref.shape[0] // rows, body, 0)


LN_ROWS = 128


def out_proj_ln(merged, w_out_bf, x, g, b, layer, alpha, tm, tk):
    n, d = x.shape
    return pl.pallas_call(
        functools.partial(_out_ln_kernel, alpha=alpha),
        grid=(n // tm, d // tk),
        in_specs=[pl.BlockSpec((tm, tk), lambda m, k: (m, k)),
                  pl.BlockSpec((None, tk, d), lambda m, k: (layer, k, 0)),
                  pl.BlockSpec((tm, d), lambda m, k: (m, 0), pipeline_mode=pl.Buffered(1)),
                  pl.BlockSpec((None, 1, d), lambda m, k: (layer, 0, 0)),
                  pl.BlockSpec((None, 1, d), lambda m, k: (layer, 0, 0))],
        out_specs=[pl.BlockSpec((tm, d), lambda m, k: (m, 0)), pl.BlockSpec((tm, d), lambda m, k: (m, 0))],
        out_shape=[jax.ShapeDtypeStruct((n, d), F32), jax.ShapeDtypeStruct((n, d), BF16)],
        compiler_params=_params(("parallel", "arbitrary")),
        name="out_proj_ln",
    )(merged, w_out_bf, x, g, b)


def _router_kernel(x_ref, whi_ref, wlo_ref, b_ref, o_ref):
    x = x_ref[...]
    x_hi = x.astype(BF16)
    x_lo = (x - x_hi.astype(F32)).astype(BF16)
    whi = whi_ref[...]
    logits = _dot(x_hi, whi) + _dot(x_lo, whi) + _dot(x_hi, wlo_ref[...]) + b_ref[...]
    lane = lax.broadcasted_iota(jnp.int32, logits.shape, 1)
    big = jnp.int32(LANES)
    gmask = lane < N_GROUPS
    gl = jnp.where(gmask, logits, NEG_BIG)
    gmax = jnp.max(gl, axis=-1, keepdims=True)
    g_top = jnp.min(jnp.where(gmask & (gl == gmax), lane, big), axis=-1, keepdims=True)
    g_w = 1.0 / jnp.sum(jnp.where(gmask, jnp.exp(gl - gmax), 0.0), axis=-1, keepdims=True)
    lo = N_GROUPS + g_top * EXPERTS_PER_GROUP
    emask = (lane >= lo) & (lane < lo + EXPERTS_PER_GROUP)
    e1 = jnp.where(emask, logits, NEG_BIG)
    v1 = jnp.max(e1, axis=-1, keepdims=True)
    i1 = jnp.min(jnp.where(emask & (e1 == v1), lane, big), axis=-1, keepdims=True)
    emask2 = emask & (lane != i1)
    e2 = jnp.where(emask2, logits, NEG_BIG)
    v2 = jnp.max(e2, axis=-1, keepdims=True)
    i2 = jnp.min(jnp.where(emask2 & (e2 == v2), lane, big), axis=-1, keepdims=True)
    t = jnp.exp(v2 - v1)
    w1 = g_w / (1.0 + t)
    w2 = g_w * t / (1.0 + t)
    o_ref[...] = (jnp.where(lane == 0, (i1 - N_GROUPS).astype(F32), 0.0)
                  + jnp.where(lane == 1, (i2 - N_GROUPS).astype(F32), 0.0)
                  + jnp.where(lane == 2, w1, 0.0) + jnp.where(lane == 3, w2, 0.0))


def router(x, w_hi, w_lo, bias, tm):
    n, d = x.shape
    return pl.pallas_call(
        _router_kernel,
        grid=(n // tm,),
        in_specs=[pl.BlockSpec((tm, d), lambda m: (m, 0)),
                  pl.BlockSpec((d, LANES), lambda m: (0, 0)),
                  pl.BlockSpec((d, LANES), lambda m: (0, 0)),
                  pl.BlockSpec((1, LANES), lambda m: (0, 0))],
        out_specs=pl.BlockSpec((tm, LANES), lambda m: (m, 0)),
        out_shape=jax.ShapeDtypeStruct((n, LANES), F32),
        compiler_params=_params(("parallel",)),
        name="router",
    )(x, w_hi, w_lo, bias)


MOE_TM = 256


def moe_routing(route, n_tok, tm):
    n_pairs = 2 * n_tok
    e = route[:, 0:2].astype(jnp.int32).reshape(n_pairs)
    w = route[:, 2:4].reshape(n_pairs)
    counts = jnp.sum((e[:, None] == jnp.arange(N_EXPERTS)[None, :]).astype(jnp.int32), axis=0)
    tiles_per = (counts + tm - 1) // tm
    tile_end = jnp.cumsum(tiles_per)
    n_tiles = tile_end[-1]
    t_max = n_pairs // tm + N_EXPERTS
    tile_expert = jnp.minimum(jnp.searchsorted(tile_end, jnp.arange(t_max), side="right"), N_EXPERTS - 1).astype(jnp.int32)
    pad_exp = jnp.repeat(jnp.arange(N_EXPERTS, dtype=jnp.int32), tm)
    pad_rank = jnp.tile(jnp.arange(tm, dtype=jnp.int32), N_EXPERTS)
    pad_key = jnp.where(pad_rank < (tiles_per * tm - counts)[pad_exp], 2 * pad_exp + 1, 2 * N_EXPERTS)
    keys = jnp.concatenate([2 * e, pad_key])
    ident = jnp.concatenate([jnp.arange(n_pairs, dtype=jnp.int32),
                             n_pairs + jnp.arange(N_EXPERTS * tm, dtype=jnp.int32)])
    _, ident, wslot = lax.sort((keys, ident, jnp.concatenate([w, jnp.zeros((N_EXPERTS * tm,), w.dtype)])),
                               num_keys=1, is_stable=True)
    valid = ident < n_pairs
    tok = jnp.where(valid, ident // 2, 0)
    dst = jnp.where(valid, (ident % 2) * n_tok + ident // 2, ident)
    return (tile_expert, n_tiles.astype(jnp.int32)[None], tok.astype(jnp.int32), dst.astype(jnp.int32),
            jnp.broadcast_to(wslot[:, None], (t_max * tm, LANES)))


def _moe_kernel(te_ref, nt_ref, tok_ref, dst_ref, x_hbm, w_ref, wgu_ref, wd_ref, out_hbm,
                xbuf, ybuf, gsem, ssem):
    t = pl.program_id(0)
    nt = nt_ref[0]
    tm = xbuf.shape[1]
    f = wd_ref.shape[0]

    def gather_row(tile, r, slot):
        tok = tok_ref[tile * tm + r]
        return pltpu.make_async_copy(x_hbm.at[pl.ds(tok, 1), :], xbuf.at[slot, pl.ds(r, 1), :], gsem.at[slot])

    def scatter_row(tile, r, slot):
        d = dst_ref[tile * tm + r]
        return pltpu.make_async_copy(ybuf.at[slot, pl.ds(r, 1), :], out_hbm.at[pl.ds(d, 1), :], ssem.at[slot])

    def for_rows(fn):
        def body(r, c):
            fn(r)
            return c
        lax.fori_loop(0, tm, body, 0, unroll=8)

    @pl.when((t == 0) & (nt > 0))
    def _():
        for_rows(lambda r: gather_row(0, r, 0).start())

    @pl.when(t + 1 < nt)
    def _():
        for_rows(lambda r: gather_row(t + 1, r, (t + 1) % 2).start())

    @pl.when(t < nt)
    def _():
        slot = t % 2
        for_rows(lambda r: gather_row(t, r, slot).wait())
        xb = xbuf[slot].astype(BF16)
        gu = _dot(xb, wgu_ref[...])
        gate = gu[:, :f]
        w = w_ref[...]
        hmid = (gate / (1.0 + jnp.exp(-gate))) * gu[:, f:] * jnp.concatenate([w] * (f // LANES), axis=1)
        y = _dot(hmid.astype(BF16), wd_ref[...])

        @pl.when(t >= 2)
        def _():
            for_rows(lambda r: scatter_row(t - 2, r, slot).wait())

        ybuf[slot] = y
        for_rows(lambda r: scatter_row(t, r, slot).start())

    @pl.when(t == pl.num_programs(0) - 1)
    def _():
        for back in (1, 2):
            @pl.when(nt >= back)
            def _():
                for_rows(lambda r: scatter_row(nt - back, r, (nt - back) % 2).wait())


def moe_experts(x, routing, wgu_bf, wd_bf, layer, tm):
    n, d = x.shape
    f = wd_bf.shape[-2]
    tile_expert, n_tiles, tok, dst, wslot = routing
    t_max = tile_expert.shape[0]
    grid_spec = pltpu.PrefetchScalarGridSpec(
        num_scalar_prefetch=4,
        grid=(t_max,),
        in_specs=[pl.BlockSpec(memory_space=pl.ANY),
                  pl.BlockSpec((tm, LANES), lambda t, te, nt, tk, ds: (t, 0)),
                  pl.BlockSpec((None, None, d, 2 * f), lambda t, te, nt, tk, ds: (layer, te[t], 0, 0)),
                  pl.BlockSpec((None, None, f, d), lambda t, te, nt, tk, ds: (layer, te[t], 0, 0))],
        out_specs=pl.BlockSpec(memory_space=pl.ANY),
        scratch_shapes=[pltpu.VMEM((2, tm, d), F32), pltpu.VMEM((2, tm, d), F32),
                        pltpu.SemaphoreType.DMA((2,)), pltpu.SemaphoreType.DMA((2,))])
    return pl.pallas_call(
        _moe_kernel,
        grid_spec=grid_spec,
        out_shape=jax.ShapeDtypeStruct((t_max * tm, d), F32),
        compiler_params=_params(("arbitrary",)),
        name="moe_experts",
    )(tile_expert, n_tiles, tok, dst, x, wslot, wgu_bf, wd_bf)


def _pair_ln_kernel(y0_ref, y1_ref, x_ref, g_ref, b_ref, of_ref, ob_ref, *, alpha):
    out = _layer_norm(alpha * x_ref[...] + (y0_ref[...] + y1_ref[...]), g_ref[...], b_ref[...])
    of_ref[...] = out
    ob_ref[...] = out.astype(ob_ref.dtype)


def pair_sum_ln(y_pairs, x, g, b, layer, alpha, tm):
    n, d = x.shape
    nblk = n // tm
    return pl.pallas_call(
        functools.partial(_pair_ln_kernel, alpha=alpha),
        grid=(nblk,),
        in_specs=[pl.BlockSpec((tm, d), lambda m: (m, 0)),
                  pl.BlockSpec((tm, d), lambda m: (nblk + m, 0)),
                  pl.BlockSpec((tm, d), lambda m: (m, 0)),
                  pl.BlockSpec((None, 1, d), lambda m: (layer, 0, 0)),
                  pl.BlockSpec((None, 1, d), lambda m: (layer, 0, 0))],
        out_specs=[pl.BlockSpec((tm, d), lambda m: (m, 0)), pl.BlockSpec((tm, d), lambda m: (m, 0))],
        out_shape=[jax.ShapeDtypeStruct((n, d), F32), jax.ShapeDtypeStruct((n, d), BF16)],
        compiler_params=_params(("parallel",)),
        name="pair_sum_ln",
    )(y_pairs, y_pairs, x, g, b)


def _rope_tables(seq):
    t = jnp.arange(seq)
    row_id = (t // GRID_W).astype(F32)
    col_id = (t % GRID_W).astype(F32)
    nfreq = HEAD_DIM // 4
    inv = ROPE_THETA ** (-jnp.arange(nfreq, dtype=F32) / nfreq)
    ar = row_id[:, None] * inv
    ac = col_id[:, None] * inv
    cos = jnp.concatenate([jnp.cos(ar), jnp.cos(ar), jnp.cos(ac), jnp.cos(ac)], axis=-1)
    sin = jnp.concatenate([-jnp.sin(ar), jnp.sin(ar), -jnp.sin(ac), jnp.sin(ac)], axis=-1)
    return cos, sin


def _tile(n, pref):
    t = min(n, pref)
    while n % t:
        t //= 2
    return t


def kernel(x, w_in, attn_sink, qk_norm_q, qk_norm_k, ret_decay_fwd, ret_decay_bwd, ret_norm_g,
           diff_lambda_q1, diff_lambda_k1, diff_lambda_q2, diff_lambda_k2, diff_subln_g,
           w_branch, w_out, ln_mix_g, ln_mix_b, router_group, router_group_b, router_expert,
           router_expert_b, expert_w_gate, expert_w_up, expert_w_down, ln_ffn_g, ln_ffn_b):
    batch, seq, d = x.shape
    depth = w_in.shape[0]
    n = batch * seq
    alpha = (2.0 * depth) ** 0.25
    assert w_in.shape[2] == MIX_COLS + N_BRANCH * d and seq % BLOCK == 0 and seq % GRID_W == 0

    colscale = jnp.ones((w_in.shape[2],), F32)
    colscale = colscale.at[COL_AQ:COL_AQ + BRANCH_W].set(HEAD_DIM ** -0.5 * LOG2E)
    colscale = colscale.at[COL_DQ:COL_DQ + BRANCH_W].set(D_HALF ** -0.5 * LOG2E)
    w_in_bf = (w_in * colscale).astype(BF16)
    w_branch_bf = w_branch.astype(BF16)
    w_out_bf = w_out.astype(BF16)
    wgu_bf = jnp.concatenate([expert_w_gate.astype(BF16), expert_w_up.astype(BF16)], axis=-1)
    wd_bf = expert_w_down.astype(BF16)
    ln_mix_g3, ln_mix_b3 = ln_mix_g[:, None, :], ln_mix_b[:, None, :]
    ln_ffn_g3, ln_ffn_b3 = ln_ffn_g[:, None, :], ln_ffn_b[:, None, :]

    cos_t, sin_t = _rope_tables(seq)
    slopes_log2 = jnp.asarray([LOG2E * 2.0 ** (-(i + 1.0)) for i in range(N_HEADS)], F32)

    tm_big = _tile(n, 1024)
    xf = x.reshape(n, d)
    xb = xf.astype(BF16)
    for l in range(depth):
        h = in_proj(xb, w_in_bf, l, MIX_COLS, tm_big, 1024)
        o_all = attn_a(h, attn_sink[l].astype(F32) * LOG2E, batch, seq)
        gains = jnp.concatenate([jnp.tile(qk_norm_q[l], N_HEADS) * (HEAD_DIM ** -0.5 * LOG2E),
                                 jnp.tile(qk_norm_k[l], KV_HEADS)])[None, :]
        qt_b, k_b = qk_prep(h, gains, cos_t, sin_t, batch, seq, _tile(seq, 512))
        vt_b = h[:, COL_BV:COL_BV + KV_HEADS * HEAD_DIM].T
        o_all = attn_b(qt_b, k_b, vt_b, o_all, batch, seq, _tile(seq, 256), _tile(seq // 2, 512))
        tabs_f = retention_tables(ret_decay_fwd[l], HEAD_DIM ** -0.5, False)
        tabs_b = retention_tables(ret_decay_bwd[l], HEAD_DIM ** -0.5, True)
        o_all = retention(h, tabs_f, tabs_b, ret_norm_g[l][None, :], o_all, batch, seq)
        lam_init = 0.8 - 0.6 * math.exp(-0.3 * l)
        lam_params = jnp.stack([diff_lambda_q1[l], diff_lambda_k1[l], diff_lambda_q2[l], diff_lambda_k2[l]]).astype(F32)
        qt_d = h[:, COL_DQ:COL_DQ + BRANCH_W].T
        vt_d = h[:, COL_DV:COL_DV + BRANCH_W].T
        o_all = attn_d(qt_d, h, vt_d, o_all, slopes_log2, lam_params, diff_subln_g[l][:, None], lam_init, batch, seq,
                       _tile(seq, 512), _tile(seq // 2, 512))
        merged = gate_merge(xb, w_in_bf, o_all, w_branch_bf, l, tm_big, 256)
        xf, xb = out_proj_ln(merged, w_out_bf, xf, ln_mix_g3, ln_mix_b3, l, alpha, _tile(n, 512), _tile(d, 512))
        w_r = jnp.concatenate([router_group[l], router_expert[l]], axis=1)
        w_r = jnp.pad(w_r, ((0, 0), (0, LANES - w_r.shape[1])))
        w_r_hi = w_r.astype(BF16)
        w_r_lo = (w_r - w_r_hi.astype(F32)).astype(BF16)
        b_r = jnp.pad(jnp.concatenate([router_group_b[l], router_expert_b[l]]), (0, LANES - N_GROUPS - N_EXPERTS))[None, :]
        route = router(xf, w_r_hi, w_r_lo, b_r, _tile(n, 512))
        y_pairs = moe_experts(xf, moe_routing(route, n, MOE_TM), wgu_bf, wd_bf, l, MOE_TM)
        xf, xb = pair_sum_ln(y_pairs, xf, ln_ffn_g3, ln_ffn_b3, l, alpha, _tile(n, 256))
    return xf.reshape(batch, seq, d)
```

```python
import functools
import math

import jax
import jax.numpy as jnp
from jax import lax
from jax.experimental import pallas as pl
from jax.experimental.pallas import tpu as pltpu

F32 = jnp.float32
BF16 = jnp.bfloat16

HEAD_DIM = 128
BLOCK = 128
GRID_W = 64
N_HEADS = 8
KV_HEADS = 2
D_HALF = HEAD_DIM // 2
N_BRANCH = 4
BRANCH_W = N_HEADS * HEAD_DIM
ROPE_THETA = 10000.0
N_GROUPS = 4
EXPERTS_PER_GROUP = 4
N_EXPERTS = N_GROUPS * EXPERTS_PER_GROUP
LN_EPS = 1e-5
RMS_EPS = 1e-6
NEG_BIG = -1e30
LOG2E = 1.4426950408889634
LANES = 128
VMEM_LIMIT = 56 * 1024 * 1024

COL_AQ, COL_AK, COL_AV = 0, 1024, 1280
COL_BQ, COL_BK, COL_BV = 1536, 2560, 2816
COL_CQ, COL_CK, COL_CV, COL_CG = 3072, 4096, 5120, 6144
COL_DQ, COL_DK, COL_DV = 7168, 8192, 9216
MIX_COLS = 10240


def _params(sem):
    return pltpu.CompilerParams(dimension_semantics=sem, vmem_limit_bytes=VMEM_LIMIT)


def _nt_dot(a, b):
    return lax.dot_general(a, b, (((1,), (1,)), ((), ())), preferred_element_type=F32)


def _tn_dot(a, b):
    return lax.dot_general(a, b, (((0,), (0,)), ((), ())), preferred_element_type=F32)


def _dot(a, b):
    return jnp.dot(a, b, preferred_element_type=F32)


def _mm_kernel(x_ref, w_ref, o_ref):
    o_ref[...] = _dot(x_ref[...], w_ref[...]).astype(o_ref.dtype)


def in_proj(x_bf, w_in_bf, layer, n_cols, tm, tn):
    n, k = x_bf.shape
    return pl.pallas_call(
        _mm_kernel,
        grid=(n // tm, n_cols // tn),
        in_specs=[pl.BlockSpec((tm, k), lambda i, j: (i, 0)),
                  pl.BlockSpec((None, k, tn), lambda i, j: (layer, 0, j))],
        out_specs=pl.BlockSpec((tm, tn), lambda i, j: (i, j)),
        out_shape=jax.ShapeDtypeStruct((n, n_cols), BF16),
        compiler_params=_params(("parallel", "parallel")),
        name="in_proj",
    )(x_bf, w_in_bf)


def _attn_a_kernel(sink_ref, q_ref, kp_ref, kc_ref, kn_ref, vp_ref, vc_ref, vn_ref, o_ref, *, slopes, nb):
    i = pl.program_id(1)
    rep = N_HEADS // KV_HEADS
    rows = rep * BLOCK
    row = lax.broadcasted_iota(jnp.int32, (rows, 3 * BLOCK), 0)
    col = lax.broadcasted_iota(jnp.int32, (rows, 3 * BLOCK), 1)
    rel = (row % BLOCK) - (col - BLOCK)
    dist = jnp.abs(rel)
    valid = (dist <= BLOCK) & ((col >= BLOCK) | (i > 0)) & ((col < 2 * BLOCK) | (i < nb - 1))
    distf = dist.astype(F32)
    rowc = lax.broadcasted_iota(jnp.int32, (rows, 1), 0)
    for g in range(KV_HEADS):
        q = jnp.concatenate([q_ref[:, (g * rep + r) * HEAD_DIM:(g * rep + r + 1) * HEAD_DIM]
                             for r in range(rep)], axis=0)
        ks = slice(g * HEAD_DIM, (g + 1) * HEAD_DIM)
        k = jnp.concatenate([kp_ref[:, ks], kc_ref[:, ks], kn_ref[:, ks]], axis=0)
        v = jnp.concatenate([vp_ref[:, ks], vc_ref[:, ks], vn_ref[:, ks]], axis=0)
        slope = jnp.full((rows, 1), slopes[g * rep], F32)
        sink = jnp.full((rows, 1), sink_ref[g * rep], F32)
        for r in range(1, rep):
            slope = jnp.where(rowc >= r * BLOCK, slopes[g * rep + r], slope)
            sink = jnp.where(rowc >= r * BLOCK, sink_ref[g * rep + r], sink)
        s = _nt_dot(q, k) - slope * distf
        s = jnp.where(valid, s, NEG_BIG)
        m = jnp.maximum(jnp.max(s, axis=-1, keepdims=True), sink)
        p = jnp.exp2(s - m)
        denom = jnp.sum(p, axis=-1, keepdims=True) + jnp.exp2(sink - m)
        o = _dot(p.astype(BF16), v) / denom
        for r in range(rep):
            hh = g * rep + r
            o_ref[:, hh * HEAD_DIM:(hh + 1) * HEAD_DIM] = o[r * BLOCK:(r + 1) * BLOCK].astype(o_ref.dtype)


def attn_a(h, sink_log2, batch, seq):
    n = h.shape[0]
    nb = seq // BLOCK
    slopes = tuple(LOG2E * 2.0 ** (-(i + 1.0)) for i in range(N_HEADS))
    kvw = KV_HEADS * HEAD_DIM
    kcol, vcol = COL_AK // kvw, COL_AV // kvw

    def prev(b, i):
        return b * nb + jnp.maximum(i - 1, 0)

    def nxt(b, i):
        return b * nb + jnp.minimum(i + 1, nb - 1)

    return pl.pallas_call(
        functools.partial(_attn_a_kernel, slopes=slopes, nb=nb),
        grid=(batch, nb),
        in_specs=[pl.BlockSpec(memory_space=pltpu.SMEM),
                  pl.BlockSpec((BLOCK, BRANCH_W), lambda b, i: (b * nb + i, COL_AQ // BRANCH_W)),
                  pl.BlockSpec((BLOCK, kvw), lambda b, i: (prev(b, i), kcol)),
                  pl.BlockSpec((BLOCK, kvw), lambda b, i: (b * nb + i, kcol)),
                  pl.BlockSpec((BLOCK, kvw), lambda b, i: (nxt(b, i), kcol)),
                  pl.BlockSpec((BLOCK, kvw), lambda b, i: (prev(b, i), vcol)),
                  pl.BlockSpec((BLOCK, kvw), lambda b, i: (b * nb + i, vcol)),
                  pl.BlockSpec((BLOCK, kvw), lambda b, i: (nxt(b, i), vcol))],
        out_specs=pl.BlockSpec((None, BLOCK, BRANCH_W), lambda b, i: (0, b * nb + i, 0)),
        out_shape=jax.ShapeDtypeStruct((N_BRANCH, n, BRANCH_W), BF16),
        compiler_params=_params(("parallel", "parallel")),
        name="attn_a",
    )(sink_log2, h, h, h, h, h, h, h)


def _norm_rope(x, g, cos, sin, first_quarter):
    y = x * lax.rsqrt(jnp.mean(x * x, axis=-1, keepdims=True) + RMS_EPS) * g
    partner = jnp.where(first_quarter, pltpu.roll(y, HEAD_DIM - HEAD_DIM // 4, 1), pltpu.roll(y, HEAD_DIM // 4, 1))
    return y * cos + partner * sin


def _qk_prep_kernel(q0_ref, q1_ref, k_ref, g_ref, cos_ref, sin_ref, qt_ref, ko_ref):
    lane = lax.broadcasted_iota(jnp.int32, (k_ref.shape[0], HEAD_DIM), 1)
    first_quarter = (lane % (HEAD_DIM // 2)) < (HEAD_DIM // 4)
    cos = cos_ref[...]
    sin = sin_ref[...]
    half = N_HEADS // 2
    for hh in range(N_HEADS):
        src = q0_ref if hh < half else q1_ref
        sl = slice((hh % half) * HEAD_DIM, (hh % half + 1) * HEAD_DIM)
        gs = slice(hh * HEAD_DIM, (hh + 1) * HEAD_DIM)
        y = _norm_rope(src[:, sl].astype(F32), g_ref[:, gs], cos, sin, first_quarter)
        qt_ref[gs, :] = y.T.astype(qt_ref.dtype)
    for hh in range(KV_HEADS):
        sl = slice(hh * HEAD_DIM, (hh + 1) * HEAD_DIM)
        gs = slice((N_HEADS + hh) * HEAD_DIM, (N_HEADS + hh + 1) * HEAD_DIM)
        ko_ref[:, sl] = _norm_rope(k_ref[:, sl].astype(F32), g_ref[:, gs], cos, sin, first_quarter).astype(ko_ref.dtype)


def qk_prep(h, gains, cos_t, sin_t, batch, seq, tm):
    n = h.shape[0]
    qw = BRANCH_W // 2
    kw = KV_HEADS * HEAD_DIM
    nblk = seq // tm
    return pl.pallas_call(
        _qk_prep_kernel,
        grid=(batch, nblk),
        in_specs=[pl.BlockSpec((tm, qw), lambda b, i: (b * nblk + i, COL_BQ // qw)),
                  pl.BlockSpec((tm, qw), lambda b, i: (b * nblk + i, COL_BQ // qw + 1)),
                  pl.BlockSpec((tm, kw), lambda b, i: (b * nblk + i, COL_BK // kw)),
                  pl.BlockSpec((1, BRANCH_W + kw), lambda b, i: (0, 0)),
                  pl.BlockSpec((tm, HEAD_DIM), lambda b, i: (i, 0)),
                  pl.BlockSpec((tm, HEAD_DIM), lambda b, i: (i, 0))],
        out_specs=[pl.BlockSpec((BRANCH_W, tm), lambda b, i: (0, b * nblk + i)),
                   pl.BlockSpec((tm, kw), lambda b, i: (b * nblk + i, 0))],
        out_shape=[jax.ShapeDtypeStruct((BRANCH_W, n), BF16), jax.ShapeDtypeStruct((n, kw), BF16)],
        compiler_params=_params(("parallel", "parallel")),
        name="qk_prep",
    )(h, h, h, gains, cos_t, sin_t)


ONES_ROWS = 16


FLASH_CHUNKS_PER_TRIP = 4


def _flash_pipeline(n_chunks, score_fn, vt_fn, bufs, m_ref, acc_ref):
    per_trip = FLASH_CHUNKS_PER_TRIP if n_chunks % FLASH_CHUNKS_PER_TRIP == 0 else 2
    assert n_chunks % per_trip == 0
    scores, accumulate = _flash_stages(m_ref, acc_ref)
    scores(score_fn(0), bufs[0])

    def body(i, carry):
        c = per_trip * i
        for j in range(per_trip):
            scores(score_fn(jnp.minimum(c + j + 1, n_chunks - 1)), bufs[(j + 1) % 2])
            accumulate(vt_fn(c + j), bufs[j % 2])
        return carry

    lax.fori_loop(0, n_chunks // per_trip, body, 0)


def _flash_init(m_ref, acc_ref):
    m_ref[...] = jnp.full(m_ref.shape, NEG_BIG, F32)
    acc_ref[...] = jnp.zeros(acc_ref.shape, F32)


def _flash_stages(m_ref, acc_ref):
    def scores(s_and_offset, buf):
        s, offset = s_and_offset
        s_ref, x_ref, c_ref = buf
        s_ref[...] = s
        x_ref[...] = jnp.max(s, axis=0, keepdims=True) + offset
        c_ref[...] = jnp.full(c_ref.shape, offset, F32)

    def accumulate(vt, buf):
        s_ref, x_ref, c_ref = buf
        m_old = m_ref[...]
        m_new = jnp.maximum(m_old, x_ref[...])
        a = jnp.exp2(m_old - m_new)
        p = jnp.exp2(s_ref[...] - (m_new - c_ref[...])).astype(BF16)
        vt1 = jnp.concatenate([vt, jnp.ones((ONES_ROWS, vt.shape[1]), vt.dtype)], axis=0)
        acc_ref[...] = a * acc_ref[...] + _dot(vt1, p)
        m_ref[...] = m_new

    return scores, accumulate


def _attn_b_kernel(qt_ref, k_ref, vt_ref, stack_ref, o_ref, m_ref, acc_ref, *bufs, tk):
    rep = qt_ref.shape[0] // HEAD_DIM
    tq = qt_ref.shape[1]
    seq = k_ref.shape[0]
    qt = jnp.concatenate([qt_ref[r * HEAD_DIM:(r + 1) * HEAD_DIM, :] for r in range(rep)], axis=1)

    def score_fn(c):
        return _dot(k_ref[pl.ds(pl.multiple_of(c * tk, tk), tk), :], qt), 0.0

    def vt_fn(c):
        return vt_ref[:, pl.ds(pl.multiple_of(c * tk, tk), tk)]

    _flash_init(m_ref, acc_ref)
    _flash_pipeline(seq // tk, score_fn, vt_fn, (bufs[:3], bufs[3:]), m_ref, acc_ref)
    o = acc_ref[:HEAD_DIM, :] / acc_ref[HEAD_DIM:HEAD_DIM + 1, :]
    for r in range(rep):
        o_ref[:, r * HEAD_DIM:(r + 1) * HEAD_DIM] = o[:, r * tq:(r + 1) * tq].T.astype(o_ref.dtype)


def _flash_scratch(tk, width):
    row = pltpu.VMEM((1, width), F32)
    return [row, pltpu.VMEM((HEAD_DIM + ONES_ROWS, width), F32),
            pltpu.VMEM((tk, width), F32), row, row,
            pltpu.VMEM((tk, width), F32), row, row]


def attn_b(qt, kb, vt, o_stack, batch, seq, tq, tk):
    n = kb.shape[0]
    rep = N_HEADS // KV_HEADS
    nq = seq // tq
    width = rep * tq
    return pl.pallas_call(
        functools.partial(_attn_b_kernel, tk=tk),
        grid=(batch, KV_HEADS, nq),
        in_specs=[pl.BlockSpec((rep * HEAD_DIM, tq), lambda b, g, i: (g, b * nq + i)),
                  pl.BlockSpec((seq, HEAD_DIM), lambda b, g, i: (b, g)),
                  pl.BlockSpec((HEAD_DIM, seq), lambda b, g, i: (g, b)),
                  pl.BlockSpec(memory_space=pl.ANY)],
        out_specs=pl.BlockSpec((None, tq, rep * HEAD_DIM), lambda b, g, i: (1, b * nq + i, g)),
        out_shape=jax.ShapeDtypeStruct(o_stack.shape, o_stack.dtype),
        input_output_aliases={3: 0},
        scratch_shapes=_flash_scratch(tk, width),
        compiler_params=_params(("parallel", "parallel", "parallel")),
        name="attn_b",
    )(qt, kb, vt, o_stack)


def _retention_kernel(*refs, final):
    if final:
        (q_ref, k_ref, v_ref, inner_ref, qdec_ref, kdec_ref, cdec_ref, prev_ref, gate_ref, ng_ref, stack_ref,
         o_ref, state_ref) = refs
    else:
        q_ref, k_ref, v_ref, inner_ref, qdec_ref, kdec_ref, cdec_ref, o_ref, state_ref = refs

    @pl.when(pl.program_id(1) == 0)
    def _():
        state_ref[...] = jnp.zeros(state_ref.shape, F32)

    n_sub = q_ref.shape[0] // BLOCK
    for sub in (reversed(range(n_sub)) if final else range(n_sub)):
        rows = slice(sub * BLOCK, (sub + 1) * BLOCK)
        for hh in range(N_HEADS):
            sl = slice(hh * HEAD_DIM, (hh + 1) * HEAD_DIM)
            q = q_ref[rows, sl]
            k = k_ref[rows, sl]
            v = v_ref[rows, sl]
            state = state_ref[hh]
            inner = _nt_dot(q, k) * inner_ref[hh]
            o = _dot(inner.astype(BF16), v) + _dot(q, state.astype(BF16)) * qdec_ref[hh]
            kd = (k.astype(F32) * kdec_ref[hh]).astype(BF16)
            state_ref[hh] = state * cdec_ref[hh] + _tn_dot(kd, v)
            if final:
                r = o + prev_ref[rows, sl]
                mu = jnp.mean(r, axis=-1, keepdims=True)
                d = r - mu
                var = jnp.mean(d * d, axis=-1, keepdims=True)
                rn = d * lax.rsqrt(var + LN_EPS) * ng_ref[:, sl]
                gate = gate_ref[rows, sl].astype(F32)
                silu = gate / (1.0 + jnp.exp(-gate))
                o_ref[rows, sl] = (silu * rn).astype(o_ref.dtype)
            else:
                o_ref[rows, sl] = o


RET_CHUNKS = 4


def retention(h, tabs_f, tabs_b, norm_g, o_stack, batch, seq):
    n = h.shape[0]
    rb = BLOCK * RET_CHUNKS if seq % (BLOCK * RET_CHUNKS) == 0 else BLOCK
    nb = seq // rb
    w = BRANCH_W
    tab_specs = [pl.BlockSpec((N_HEADS, BLOCK, BLOCK), lambda b, c: (0, 0, 0)),
                 pl.BlockSpec((N_HEADS, BLOCK, HEAD_DIM), lambda b, c: (0, 0, 0)),
                 pl.BlockSpec((N_HEADS, BLOCK, HEAD_DIM), lambda b, c: (0, 0, 0)),
                 pl.BlockSpec((N_HEADS, 1, HEAD_DIM), lambda b, c: (0, 0, 0))]

    def specs(rowmap):
        return [pl.BlockSpec((rb, w), lambda b, c: (rowmap(b, c), COL_CQ // w)),
                pl.BlockSpec((rb, w), lambda b, c: (rowmap(b, c), COL_CK // w)),
                pl.BlockSpec((rb, w), lambda b, c: (rowmap(b, c), COL_CV // w))]

    fmap = lambda b, c: b * nb + c
    bmap = lambda b, c: b * nb + (nb - 1 - c)
    scratch = [pltpu.VMEM((N_HEADS, HEAD_DIM, HEAD_DIM), F32)]
    o_f = pl.pallas_call(
        functools.partial(_retention_kernel, final=False),
        grid=(batch, nb),
        in_specs=specs(fmap) + tab_specs,
        out_specs=pl.BlockSpec((rb, w), lambda b, c: (fmap(b, c), 0)),
        out_shape=jax.ShapeDtypeStruct((n, w), F32),
        scratch_shapes=scratch,
        compiler_params=_params(("parallel", "arbitrary")),
        name="retention_fwd",
    )(h, h, h, *tabs_f)
    return pl.pallas_call(
        functools.partial(_retention_kernel, final=True),
        grid=(batch, nb),
        in_specs=specs(bmap) + tab_specs + [
            pl.BlockSpec((rb, w), lambda b, c: (bmap(b, c), 0)),
            pl.BlockSpec((rb, w), lambda b, c: (bmap(b, c), COL_CG // w)),
            pl.BlockSpec((1, w), lambda b, c: (0, 0)),
            pl.BlockSpec(memory_space=pl.ANY)],
        out_specs=pl.BlockSpec((None, rb, w), lambda b, c: (2, bmap(b, c), 0)),
        out_shape=jax.ShapeDtypeStruct(o_stack.shape, o_stack.dtype),
        input_output_aliases={10: 0},
        scratch_shapes=scratch,
        compiler_params=_params(("parallel", "arbitrary")),
        name="retention_bwd",
    )(h, h, h, *tabs_b, o_f, h, norm_g, o_stack)


def retention_tables(dec, scale, backward):
    lg = jnp.log1p(-jnp.exp(dec.astype(F32)))
    pos = jnp.arange(BLOCK, dtype=F32)
    diff = pos[:, None] - pos[None, :]
    if backward:
        diff = -diff
        mask = diff > 0
        qexp = BLOCK - pos
        kexp = pos
    else:
        mask = diff >= 0
        qexp = pos + 1.0
        kexp = BLOCK - 1.0 - pos
    inner = jnp.where(mask[None], jnp.exp(lg[:, None, None] * jnp.where(mask, diff, 0.0)[None]), 0.0) * scale
    qdec = jnp.broadcast_to(jnp.exp(lg[:, None] * qexp)[:, :, None], (N_HEADS, BLOCK, HEAD_DIM))
    kdec = jnp.broadcast_to((jnp.exp(lg[:, None] * kexp) * scale)[:, :, None], (N_HEADS, BLOCK, HEAD_DIM))
    cdec = jnp.broadcast_to(jnp.exp(lg * BLOCK)[:, None, None], (N_HEADS, 1, HEAD_DIM))
    return inner, qdec, kdec, cdec


def _attn_d_kernel(slope_ref, lam_ref, g_ref, q_ref, k_ref, vt_ref, stack_ref, o_ref, m_ref, acc_ref, *bufs,
                   tk, lam_init):
    tq = q_ref.shape[1]
    seq = k_ref.shape[0]
    slope = slope_ref[pl.program_id(1)]
    q0 = pl.program_id(2) * tq
    feat = lax.broadcasted_iota(jnp.int32, (HEAD_DIM, tq), 0)
    q = q_ref[...]
    zero = jnp.zeros_like(q)
    qt = jnp.concatenate([jnp.where(feat < D_HALF, q, zero), jnp.where(feat >= D_HALF, q, zero)], axis=1)
    rel = slope * (lax.broadcasted_iota(jnp.int32, (tk, tq), 0)
                   - lax.broadcasted_iota(jnp.int32, (tk, tq), 1)).astype(F32)

    def score_fn(c):
        off = pl.multiple_of(c * tk, tk)
        bias = jnp.abs(rel + slope * (off - q0).astype(F32))
        return _dot(k_ref[pl.ds(off, tk), :], qt) - jnp.concatenate([bias, bias], axis=1), 0.0

    def vt_fn(c):
        return vt_ref[:, pl.ds(pl.multiple_of(c * tk, tk), tk)]

    _flash_init(m_ref, acc_ref)
    _flash_pipeline(seq // tk, score_fn, vt_fn, (bufs[:3], bufs[3:]), m_ref, acc_ref)
    lp = lam_ref[...]
    lam = (jnp.exp(jnp.sum(lp[0:1] * lp[1:2], axis=-1, keepdims=True))
           - jnp.exp(jnp.sum(lp[2:3] * lp[3:4], axis=-1, keepdims=True)) + lam_init)
    on = acc_ref[:HEAD_DIM, :] / acc_ref[HEAD_DIM:HEAD_DIM + 1, :]
    o = on[:, :tq] - lam * on[:, tq:]
    y = o * lax.rsqrt(jnp.mean(o * o, axis=0, keepdims=True) + RMS_EPS) * g_ref[...]
    o_ref[...] = (y * (1.0 - lam_init)).T.astype(o_ref.dtype)


def attn_d(qt, h, vt, o_stack, slopes_log2, lam_params, subln_g, lam_init, batch, seq, tq, tk):
    n = h.shape[0]
    nq = seq // tq
    return pl.pallas_call(
        functools.partial(_attn_d_kernel, tk=tk, lam_init=lam_init),
        grid=(batch, N_HEADS, nq),
        in_specs=[pl.BlockSpec(memory_space=pltpu.SMEM),
                  pl.BlockSpec((4, D_HALF), lambda b, hh, i: (0, 0)),
                  pl.BlockSpec((HEAD_DIM, 1), lambda b, hh, i: (0, 0)),
                  pl.BlockSpec((HEAD_DIM, tq), lambda b, hh, i: (hh, b * nq + i)),
                  pl.BlockSpec((seq, HEAD_DIM), lambda b, hh, i: (b, COL_DK // HEAD_DIM + hh)),
                  pl.BlockSpec((HEAD_DIM, seq), lambda b, hh, i: (hh, b)),
                  pl.BlockSpec(memory_space=pl.ANY)],
        out_specs=pl.BlockSpec((None, tq, HEAD_DIM), lambda b, hh, i: (3, b * nq + i, hh)),
        out_shape=jax.ShapeDtypeStruct(o_stack.shape, o_stack.dtype),
        input_output_aliases={6: 0},
        scratch_shapes=_flash_scratch(tk, 2 * tq),
        compiler_params=_params(("parallel", "parallel", "parallel")),
        name="attn_d",
    )(slopes_log2, lam_params, subln_g, qt, h, vt, o_stack)


def _gate_merge_kernel(x_ref, *refs):
    wg_refs, o_refs, wb_refs = refs[:N_BRANCH], refs[N_BRANCH:2 * N_BRANCH], refs[2 * N_BRANCH:3 * N_BRANCH]
    out_ref = refs[3 * N_BRANCH]
    x = x_ref[...]
    acc = None
    for i in range(N_BRANCH):
        gate = _dot(x, wg_refs[i][...])
        contrib = _dot(o_refs[i][...], wb_refs[i][...]) / (1.0 + jnp.exp(-gate))
        acc = contrib if acc is None else acc + contrib
    out_ref[...] = acc.astype(out_ref.dtype)


def gate_merge(x_bf, w_in_bf, o_stack, w_branch_bf, layer, tm, tn):
    n, d = x_bf.shape
    gcol0 = MIX_COLS // tn
    per = d // tn
    once = dict(pipeline_mode=pl.Buffered(1))
    wg_specs = [pl.BlockSpec((None, d, tn), functools.partial(lambda m, j, i: (layer, 0, gcol0 + i * per + j), i=i))
                for i in range(N_BRANCH)]
    o_specs = [pl.BlockSpec((None, tm, BRANCH_W), functools.partial(lambda m, j, i: (i, m, 0), i=i), **once)
               for i in range(N_BRANCH)]
    wb_specs = [pl.BlockSpec((None, None, BRANCH_W, tn), functools.partial(lambda m, j, i: (layer, i, 0, j), i=i))
                for i in range(N_BRANCH)]
    return pl.pallas_call(
        _gate_merge_kernel,
        grid=(n // tm, d // tn),
        in_specs=[pl.BlockSpec((tm, d), lambda m, j: (m, 0))] + wg_specs + o_specs + wb_specs,
        out_specs=pl.BlockSpec((tm, tn), lambda m, j: (m, j)),
        out_shape=jax.ShapeDtypeStruct((n, d), BF16),
        compiler_params=_params(("parallel", "parallel")),
        name="gate_merge",
    )(x_bf, *([w_in_bf] * N_BRANCH), *([o_stack] * N_BRANCH), *([w_branch_bf] * N_BRANCH))


def _layer_norm(y, g, b):
    mu = jnp.mean(y, axis=-1, keepdims=True)
    d = y - mu
    var = jnp.mean(d * d, axis=-1, keepdims=True)
    return d * lax.rsqrt(var + LN_EPS) * g + b


def _out_ln_kernel(m_ref, w_ref, x_ref, g_ref, b_ref, of_ref, ob_ref, *, alpha):
    k = pl.program_id(1)

    @pl.when(k == 0)
    def _():
        of_ref[...] = jnp.zeros(of_ref.shape, F32)

    of_ref[...] += _dot(m_ref[...], w_ref[...])

    @pl.when(k == pl.num_programs(1) - 1)
    def _():
        rows = min(of_ref.shape[0], LN_ROWS)

        def body(r, c):
            sl = pl.ds(pl.multiple_of(r * rows, rows), rows)
            out = _layer_norm(alpha * x_ref[sl, :] + of_ref[sl, :], g_ref[...], b_ref[...])
            of_ref[sl, :] = out
            ob_ref[sl, :] = out.astype(ob_ref.dtype)
            return c

        lax.fori_loop(0, of_ref.shape[0] // rows, body, 0)


LN_ROWS = 128


def out_proj_ln(merged, w_out_bf, x, g, b, layer, alpha, tm, tk):
    n, d = x.shape
    return pl.pallas_call(
        functools.partial(_out_ln_kernel, alpha=alpha),
        grid=(n // tm, d // tk),
        in_specs=[pl.BlockSpec((tm, tk), lambda m, k: (m, k)),
                  pl.BlockSpec((None, tk, d), lambda m, k: (layer, k, 0)),
                  pl.BlockSpec((tm, d), lambda m, k: (m, 0), pipeline_mode=pl.Buffered(1)),
                  pl.BlockSpec((None, 1, d), lambda m, k: (layer, 0, 0)),
                  pl.BlockSpec((None, 1, d), lambda m, k: (layer, 0, 0))],
        out_specs=[pl.BlockSpec((tm, d), lambda m, k: (m, 0)), pl.BlockSpec((tm, d), lambda m, k: (m, 0))],
        out_shape=[jax.ShapeDtypeStruct((n, d), F32), jax.ShapeDtypeStruct((n, d), BF16)],
        compiler_params=_params(("parallel", "arbitrary")),
        name="out_proj_ln",
    )(merged, w_out_bf, x, g, b)


def _router_kernel(x_ref, whi_ref, wlo_ref, b_ref, o_ref):
    x = x_ref[...]
    x_hi = x.astype(BF16)
    x_lo = (x - x_hi.astype(F32)).astype(BF16)
    whi = whi_ref[...]
    logits = _dot(x_hi, whi) + _dot(x_lo, whi) + _dot(x_hi, wlo_ref[...]) + b_ref[...]
    lane = lax.broadcasted_iota(jnp.int32, logits.shape, 1)
    big = jnp.int32(LANES)
    gmask = lane < N_GROUPS
    gl = jnp.where(gmask, logits, NEG_BIG)
    gmax = jnp.max(gl, axis=-1, keepdims=True)
    g_top = jnp.min(jnp.where(gmask & (gl == gmax), lane, big), axis=-1, keepdims=True)
    g_w = 1.0 / jnp.sum(jnp.where(gmask, jnp.exp(gl - gmax), 0.0), axis=-1, keepdims=True)
    lo = N_GROUPS + g_top * EXPERTS_PER_GROUP
    emask = (lane >= lo) & (lane < lo + EXPERTS_PER_GROUP)
    e1 = jnp.where(emask, logits, NEG_BIG)
    v1 = jnp.max(e1, axis=-1, keepdims=True)
    i1 = jnp.min(jnp.where(emask & (e1 == v1), lane, big), axis=-1, keepdims=True)
    emask2 = emask & (lane != i1)
    e2 = jnp.where(emask2, logits, NEG_BIG)
    v2 = jnp.max(e2, axis=-1, keepdims=True)
    i2 = jnp.min(jnp.where(emask2 & (e2 == v2), lane, big), axis=-1, keepdims=True)
    t = jnp.exp(v2 - v1)
    w1 = g_w / (1.0 + t)
    w2 = g_w * t / (1.0 + t)
    o_ref[...] = (jnp.where(lane == 0, (i1 - N_GROUPS).astype(F32), 0.0)
                  + jnp.where(lane == 1, (i2 - N_GROUPS).astype(F32), 0.0)
                  + jnp.where(lane == 2, w1, 0.0) + jnp.where(lane == 3, w2, 0.0))


def router(x, w_hi, w_lo, bias, tm):
    n, d = x.shape
    return pl.pallas_call(
        _router_kernel,
        grid=(n // tm,),
        in_specs=[pl.BlockSpec((tm, d), lambda m: (m, 0)),
                  pl.BlockSpec((d, LANES), lambda m: (0, 0)),
                  pl.BlockSpec((d, LANES), lambda m: (0, 0)),
                  pl.BlockSpec((1, LANES), lambda m: (0, 0))],
        out_specs=pl.BlockSpec((tm, LANES), lambda m: (m, 0)),
        out_shape=jax.ShapeDtypeStruct((n, LANES), F32),
        compiler_params=_params(("parallel",)),
        name="router",
    )(x, w_hi, w_lo, bias)


MOE_TM = 256


def moe_routing(route, n_tok, tm):
    n_pairs = 2 * n_tok
    e = route[:, 0:2].astype(jnp.int32).reshape(n_pairs)
    w = route[:, 2:4].reshape(n_pairs)
    counts = jnp.sum((e[:, None] == jnp.arange(N_EXPERTS)[None, :]).astype(jnp.int32), axis=0)
    tiles_per = (counts + tm - 1) // tm
    tile_end = jnp.cumsum(tiles_per)
    n_tiles = tile_end[-1]
    t_max = n_pairs // tm + N_EXPERTS
    tile_expert = jnp.minimum(jnp.searchsorted(tile_end, jnp.arange(t_max), side="right"), N_EXPERTS - 1).astype(jnp.int32)
    pad_exp = jnp.repeat(jnp.arange(N_EXPERTS, dtype=jnp.int32), tm)
    pad_rank = jnp.tile(jnp.arange(tm, dtype=jnp.int32), N_EXPERTS)
    pad_key = jnp.where(pad_rank < (tiles_per * tm - counts)[pad_exp], 2 * pad_exp + 1, 2 * N_EXPERTS)
    keys = jnp.concatenate([2 * e, pad_key])
    ident = jnp.concatenate([jnp.arange(n_pairs, dtype=jnp.int32),
                             n_pairs + jnp.arange(N_EXPERTS * tm, dtype=jnp.int32)])
    _, ident, wslot = lax.sort((keys, ident, jnp.concatenate([w, jnp.zeros((N_EXPERTS * tm,), w.dtype)])),
                               num_keys=1, is_stable=True)
    valid = ident < n_pairs
    tok = jnp.where(valid, ident // 2, 0)
    dst = jnp.where(valid, (ident % 2) * n_tok + ident // 2, ident)
    return (tile_expert, n_tiles.astype(jnp.int32)[None], tok.astype(jnp.int32), dst.astype(jnp.int32),
            jnp.broadcast_to(wslot[:, None], (t_max * tm, LANES)))


def _moe_kernel(te_ref, nt_ref, tok_ref, dst_ref, x_hbm, w_ref, wgu_ref, wd_ref, out_hbm,
                xbuf, ybuf, gsem, ssem):
    t = pl.program_id(0)
    nt = nt_ref[0]
    tm = xbuf.shape[1]
    f = wd_ref.shape[0]

    def gather_row(tile, r, slot):
        tok = tok_ref[tile * tm + r]
        return pltpu.make_async_copy(x_hbm.at[pl.ds(tok, 1), :], xbuf.at[slot, pl.ds(r, 1), :], gsem.at[slot])

    def scatter_row(tile, r, slot):
        d = dst_ref[tile * tm + r]
        return pltpu.make_async_copy(ybuf.at[slot, pl.ds(r, 1), :], out_hbm.at[pl.ds(d, 1), :], ssem.at[slot])

    def for_rows(fn):
        def body(r, c):
            fn(r)
            return c
        lax.fori_loop(0, tm, body, 0, unroll=8)

    def start_rows(copy_of_row):
        def body(r2, c):
            copy_of_row(2 * r2).start(priority=0)
            copy_of_row(2 * r2 + 1).start(priority=1)
            return c
        lax.fori_loop(0, tm // 2, body, 0, unroll=4)

    @pl.when((t == 0) & (nt > 0))
    def _():
        start_rows(lambda r: gather_row(0, r, 0))

    @pl.when(t + 1 < nt)
    def _():
        start_rows(lambda r: gather_row(t + 1, r, (t + 1) % 2))

    @pl.when(t < nt)
    def _():
        slot = t % 2
        for_rows(lambda r: gather_row(t, r, slot).wait())
        xb = xbuf[slot].astype(BF16)
        gu = _dot(xb, wgu_ref[...])
        gate = gu[:, :f]
        w = w_ref[...]
        hmid = (gate / (1.0 + jnp.exp(-gate))) * gu[:, f:] * jnp.concatenate([w] * (f // LANES), axis=1)
        y = _dot(hmid.astype(BF16), wd_ref[...])

        @pl.when(t >= 2)
        def _():
            for_rows(lambda r: scatter_row(t - 2, r, slot).wait())

        ybuf[slot] = y
        start_rows(lambda r: scatter_row(t, r, slot))

    @pl.when(t == pl.num_programs(0) - 1)
    def _():
        for back in (1, 2):
            @pl.when(nt >= back)
            def _():
                for_rows(lambda r: scatter_row(nt - back, r, (nt - back) % 2).wait())


def moe_experts(x, routing, wgu_bf, wd_bf, layer, tm):
    n, d = x.shape
    f = wd_bf.shape[-2]
    tile_expert, n_tiles, tok, dst, wslot = routing
    t_max = tile_expert.shape[0]
    grid_spec = pltpu.PrefetchScalarGridSpec(
        num_scalar_prefetch=4,
        grid=(t_max,),
        in_specs=[pl.BlockSpec(memory_space=pl.ANY),
                  pl.BlockSpec((tm, LANES), lambda t, te, nt, tk, ds: (t, 0)),
                  pl.BlockSpec((None, None, d, 2 * f), lambda t, te, nt, tk, ds: (layer, te[t], 0, 0)),
                  pl.BlockSpec((None, None, f, d), lambda t, te, nt, tk, ds: (layer, te[t], 0, 0))],
        out_specs=pl.BlockSpec(memory_space=pl.ANY),
        scratch_shapes=[pltpu.VMEM((2, tm, d), F32), pltpu.VMEM((2, tm, d), F32),
                        pltpu.SemaphoreType.DMA((2,)), pltpu.SemaphoreType.DMA((2,))])
    return pl.pallas_call(
        _moe_kernel,
        grid_spec=grid_spec,
        out_shape=jax.ShapeDtypeStruct((t_max * tm, d), F32),
        compiler_params=_params(("arbitrary",)),
        name="moe_experts",
    )(tile_expert, n_tiles, tok, dst, x, wslot, wgu_bf, wd_bf)


def _pair_ln_kernel(y0_ref, y1_ref, x_ref, g_ref, b_ref, of_ref, ob_ref, *, alpha):
    out = _layer_norm(alpha * x_ref[...] + (y0_ref[...] + y1_ref[...]), g_ref[...], b_ref[...])
    of_ref[...] = out
    ob_ref[...] = out.astype(ob_ref.dtype)


def pair_sum_ln(y_pairs, x, g, b, layer, alpha, tm):
    n, d = x.shape
    nblk = n // tm
    return pl.pallas_call(
        functools.partial(_pair_ln_kernel, alpha=alpha),
        grid=(nblk,),
        in_specs=[pl.BlockSpec((tm, d), lambda m: (m, 0)),
                  pl.BlockSpec((tm, d), lambda m: (nblk + m, 0)),
                  pl.BlockSpec((tm, d), lambda m: (m, 0)),
                  pl.BlockSpec((None, 1, d), lambda m: (layer, 0, 0)),
                  pl.BlockSpec((None, 1, d), lambda m: (layer, 0, 0))],
        out_specs=[pl.BlockSpec((tm, d), lambda m: (m, 0)), pl.BlockSpec((tm, d), lambda m: (m, 0))],
        out_shape=[jax.ShapeDtypeStruct((n, d), F32), jax.ShapeDtypeStruct((n, d), BF16)],
        compiler_params=_params(("parallel",)),
        name="pair_sum_ln",
    )(y_pairs, y_pairs, x, g, b)


def _rope_tables(seq):
    t = jnp.arange(seq)
    row_id = (t // GRID_W).astype(F32)
    col_id = (t % GRID_W).astype(F32)
    nfreq = HEAD_DIM // 4
    inv = ROPE_THETA ** (-jnp.arange(nfreq, dtype=F32) / nfreq)
    ar = row_id[:, None] * inv
    ac = col_id[:, None] * inv
    cos = jnp.concatenate([jnp.cos(ar), jnp.cos(ar), jnp.cos(ac), jnp.cos(ac)], axis=-1)
    sin = jnp.concatenate([-jnp.sin(ar), jnp.sin(ar), -jnp.sin(ac), jnp.sin(ac)], axis=-1)
    return cos, sin


def _tile(n, pref):
    t = min(n, pref)
    while n % t:
        t //= 2
    return t


def kernel(x, w_in, attn_sink, qk_norm_q, qk_norm_k, ret_decay_fwd, ret_decay_bwd, ret_norm_g,
           diff_lambda_q1, diff_lambda_k1, diff_lambda_q2, diff_lambda_k2, diff_subln_g,
           w_branch, w_out, ln_mix_g, ln_mix_b, router_group, router_group_b, router_expert,
           router_expert_b, expert_w_gate, expert_w_up, expert_w_down, ln_ffn_g, ln_ffn_b):
    batch, seq, d = x.shape
    depth = w_in.shape[0]
    n = batch * seq
    alpha = (2.0 * depth) ** 0.25
    assert w_in.shape[2] == MIX_COLS + N_BRANCH * d and seq % BLOCK == 0 and seq % GRID_W == 0

    colscale = jnp.ones((w_in.shape[2],), F32)
    colscale = colscale.at[COL_AQ:COL_AQ + BRANCH_W].set(HEAD_DIM ** -0.5 * LOG2E)
    colscale = colscale.at[COL_DQ:COL_DQ + BRANCH_W].set(D_HALF ** -0.5 * LOG2E)
    w_in_bf = (w_in * colscale).astype(BF16)
    w_branch_bf = w_branch.astype(BF16)
    w_out_bf = w_out.astype(BF16)
    wgu_bf = jnp.concatenate([expert_w_gate.astype(BF16), expert_w_up.astype(BF16)], axis=-1)
    wd_bf = expert_w_down.astype(BF16)
    ln_mix_g3, ln_mix_b3 = ln_mix_g[:, None, :], ln_mix_b[:, None, :]
    ln_ffn_g3, ln_ffn_b3 = ln_ffn_g[:, None, :], ln_ffn_b[:, None, :]

    cos_t, sin_t = _rope_tables(seq)
    slopes_log2 = jnp.asarray([LOG2E * 2.0 ** (-(i + 1.0)) for i in range(N_HEADS)], F32)

    tm_big = _tile(n, 1024)
    xf = x.reshape(n, d)
    xb = xf.astype(BF16)
    for l in range(depth):
        h = in_proj(xb, w_in_bf, l, MIX_COLS, tm_big, 1024)
        o_all = attn_a(h, attn_sink[l].astype(F32) * LOG2E, batch, seq)
        gains = jnp.concatenate([jnp.tile(qk_norm_q[l], N_HEADS) * (HEAD_DIM ** -0.5 * LOG2E),
                                 jnp.tile(qk_norm_k[l], KV_HEADS)])[None, :]
        qt_b, k_b = qk_prep(h, gains, cos_t, sin_t, batch, seq, _tile(seq, 512))
        vt_b = h[:, COL_BV:COL_BV + KV_HEADS * HEAD_DIM].T
        o_all = attn_b(qt_b, k_b, vt_b, o_all, batch, seq, _tile(seq, 256), _tile(seq // 2, 512))
        tabs_f = retention_tables(ret_decay_fwd[l], HEAD_DIM ** -0.5, False)
        tabs_b = retention_tables(ret_decay_bwd[l], HEAD_DIM ** -0.5, True)
        o_all = retention(h, tabs_f, tabs_b, ret_norm_g[l][None, :], o_all, batch, seq)
        lam_init = 0.8 - 0.6 * math.exp(-0.3 * l)
        lam_params = jnp.stack([diff_lambda_q1[l], diff_lambda_k1[l], diff_lambda_q2[l], diff_lambda_k2[l]]).astype(F32)
        qt_d = h[:, COL_DQ:COL_DQ + BRANCH_W].T
        vt_d = h[:, COL_DV:COL_DV + BRANCH_W].T
        o_all = attn_d(qt_d, h, vt_d, o_all, slopes_log2, lam_params, diff_subln_g[l][:, None], lam_init, batch, seq,
                       _tile(seq, 512), _tile(seq // 2, 512))
        merged = gate_merge(xb, w_in_bf, o_all, w_branch_bf, l, tm_big, 256)
        xf, xb = out_proj_ln(merged, w_out_bf, xf, ln_mix_g3, ln_mix_b3, l, alpha, _tile(n, 512), _tile(d, 512))
        w_r = jnp.concatenate([router_group[l], router_expert[l]], axis=1)
        w_r = jnp.pad(w_r, ((0, 0), (0, LANES - w_r.shape[1])))
        w_r_hi = w_r.astype(BF16)
        w_r_lo = (w_r - w_r_hi.astype(F32)).astype(BF16)
        b_r = jnp.pad(jnp.concatenate([router_group_b[l], router_expert_b[l]]), (0, LANES - N_GROUPS - N_EXPERTS))[None, :]
        route = router(xf, w_r_hi, w_r_lo, b_r, _tile(n, 512))
        y_pairs = moe_experts(xf, moe_routing(route, n, MOE_TM), wgu_bf, wd_bf, l, MOE_TM)
        xf, xb = pair_sum_ln(y_pairs, xf, ln_ffn_g3, ln_ffn_b3, l, alpha, _tile(n, 256))
    return xf.reshape(batch, seq, d)
```
